```python
import math
import jax, jax.numpy as jnp
from jax import lax
import numpy as np

D_MODEL = 4096
BATCH = 2
SEQ = 8192
DEPTH = 1

PLE_DIM = 256
CONV_WIDTH = D_MODEL
CONV_GROUP = 128
N_HEADS = 32
HEAD_DIM = 128
ATTN_WIDTH = N_HEADS * HEAD_DIM
SHORT_K = 3
D_FF = ((8 * D_MODEL // 3 + 255) // 256) * 256
Q_BLOCK = 128
EPS = 1e-6
IN_SPLITS = [CONV_WIDTH, CONV_WIDTH, CONV_WIDTH,
             ATTN_WIDTH, ATTN_WIDTH, ATTN_WIDTH,
             N_HEADS, D_MODEL, D_MODEL]
IN_COLS = sum(IN_SPLITS)
IN_OFFSETS = [int(v) for v in np.cumsum(IN_SPLITS)[:-1]]
NEG_INF = -1e30

kernel_name = "hybrid_gated_conv_fox_block"


def rms_norm(x, g):
    xf = x.astype(jnp.float32)
    inv = lax.rsqrt(jnp.mean(xf * xf, axis=-1, keepdims=True) + EPS)
    return (xf * inv * g.astype(jnp.float32)).astype(x.dtype)


def causal_dwconv3(u, w, b=None):
    up = jnp.pad(u, ((0, 0), (SHORT_K - 1, 0), (0, 0)))
    s = u.shape[1]
    y = w[0] * up[:, 0:s] + w[1] * up[:, 1:s + 1] + w[2] * up[:, 2:s + 2]
    if b is not None:
        y = y + b
    return y


def fox_attention(q, k, v, log_f):
    b, s, h, dh = q.shape
    nb = s // Q_BLOCK
    scale = 1.0 / math.sqrt(dh)
    c = jnp.cumsum(log_f, axis=1)
    c_kT = jnp.transpose(c, (0, 2, 1))
    q_blocks = jnp.transpose(q.reshape(b, nb, Q_BLOCK, h, dh), (1, 0, 2, 3, 4))
    c_blocks = jnp.transpose(c.reshape(b, nb, Q_BLOCK, h), (1, 0, 3, 2))
    k_pos = jnp.arange(s)

    def one_block(args):
        i, q_i, c_i = args
        logits = jnp.einsum('bqhd,bkhd->bhqk', q_i, k,
                            preferred_element_type=jnp.float32) * scale
        logits = logits + c_i[..., :, None] - c_kT[:, :, None, :]
        q_pos = i * Q_BLOCK + jnp.arange(Q_BLOCK)
        mask = k_pos[None, :] <= q_pos[:, None]
        logits = jnp.where(mask[None, None], logits, NEG_INF)
        probs = jax.nn.softmax(logits, axis=-1)
        return jnp.einsum('bhqk,bkhd->bqhd', probs.astype(v.dtype), v)

    out = lax.map(one_block, (jnp.arange(nb), q_blocks, c_blocks))
    return jnp.transpose(out, (1, 0, 2, 3, 4)).reshape(b, s, h, dh)


def setup_inputs(seed: int = 0) -> dict:
    key = jax.random.key(seed)
    ks = jax.random.split(key, 24)
    f32 = jnp.float32

    def dense(k, fan_in, fan_out):
        return jax.random.normal(k, (DEPTH, fan_in, fan_out), f32) * (fan_in ** -0.5)

    def gain(k, n):
        return 1.0 + 0.02 * jax.random.normal(k, (DEPTH, n), f32)

    x = jax.random.normal(ks[0], (BATCH, SEQ, D_MODEL), f32)
    p = jax.random.normal(ks[1], (DEPTH, BATCH, SEQ, PLE_DIM), f32)
    forget_bias = jax.random.uniform(ks[4], (DEPTH, N_HEADS), f32, 1.0, 6.0)
    return {
        "x": x,
        "p": p,
        "norm_mix_pre": gain(ks[2], D_MODEL),
        "w_in": dense(ks[3], D_MODEL, IN_COLS),
        "forget_bias": forget_bias,
        "conv_mix_w": jax.random.normal(ks[5], (DEPTH, SHORT_K, CONV_WIDTH), f32) * (SHORT_K ** -0.5),
        "w_branch_conv": dense(ks[6], CONV_WIDTH, D_MODEL),
        "w_branch_attn": dense(ks[7], ATTN_WIDTH, D_MODEL),
        "w_out": dense(ks[8], D_MODEL, D_MODEL),
        "norm_mix_post": gain(ks[9], D_MODEL),
        "norm_ffn_pre": gain(ks[10], D_MODEL),
        "w_up": dense(ks[11], D_MODEL, 2 * D_FF),
        "ffn_conv_w": jax.random.normal(ks[12], (DEPTH, SHORT_K, 2 * D_FF), f32) * (SHORT_K ** -0.5),
        "ffn_conv_b": 0.02 * jax.random.normal(ks[13], (DEPTH, 2 * D_FF), f32),
        "w_down": dense(ks[14], D_FF, D_MODEL),
        "norm_ffn_post": gain(ks[15], D_MODEL),
        "w_ple_proj": dense(ks[16], PLE_DIM, D_MODEL),
        "norm_ple_gate": gain(ks[17], D_MODEL),
        "w_ple_gate": dense(ks[18], D_MODEL, D_MODEL),
        "norm_ple_post": gain(ks[19], D_MODEL),
    }


def reference(x, p, norm_mix_pre, w_in, forget_bias, conv_mix_w, w_branch_conv,
              w_branch_attn, w_out, norm_mix_post, norm_ffn_pre, w_up, ffn_conv_w,
              ffn_conv_b, w_down, norm_ffn_post, w_ple_proj, norm_ple_gate,
              w_ple_gate, norm_ple_post):
    b, s, _ = x.shape
    for i in range(DEPTH):
        h = rms_norm(x, norm_mix_pre[i])
        z = h @ w_in[i]
        c_b, c_c, c_v, q, k, v, f_logit, g_a, g_b = jnp.split(z, IN_OFFSETS, axis=-1)

        conv_out = causal_dwconv3(c_c * c_v, conv_mix_w[i])
        y_a = (c_b * conv_out) @ w_branch_conv[i]

        log_f = jax.nn.log_sigmoid(f_logit.astype(jnp.float32)
                                   + forget_bias[i].astype(jnp.float32))
        o = fox_attention(q.reshape(b, s, N_HEADS, HEAD_DIM),
                          k.reshape(b, s, N_HEADS, HEAD_DIM),
                          v.reshape(b, s, N_HEADS, HEAD_DIM), log_f)
        y_b = o.reshape(b, s, ATTN_WIDTH) @ w_branch_attn[i]

        merged = jax.nn.sigmoid(g_a) * y_a + jax.nn.sigmoid(g_b) * y_b
        x = x + rms_norm(merged @ w_out[i], norm_mix_post[i])

        h = rms_norm(x, norm_ffn_pre[i])
        u = causal_dwconv3(h @ w_up[i], ffn_conv_w[i], ffn_conv_b[i])
        u_gate, u_val = jnp.split(u, 2, axis=-1)
        ffn = (jax.nn.gelu(u_gate, approximate=True) * u_val) @ w_down[i]
        x = x + rms_norm(ffn, norm_ffn_post[i])

        e = p[i] @ w_ple_proj[i]
        gate = jax.nn.sigmoid(rms_norm(x, norm_ple_gate[i]) @ w_ple_gate[i])
        x = x + rms_norm(gate * e, norm_ple_post[i])
    return x
```

```python
import functools
import math

import jax
import jax.numpy as jnp
from jax import lax
from jax.experimental import pallas as pl
from jax.experimental.pallas import tpu as pltpu

BF16 = jnp.bfloat16
F32 = jnp.float32

N_HEADS = 32
HEAD_DIM = 128
SHORT_K = 3
EPS = 1e-6
NEG_INF = -1e30
MIB = 1024 * 1024

ROW_TILE_NORM = 256
ROW_TILE_MM = 1024
ROW_TILE_WIDE_K = 512
ATTN_TILE = 512
CUMSUM_TILE = 512
CARRY_ROWS = 8


def _params(semantics, vmem_mib):
    return pltpu.CompilerParams(dimension_semantics=semantics, vmem_limit_bytes=vmem_mib * MIB)


def _dot(a, b):
    return jnp.dot(a, b, preferred_element_type=F32)


def _rms_scale(x, g):
    inv = lax.rsqrt(jnp.mean(x * x, axis=-1, keepdims=True) + EPS)
    return x * inv * g


def _norm_kernel(x_ref, g_ref, h_ref):
    h_ref[...] = _rms_scale(x_ref[...], g_ref[...]).astype(BF16)


def _residual_norm_kernel(x_ref, y_ref, g_post_ref, g_pre_ref, x_out_ref, h_ref):
    x_new = x_ref[...] + _rms_scale(y_ref[...], g_post_ref[...])
    x_out_ref[...] = x_new
    h_ref[...] = _rms_scale(x_new, g_pre_ref[...]).astype(BF16)


def _residual_kernel(x_ref, y_ref, g_post_ref, x_out_ref):
    x_out_ref[...] = x_ref[...] + _rms_scale(y_ref[...], g_post_ref[...])


def _row_spec(tr, d):
    return pl.BlockSpec((tr, d), lambda i: (i, 0))


def _gain_spec(d):
    return pl.BlockSpec((1, d), lambda i: (0, 0))


def _norm(x, g):
    t, d = x.shape
    tr = ROW_TILE_NORM
    return pl.pallas_call(
        _norm_kernel,
        grid=(t // tr,),
        in_specs=[_row_spec(tr, d), _gain_spec(d)],
        out_specs=_row_spec(tr, d),
        out_shape=jax.ShapeDtypeStruct((t, d), BF16),
        compiler_params=_params(("parallel",), 40),
        name="norm",
    )(x, g.reshape(1, d))


def _residual_norm(x, y, g_post, g_pre):
    t, d = x.shape
    tr = ROW_TILE_NORM
    return pl.pallas_call(
        _residual_norm_kernel,
        grid=(t // tr,),
        in_specs=[_row_spec(tr, d), _row_spec(tr, d), _gain_spec(d), _gain_spec(d)],
        out_specs=[_row_spec(tr, d), _row_spec(tr, d)],
        out_shape=[jax.ShapeDtypeStruct((t, d), F32), jax.ShapeDtypeStruct((t, d), BF16)],
        compiler_params=_params(("parallel",), 40),
        name="residual_norm",
    )(x, y, g_post.reshape(1, d), g_pre.reshape(1, d))


def _residual(x, y, g_post):
    t, d = x.shape
    tr = ROW_TILE_NORM
    return pl.pallas_call(
        _residual_kernel,
        grid=(t // tr,),
        in_specs=[_row_spec(tr, d), _row_spec(tr, d), _gain_spec(d)],
        out_specs=_row_spec(tr, d),
        out_shape=jax.ShapeDtypeStruct((t, d), F32),
        compiler_params=_params(("parallel",), 40),
        name="residual",
    )(x, y, g_post.reshape(1, d))


def _causal_conv3(u, carry_ref, w_ref):
    tm = u.shape[0]
    rows = lax.broadcasted_iota(jnp.int32, u.shape, 0)
    prev1 = carry_ref[CARRY_ROWS - 1:CARRY_ROWS, :]
    prev2 = carry_ref[CARRY_ROWS - 2:CARRY_ROWS - 1, :]
    back1 = jnp.where(rows == 0, prev1, pltpu.roll(u, 1, 0))
    back2 = jnp.where(rows == 0, prev2, jnp.where(rows == 1, prev1, pltpu.roll(u, 2, 0)))
    carry_ref[...] = u[tm - CARRY_ROWS:, :]
    return w_ref[0:1, :] * back2 + w_ref[1:2, :] * back1 + w_ref[2:3, :] * u


def _reset_carry_at_sequence_start(carry_refs, tiles_per_seq):
    @pl.when(pl.program_id(1) % tiles_per_seq == 0)
    def _():
        for ref in carry_refs:
            ref[...] = jnp.zeros_like(ref)


def _conv_branch_kernel(h_ref, wb_ref, wc_ref, wv_ref, cw_ref, o_ref, carry_ref, *, tiles_per_seq):
    _reset_carry_at_sequence_start([carry_ref], tiles_per_seq)
    h = h_ref[...]
    u = _dot(h, wc_ref[...]) * _dot(h, wv_ref[...])
    y = _causal_conv3(u, carry_ref, cw_ref)
    o_ref[...] = (_dot(h, wb_ref[...]) * y).astype(BF16)


def _conv_branch(h, w_conv, conv_w, seq):
    t, d = h.shape
    width = w_conv.shape[1] // 3
    tm, tn = ROW_TILE_MM, 256
    nb = width // tn
    return pl.pallas_call(
        functools.partial(_conv_branch_kernel, tiles_per_seq=seq // tm),
        grid=(nb, t // tm),
        in_specs=[
            pl.BlockSpec((tm, d), lambda j, i: (i, 0)),
            pl.BlockSpec((d, tn), lambda j, i: (0, j)),
            pl.BlockSpec((d, tn), lambda j, i: (0, j + nb)),
            pl.BlockSpec((d, tn), lambda j, i: (0, j + 2 * nb)),
            pl.BlockSpec((SHORT_K, tn), lambda j, i: (0, j)),
        ],
        out_specs=pl.BlockSpec((tm, tn), lambda j, i: (i, j)),
        out_shape=jax.ShapeDtypeStruct((t, width), BF16),
        scratch_shapes=[pltpu.VMEM((CARRY_ROWS, tn), F32)],
        compiler_params=_params(("parallel", "arbitrary"), 48),
        name="conv_branch",
    )(h, w_conv, w_conv, w_conv, conv_w)


def _qkv_kernel(h_ref, w_ref, o_ref, *, q_tiles, scale):
    acc = _dot(h_ref[...], w_ref[...])
    factor = jnp.where(pl.program_id(0) < q_tiles, scale, 1.0).astype(F32)
    o_ref[...] = (acc * factor).astype(BF16)


def _qkv(h, w_qkv):
    t, d = h.shape
    n = w_qkv.shape[1]
    tm, tn = ROW_TILE_MM, 512
    kern = functools.partial(_qkv_kernel, q_tiles=(n // 3) // tn, scale=1.0 / math.sqrt(HEAD_DIM))
    return pl.pallas_call(
        kern,
        grid=(n // tn, t // tm),
        in_specs=[pl.BlockSpec((tm, d), lambda j, i: (i, 0)),
                  pl.BlockSpec((d, tn), lambda j, i: (0, j))],
        out_specs=pl.BlockSpec((tm, tn), lambda j, i: (i, j)),
        out_shape=jax.ShapeDtypeStruct((t, n), BF16),
        compiler_params=_params(("parallel", "arbitrary"), 48),
        name="qkv",
    )(h, w_qkv)


def _sigmoid_proj_kernel(h_ref, w_ref, o_ref):
    o_ref[...] = jax.nn.sigmoid(_dot(h_ref[...], w_ref[...]))


def _sigmoid_proj(h, w):
    t, d = h.shape
    n = w.shape[1]
    tm, tn = ROW_TILE_MM, 512
    return pl.pallas_call(
        _sigmoid_proj_kernel,
        grid=(n // tn, t // tm),
        in_specs=[pl.BlockSpec((tm, d), lambda j, i: (i, 0)),
                  pl.BlockSpec((d, tn), lambda j, i: (0, j))],
        out_specs=pl.BlockSpec((tm, tn), lambda j, i: (i, j)),
        out_shape=jax.ShapeDtypeStruct((t, n), F32),
        compiler_params=_params(("parallel", "arbitrary"), 48),
        name="gates",
    )(h, w)


def _matmul_kernel(a_ref, w_ref, o_ref):
    o_ref[...] = _dot(a_ref[...], w_ref[...])


def _matmul(a, w, tm, tn, name):
    t, k = a.shape
    n = w.shape[1]
    return pl.pallas_call(
        _matmul_kernel,
        grid=(n // tn, t // tm),
        in_specs=[pl.BlockSpec((tm, k), lambda j, i: (i, 0)),
                  pl.BlockSpec((k, tn), lambda j, i: (0, j))],
        out_specs=pl.BlockSpec((tm, tn), lambda j, i: (i, j)),
        out_shape=jax.ShapeDtypeStruct((t, n), F32),
        compiler_params=_params(("parallel", "arbitrary"), 52),
        name=name,
    )(a, w)


def _merge_kernel(a_ref, o_ref, wa_ref, wb_ref, ga_ref, gb_ref, out_ref):
    y_a = _dot(a_ref[...], wa_ref[...])
    y_b = _dot(o_ref[...], wb_ref[...])
    out_ref[...] = (ga_ref[...] * y_a + gb_ref[...] * y_b).astype(BF16)


def _merge(a, o, w_a, w_b, gates):
    t, d = a.shape
    n = w_a.shape[1]
    tm, tn = ROW_TILE_WIDE_K, 512
    nb = n // tn
    return pl.pallas_call(
        _merge_kernel,
        grid=(nb, t // tm),
        in_specs=[
            pl.BlockSpec((tm, d), lambda j, i: (i, 0)),
            pl.BlockSpec((tm, d), lambda j, i: (i, 0)),
            pl.BlockSpec((d, tn), lambda j, i: (0, j)),
            pl.BlockSpec((d, tn), lambda j, i: (0, j)),
            pl.BlockSpec((tm, tn), lambda j, i: (i, j)),
            pl.BlockSpec((tm, tn), lambda j, i: (i, j + nb)),
        ],
        out_specs=pl.BlockSpec((tm, tn), lambda j, i: (i, j)),
        out_shape=jax.ShapeDtypeStruct((t, n), BF16),
        compiler_params=_params(("parallel", "arbitrary"), 48),
        name="merge",
    )(a, o, w_a, w_b, gates, gates)


def _ffn_up_kernel(h_ref, wg_ref, wv_ref, cwg_ref, cwv_ref, bg_ref, bv_ref, o_ref,
                   carry_g_ref, carry_v_ref, *, tiles_per_seq):
    _reset_carry_at_sequence_start([carry_g_ref, carry_v_ref], tiles_per_seq)
    h = h_ref[...]
    u_gate = _causal_conv3(_dot(h, wg_ref[...]), carry_g_ref, cwg_ref) + bg_ref[...]
    u_val = _causal_conv3(_dot(h, wv_ref[...]), carry_v_ref, cwv_ref) + bv_ref[...]
    o_ref[...] = (jax.nn.gelu(u_gate, approximate=True) * u_val).astype(BF16)


def _ffn_up(h, w_up, conv_w, conv_b, seq):
    t, d = h.shape
    d_ff = w_up.shape[1] // 2
    tm, tn = ROW_TILE_MM, 256
    nb = d_ff // tn
    conv_b = conv_b.reshape(1, 2 * d_ff)
    return pl.pallas_call(
        functools.partial(_ffn_up_kernel, tiles_per_seq=seq // tm),
        grid=(nb, t // tm),
        in_specs=[
            pl.BlockSpec((tm, d), lambda j, i: (i, 0)),
            pl.BlockSpec((d, tn), lambda j, i: (0, j)),
            pl.BlockSpec((d, tn), lambda j, i: (0, j + nb)),
            pl.BlockSpec((SHORT_K, tn), lambda j, i: (0, j)),
            pl.BlockSpec((SHORT_K, tn), lambda j, i: (0, j + nb)),
            pl.BlockSpec((1, tn), lambda j, i: (0, j)),
            pl.BlockSpec((1, tn), lambda j, i: (0, j + nb)),
        ],
        out_specs=pl.BlockSpec((tm, tn), lambda j, i: (i, j)),
        out_shape=jax.ShapeDtypeStruct((t, d_ff), BF16),
        scratch_shapes=[pltpu.VMEM((CARRY_ROWS, tn), F32), pltpu.VMEM((CARRY_ROWS, tn), F32)],
        compiler_params=_params(("parallel", "arbitrary"), 48),
        name="ffn_up",
    )(h, w_up, w_up, conv_w, conv_w, conv_b, conv_b)


def _ple_kernel(h_ref, p_ref, wg_ref, wp_ref, o_ref):
    gate = jax.nn.sigmoid(_dot(h_ref[...], wg_ref[...]))
    o_ref[...] = gate * _dot(p_ref[...].astype(BF16), wp_ref[...])


def _ple(h, p, w_gate, w_proj):
    t, d = h.shape
    n = w_gate.shape[1]
    ple = p.shape[1]
    tm, tn = ROW_TILE_MM, 512
    return pl.pallas_call(
        _ple_kernel,
        grid=(n // tn, t // tm),
        in_specs=[
            pl.BlockSpec((tm, d), lambda j, i: (i, 0)),
            pl.BlockSpec((tm, ple), lambda j, i: (i, 0)),
            pl.BlockSpec((d, tn), lambda j, i: (0, j)),
            pl.BlockSpec((ple, tn), lambda j, i: (0, j)),
        ],
        out_specs=pl.BlockSpec((tm, tn), lambda j, i: (i, j)),
        out_shape=jax.ShapeDtypeStruct((t, n), F32),
        compiler_params=_params(("parallel", "arbitrary"), 48),
        name="ple",
    )(h, p, w_gate, w_proj)


def _forget_cumsum_kernel(h_ref, wf_ref, b_ref, c_ref, carry_ref, *, tiles_per_seq):
    @pl.when(pl.program_id(0) % tiles_per_seq == 0)
    def _():
        carry_ref[...] = jnp.zeros_like(carry_ref)

    ts = h_ref.shape[0]
    log_f = jax.nn.log_sigmoid(_dot(h_ref[...], wf_ref[...]) + b_ref[...])
    row = lax.broadcasted_iota(jnp.int32, (ts, ts), 0)
    col = lax.broadcasted_iota(jnp.int32, (ts, ts), 1)
    lower = (col <= row).astype(F32)
    csum = jnp.dot(lower, log_f, preferred_element_type=F32,
                   precision=lax.Precision.HIGHEST) + carry_ref[...]
    c_ref[...] = csum
    carry_ref[...] = c_ref[ts - 1:ts, :]


def _forget_cumsum(h, w_f, bias, seq):
    t, d = h.shape
    nh = w_f.shape[1]
    ts = CUMSUM_TILE
    return pl.pallas_call(
        functools.partial(_forget_cumsum_kernel, tiles_per_seq=seq // ts),
        grid=(t // ts,),
        in_specs=[pl.BlockSpec((ts, d), lambda i: (i, 0)),
                  pl.BlockSpec((d, nh), lambda i: (0, 0)),
                  pl.BlockSpec((1, nh), lambda i: (0, 0))],
        out_specs=pl.BlockSpec((ts, nh), lambda i: (i, 0)),
        out_shape=jax.ShapeDtypeStruct((t, nh), F32),
        scratch_shapes=[pltpu.VMEM((1, nh), F32)],
        compiler_params=_params(("arbitrary",), 32),
        name="forget_cumsum",
    )(h, w_f, bias.reshape(1, nh))


def _attention_kernel(q_ref, k_ref, v_ref, cq_ref, ck_ref, o_ref, *, tile):
    qi = pl.program_id(2)
    q = q_ref[...]
    c_q = cq_ref[...]

    def scores(ki):
        start = pl.multiple_of(ki * tile, tile)
        k = k_ref[pl.ds(start, tile), :]
        s = lax.dot_general(q, k, (((1,), (1,)), ((), ())), preferred_element_type=F32)
        return s + c_q - ck_ref[ki], start

    def update(carry, s, start):
        m, l, acc = carry
        m_new = jnp.maximum(m, jnp.max(s, axis=1, keepdims=True))
        alpha = jnp.exp(m - m_new)
        p = jnp.exp(s - m_new)
        l = alpha * l + jnp.sum(p, axis=1, keepdims=True)
        v = v_ref[pl.ds(start, tile), :]
        acc = alpha * acc + _dot(p.astype(BF16), v)
        return m_new, l, acc

    def full_tile(ki, carry):
        s, start = scores(ki)
        return update(carry, s, start)

    init = (jnp.full((tile, 1), NEG_INF, F32), jnp.zeros((tile, 1), F32),
            jnp.zeros((tile, HEAD_DIM), F32))
    carry = lax.fori_loop(0, qi, full_tile, init)

    s, start = scores(qi)
    row = lax.broadcasted_iota(jnp.int32, (tile, tile), 0)
    col = lax.broadcasted_iota(jnp.int32, (tile, tile), 1)
    s = jnp.where(col <= row, s, NEG_INF)
    _, l, acc = update(carry, s, start)
    o_ref[...] = (acc / l).astype(BF16)


def _attention(qkv, c, batch, seq):
    t = qkv.shape[0]
    tile = ATTN_TILE
    nq = seq // tile
    c_heads = jnp.transpose(c.reshape(batch, seq, N_HEADS), (0, 2, 1)).reshape(batch * N_HEADS, seq)
    c_col = c_heads.reshape(batch * N_HEADS, seq, 1)
    c_row = c_heads.reshape(batch * N_HEADS, nq, 1, tile)
    return pl.pallas_call(
        functools.partial(_attention_kernel, tile=tile),
        grid=(batch, N_HEADS, nq),
        in_specs=[
            pl.BlockSpec((tile, HEAD_DIM), lambda b, h, i: (b * nq + i, h)),
            pl.BlockSpec((seq, HEAD_DIM), lambda b, h, i: (b, N_HEADS + h)),
            pl.BlockSpec((seq, HEAD_DIM), lambda b, h, i: (b, 2 * N_HEADS + h)),
            pl.BlockSpec((None, tile, 1), lambda b, h, i: (b * N_HEADS + h, i, 0)),
            pl.BlockSpec((None, nq, 1, tile), lambda b, h, i: (b * N_HEADS + h, 0, 0, 0)),
        ],
        out_specs=pl.BlockSpec((tile, HEAD_DIM), lambda b, h, i: (b * nq + i, h)),
        out_shape=jax.ShapeDtypeStruct((t, N_HEADS * HEAD_DIM), BF16),
        compiler_params=_params(("parallel", "parallel", "arbitrary"), 32),
        name="fox_attention",
    )(qkv, qkv, qkv, c_col, c_row)


def kernel(x, p, norm_mix_pre, w_in, forget_bias, conv_mix_w, w_branch_conv, w_branch_attn, w_out,
           norm_mix_post, norm_ffn_pre, w_up, ffn_conv_w, ffn_conv_b, w_down, norm_ffn_post,
           w_ple_proj, norm_ple_gate, w_ple_gate, norm_ple_post):
    batch, seq, d = x.shape
    t = batch * seq
    depth = w_in.shape[0]
    conv_width = conv_mix_w.shape[-1]
    attn_width = N_HEADS * HEAD_DIM
    qkv_start = 3 * conv_width
    forget_start = qkv_start + 3 * attn_width
    gate_start = forget_start + N_HEADS

    xs = x.reshape(t, d)
    for i in range(depth):
        w_in_i = w_in[i]
        w_conv = w_in_i[:, :qkv_start].astype(BF16)
        w_qkv = w_in_i[:, qkv_start:forget_start].astype(BF16)
        w_f = w_in_i[:, forget_start:gate_start].astype(BF16)
        w_gates = w_in_i[:, gate_start:].astype(BF16)

        h1 = _norm(xs, norm_mix_pre[i])
        a = _conv_branch(h1, w_conv, conv_mix_w[i], seq)
        qkv = _qkv(h1, w_qkv)
        gates = _sigmoid_proj(h1, w_gates)
        c = _forget_cumsum(h1, w_f, forget_bias[i], seq)
        o = _attention(qkv, c, batch, seq)
        merged = _merge(a, o, w_branch_conv[i].astype(BF16), w_branch_attn[i].astype(BF16), gates)
        y1 = _matmul(merged, w_out[i].astype(BF16), ROW_TILE_MM, 512, "out_proj")
        x1, h2 = _residual_norm(xs, y1, norm_mix_post[i], norm_ffn_pre[i])

        ffn = _ffn_up(h2, w_up[i].astype(BF16), ffn_conv_w[i], ffn_conv_b[i], seq)
        y2 = _matmul(ffn, w_down[i].astype(BF16), ROW_TILE_WIDE_K, 512, "ffn_down")
        x2, h3 = _residual_norm(x1, y2, norm_ffn_post[i], norm_ple_gate[i])

        ge = _ple(h3, p[i].reshape(t, -1), w_ple_gate[i].astype(BF16), w_ple_proj[i].astype(BF16))
        xs = _residual(x2, ge, norm_ple_post[i])
    return xs.reshape(batch, seq, d)
```

```python
import functools
import math

import jax
import jax.numpy as jnp
from jax import lax
from jax.experimental import pallas as pl
from jax.experimental.pallas import tpu as pltpu

BF16 = jnp.bfloat16
F32 = jnp.float32

N_HEADS = 32
HEAD_DIM = 128
SHORT_K = 3
EPS = 1e-6
NEG_INF = -1e30
LOG2_E = math.log2(math.e)
MIB = 1024 * 1024

ROW_TILE_NORM = 256
ROW_TILE_MM = 1024
ROW_TILE_WIDE_K = 512
ATTN_TILE = 512
CUMSUM_TILE = 512
CARRY_ROWS = 8


def _params(semantics, vmem_mib):
    return pltpu.CompilerParams(dimension_semantics=semantics, vmem_limit_bytes=vmem_mib * MIB)


def _dot(a, b):
    return jnp.dot(a, b, preferred_element_type=F32)


def _rms_scale(x, g):
    inv = lax.rsqrt(jnp.mean(x * x, axis=-1, keepdims=True) + EPS)
    return x * inv * g


def _norm_kernel(x_ref, g_ref, h_ref):
    h_ref[...] = _rms_scale(x_ref[...], g_ref[...]).astype(BF16)


def _residual_norm_kernel(x_ref, y_ref, g_post_ref, g_pre_ref, x_out_ref, h_ref):
    x_new = x_ref[...] + _rms_scale(y_ref[...], g_post_ref[...])
    x_out_ref[...] = x_new
    h_ref[...] = _rms_scale(x_new, g_pre_ref[...]).astype(BF16)


def _residual_kernel(x_ref, y_ref, g_post_ref, x_out_ref):
    x_out_ref[...] = x_ref[...] + _rms_scale(y_ref[...], g_post_ref[...])


def _row_spec(tr, d):
    return pl.BlockSpec((tr, d), lambda i: (i, 0))


def _gain_spec(d):
    return pl.BlockSpec((1, d), lambda i: (0, 0))


def _norm(x, g):
    t, d = x.shape
    tr = ROW_TILE_NORM
    return pl.pallas_call(
        _norm_kernel,
        grid=(t // tr,),
        in_specs=[_row_spec(tr, d), _gain_spec(d)],
        out_specs=_row_spec(tr, d),
        out_shape=jax.ShapeDtypeStruct((t, d), BF16),
        compiler_params=_params(("parallel",), 40),
        name="norm",
    )(x, g.reshape(1, d))


def _residual_norm(x, y, g_post, g_pre):
    t, d = x.shape
    tr = ROW_TILE_NORM
    return pl.pallas_call(
        _residual_norm_kernel,
        grid=(t // tr,),
        in_specs=[_row_spec(tr, d), _row_spec(tr, d), _gain_spec(d), _gain_spec(d)],
        out_specs=[_row_spec(tr, d), _row_spec(tr, d)],
        out_shape=[jax.ShapeDtypeStruct((t, d), F32), jax.ShapeDtypeStruct((t, d), BF16)],
        compiler_params=_params(("parallel",), 40),
        name="residual_norm",
    )(x, y, g_post.reshape(1, d), g_pre.reshape(1, d))


def _residual(x, y, g_post):
    t, d = x.shape
    tr = ROW_TILE_NORM
    return pl.pallas_call(
        _residual_kernel,
        grid=(t // tr,),
        in_specs=[_row_spec(tr, d), _row_spec(tr, d), _gain_spec(d)],
        out_specs=_row_spec(tr, d),
        out_shape=jax.ShapeDtypeStruct((t, d), F32),
        compiler_params=_params(("parallel",), 40),
        name="residual",
    )(x, y, g_post.reshape(1, d))


def _causal_conv3(u, carry_ref, w_ref):
    tm = u.shape[0]
    rows = lax.broadcasted_iota(jnp.int32, u.shape, 0)
    prev1 = carry_ref[CARRY_ROWS - 1:CARRY_ROWS, :]
    prev2 = carry_ref[CARRY_ROWS - 2:CARRY_ROWS - 1, :]
    back1 = jnp.where(rows == 0, prev1, pltpu.roll(u, 1, 0))
    back2 = jnp.where(rows == 0, prev2, jnp.where(rows == 1, prev1, pltpu.roll(u, 2, 0)))
    carry_ref[...] = u[tm - CARRY_ROWS:, :]
    return w_ref[0:1, :] * back2 + w_ref[1:2, :] * back1 + w_ref[2:3, :] * u


def _reset_carry_at_sequence_start(carry_refs, tiles_per_seq):
    @pl.when(pl.program_id(1) % tiles_per_seq == 0)
    def _():
        for ref in carry_refs:
            ref[...] = jnp.zeros_like(ref)


def _conv_branch_kernel(h_ref, wb_ref, wc_ref, wv_ref, cw_ref, o_ref, carry_ref, *, tiles_per_seq):
    _reset_carry_at_sequence_start([carry_ref], tiles_per_seq)
    h = h_ref[...]
    u = _dot(h, wc_ref[...]) * _dot(h, wv_ref[...])
    y = _causal_conv3(u, carry_ref, cw_ref)
    o_ref[...] = (_dot(h, wb_ref[...]) * y).astype(BF16)


def _conv_branch(h, w_conv, conv_w, seq):
    t, d = h.shape
    width = w_conv.shape[1] // 3
    tm, tn = ROW_TILE_MM, 256
    nb = width // tn
    return pl.pallas_call(
        functools.partial(_conv_branch_kernel, tiles_per_seq=seq // tm),
        grid=(nb, t // tm),
        in_specs=[
            pl.BlockSpec((tm, d), lambda j, i: (i, 0)),
            pl.BlockSpec((d, tn), lambda j, i: (0, j)),
            pl.BlockSpec((d, tn), lambda j, i: (0, j + nb)),
            pl.BlockSpec((d, tn), lambda j, i: (0, j + 2 * nb)),
            pl.BlockSpec((SHORT_K, tn), lambda j, i: (0, j)),
        ],
        out_specs=pl.BlockSpec((tm, tn), lambda j, i: (i, j)),
        out_shape=jax.ShapeDtypeStruct((t, width), BF16),
        scratch_shapes=[pltpu.VMEM((CARRY_ROWS, tn), F32)],
        compiler_params=_params(("parallel", "arbitrary"), 48),
        name="conv_branch",
    )(h, w_conv, w_conv, w_conv, conv_w)


def _qkv_kernel(h_ref, w_ref, o_ref, *, q_tiles, scale):
    acc = _dot(h_ref[...], w_ref[...])
    factor = jnp.where(pl.program_id(0) < q_tiles, scale, 1.0).astype(F32)
    o_ref[...] = (acc * factor).astype(BF16)


def _qkv(h, w_qkv):
    t, d = h.shape
    n = w_qkv.shape[1]
    tm, tn = ROW_TILE_MM, 512
    kern = functools.partial(_qkv_kernel, q_tiles=(n // 3) // tn, scale=LOG2_E / math.sqrt(HEAD_DIM))
    return pl.pallas_call(
        kern,
        grid=(n // tn, t // tm),
        in_specs=[pl.BlockSpec((tm, d), lambda j, i: (i, 0)),
                  pl.BlockSpec((d, tn), lambda j, i: (0, j))],
        out_specs=pl.BlockSpec((tm, tn), lambda j, i: (i, j)),
        out_shape=jax.ShapeDtypeStruct((t, n), BF16),
        compiler_params=_params(("parallel", "arbitrary"), 48),
        name="qkv",
    )(h, w_qkv)


def _sigmoid_proj_kernel(h_ref, w_ref, o_ref):
    o_ref[...] = jax.nn.sigmoid(_dot(h_ref[...], w_ref[...]))


def _sigmoid_proj(h, w):
    t, d = h.shape
    n = w.shape[1]
    tm, tn = ROW_TILE_MM, 512
    return pl.pallas_call(
        _sigmoid_proj_kernel,
        grid=(n // tn, t // tm),
        in_specs=[pl.BlockSpec((tm, d), lambda j, i: (i, 0)),
                  pl.BlockSpec((d, tn), lambda j, i: (0, j))],
        out_specs=pl.BlockSpec((tm, tn), lambda j, i: (i, j)),
        out_shape=jax.ShapeDtypeStruct((t, n), F32),
        compiler_params=_params(("parallel", "arbitrary"), 48),
        name="gates",
    )(h, w)


def _matmul_kernel(a_ref, w_ref, o_ref):
    o_ref[...] = _dot(a_ref[...], w_ref[...])


def _matmul(a, w, tm, tn, name):
    t, k = a.shape
    n = w.shape[1]
    return pl.pallas_call(
        _matmul_kernel,
        grid=(n // tn, t // tm),
        in_specs=[pl.BlockSpec((tm, k), lambda j, i: (i, 0)),
                  pl.BlockSpec((k, tn), lambda j, i: (0, j))],
        out_specs=pl.BlockSpec((tm, tn), lambda j, i: (i, j)),
        out_shape=jax.ShapeDtypeStruct((t, n), F32),
        compiler_params=_params(("parallel", "arbitrary"), 52),
        name=name,
    )(a, w)


def _merge_kernel(a_ref, o_ref, wa_ref, wb_ref, ga_ref, gb_ref, out_ref):
    y_a = _dot(a_ref[...], wa_ref[...])
    y_b = _dot(o_ref[...], wb_ref[...])
    out_ref[...] = (ga_ref[...] * y_a + gb_ref[...] * y_b).astype(BF16)


def _merge(a, o, w_a, w_b, gates):
    t, d = a.shape
    n = w_a.shape[1]
    tm, tn = ROW_TILE_WIDE_K, 512
    nb = n // tn
    return pl.pallas_call(
        _merge_kernel,
        grid=(nb, t // tm),
        in_specs=[
            pl.BlockSpec((tm, d), lambda j, i: (i, 0)),
            pl.BlockSpec((tm, d), lambda j, i: (i, 0)),
            pl.BlockSpec((d, tn), lambda j, i: (0, j)),
            pl.BlockSpec((d, tn), lambda j, i: (0, j)),
            pl.BlockSpec((tm, tn), lambda j, i: (i, j)),
            pl.BlockSpec((tm, tn), lambda j, i: (i, j + nb)),
        ],
        out_specs=pl.BlockSpec((tm, tn), lambda j, i: (i, j)),
        out_shape=jax.ShapeDtypeStruct((t, n), BF16),
        compiler_params=_params(("parallel", "arbitrary"), 48),
        name="merge",
    )(a, o, w_a, w_b, gates, gates)


def _ffn_up_kernel(h_ref, wg_ref, wv_ref, cwg_ref, cwv_ref, bg_ref, bv_ref, o_ref,
                   carry_g_ref, carry_v_ref, *, tiles_per_seq):
    _reset_carry_at_sequence_start([carry_g_ref, carry_v_ref], tiles_per_seq)
    h = h_ref[...]
    u_gate = _causal_conv3(_dot(h, wg_ref[...]), carry_g_ref, cwg_ref) + bg_ref[...]
    u_val = _causal_conv3(_dot(h, wv_ref[...]), carry_v_ref, cwv_ref) + bv_ref[...]
    o_ref[...] = (jax.nn.gelu(u_gate, approximate=True) * u_val).astype(BF16)


def _ffn_up(h, w_up, conv_w, conv_b, seq):
    t, d = h.shape
    d_ff = w_up.shape[1] // 2
    tm, tn = ROW_TILE_MM, 256
    nb = d_ff // tn
    conv_b = conv_b.reshape(1, 2 * d_ff)
    return pl.pallas_call(
        functools.partial(_ffn_up_kernel, tiles_per_seq=seq // tm),
        grid=(nb, t // tm),
        in_specs=[
            pl.BlockSpec((tm, d), lambda j, i: (i, 0)),
            pl.BlockSpec((d, tn), lambda j, i: (0, j)),
            pl.BlockSpec((d, tn), lambda j, i: (0, j + nb)),
            pl.BlockSpec((SHORT_K, tn), lambda j, i: (0, j)),
            pl.BlockSpec((SHORT_K, tn), lambda j, i: (0, j + nb)),
            pl.BlockSpec((1, tn), lambda j, i: (0, j)),
            pl.BlockSpec((1, tn), lambda j, i: (0, j + nb)),
        ],
        out_specs=pl.BlockSpec((tm, tn), lambda j, i: (i, j)),
        out_shape=jax.ShapeDtypeStruct((t, d_ff), BF16),
        scratch_shapes=[pltpu.VMEM((CARRY_ROWS, tn), F32), pltpu.VMEM((CARRY_ROWS, tn), F32)],
        compiler_params=_params(("parallel", "arbitrary"), 48),
        name="ffn_up",
    )(h, w_up, w_up, conv_w, conv_w, conv_b, conv_b)


def _ple_kernel(h_ref, p_ref, wg_ref, wp_ref, o_ref):
    gate = jax.nn.sigmoid(_dot(h_ref[...], wg_ref[...]))
    o_ref[...] = gate * _dot(p_ref[...].astype(BF16), wp_ref[...])


def _ple(h, p, w_gate, w_proj):
    t, d = h.shape
    n = w_gate.shape[1]
    ple = p.shape[1]
    tm, tn = ROW_TILE_MM, 512
    return pl.pallas_call(
        _ple_kernel,
        grid=(n // tn, t // tm),
        in_specs=[
            pl.BlockSpec((tm, d), lambda j, i: (i, 0)),
            pl.BlockSpec((tm, ple), lambda j, i: (i, 0)),
            pl.BlockSpec((d, tn), lambda j, i: (0, j)),
            pl.BlockSpec((ple, tn), lambda j, i: (0, j)),
        ],
        out_specs=pl.BlockSpec((tm, tn), lambda j, i: (i, j)),
        out_shape=jax.ShapeDtypeStruct((t, n), F32),
        compiler_params=_params(("parallel", "arbitrary"), 48),
        name="ple",
    )(h, p, w_gate, w_proj)


def _split_bf16x3(x):
    hi = x.astype(BF16)
    rest = x - hi.astype(F32)
    mid = rest.astype(BF16)
    lo = (rest - mid.astype(F32)).astype(BF16)
    return hi, mid, lo


def _forget_cumsum_kernel(h_ref, wf_ref, b_ref, spread_ref, ones_ref, c_ref, augk_ref, carry_ref,
                          *, tiles_per_seq):
    @pl.when(pl.program_id(0) % tiles_per_seq == 0)
    def _():
        carry_ref[...] = jnp.zeros_like(carry_ref)

    ts = h_ref.shape[0]
    log_f = jax.nn.log_sigmoid(_dot(h_ref[...], wf_ref[...]) + b_ref[...])
    row = lax.broadcasted_iota(jnp.int32, (ts, ts), 0)
    col = lax.broadcasted_iota(jnp.int32, (ts, ts), 1)
    lower = (col <= row).astype(F32)
    csum = jnp.dot(lower, log_f, preferred_element_type=F32,
                   precision=lax.Precision.HIGHEST) + carry_ref[...]
    c_ref[...] = csum
    carry_ref[...] = c_ref[ts - 1:ts, :]

    hi, mid, lo = _split_bf16x3(csum * LOG2_E)
    aug = (_dot(hi, spread_ref[0]) + _dot(mid, spread_ref[1]) + _dot(lo, spread_ref[2])
           + ones_ref[...])
    augk_ref[...] = aug.astype(BF16)


def _forget_cumsum(h, w_f, bias, seq):
    t, d = h.shape
    nh = w_f.shape[1]
    ts = CUMSUM_TILE
    width = nh * HEAD_DIM
    lane = jnp.arange(width)[None, :]
    head = jnp.arange(nh)[:, None]
    spread = jnp.stack([jnp.where(lane == head * HEAD_DIM + j, -1.0, 0.0) for j in range(3)])
    ones = jnp.where((lane % HEAD_DIM >= 3) & (lane % HEAD_DIM < 6), 1.0, 0.0).astype(F32)
    return pl.pallas_call(
        functools.partial(_forget_cumsum_kernel, tiles_per_seq=seq // ts),
        grid=(t // ts,),
        in_specs=[pl.BlockSpec((ts, d), lambda i: (i, 0)),
                  pl.BlockSpec((d, nh), lambda i: (0, 0)),
                  pl.BlockSpec((1, nh), lambda i: (0, 0)),
                  pl.BlockSpec((3, nh, width), lambda i: (0, 0, 0)),
                  pl.BlockSpec((1, width), lambda i: (0, 0))],
        out_specs=[pl.BlockSpec((ts, nh), lambda i: (i, 0)),
                   pl.BlockSpec((ts, width), lambda i: (i, 0))],
        out_shape=[jax.ShapeDtypeStruct((t, nh), F32), jax.ShapeDtypeStruct((t, width), BF16)],
        scratch_shapes=[pltpu.VMEM((1, nh), F32)],
        compiler_params=_params(("arbitrary",), 48),
        name="forget_cumsum",
    )(h, w_f, bias.reshape(1, nh), spread.astype(BF16), ones)


AUG_ROWS = 16


def _attention_kernel(q_ref, k_ref, v_ref, augk_ref, c_ref, o_ref,
                      kx_ref, vt_ref, qxt_ref, s_a, s_b, p_a, p_b, acc_ref, *, tile):
    kt = tile // 2
    qi = pl.program_id(2)
    n_key_tiles = kx_ref.shape[0]

    @pl.when(qi == 0)
    def _():
        def stage(j, _):
            rows = pl.ds(pl.multiple_of(j * kt, kt), kt)
            kx_ref[j, :, 0:HEAD_DIM] = k_ref[rows, :]
            kx_ref[j, :, HEAD_DIM:] = augk_ref[rows, :]
            vt_ref[j] = v_ref[rows, :].astype(F32).T.astype(BF16)
            return 0
        lax.fori_loop(0, n_key_tiles, stage, 0)
        qxt_ref[HEAD_DIM + AUG_ROWS:, :] = jnp.zeros((HEAD_DIM - AUG_ROWS, tile), BF16)

    qxt_ref[0:HEAD_DIM, :] = q_ref[...].astype(F32).T.astype(BF16)
    hi, mid, lo = _split_bf16x3(c_ref[qi] * LOG2_E)
    r = lax.broadcasted_iota(jnp.int32, (AUG_ROWS, tile), 0)
    aug = jnp.where(r < 3, 1.0,
                    jnp.where(r == 3, hi.astype(F32),
                              jnp.where(r == 4, mid.astype(F32),
                                        jnp.where(r == 5, lo.astype(F32), 0.0))))
    qxt_ref[HEAD_DIM:HEAD_DIM + AUG_ROWS, :] = aug.astype(BF16)

    def scores_into(s_ref, ki):
        s_ref[...] = _dot(kx_ref[ki], qxt_ref[...])

    def softmax_into(s_ref, p_ref, m, l, mask=None):
        s = s_ref[...]
        if mask is not None:
            s = jnp.where(mask, s, NEG_INF)
        m_new = jnp.maximum(m, jnp.max(s, axis=0, keepdims=True))
        alpha = jnp.exp2(m - m_new)
        p = jnp.exp2(s - m_new)
        p_ref[...] = p.astype(BF16)
        return m_new, alpha * l + jnp.sum(p, axis=0, keepdims=True), alpha

    def accumulate(p_ref, alpha, ki):
        acc_ref[...] = alpha * acc_ref[...] + _dot(vt_ref[ki], p_ref[...])

    scores_into(s_a, 0)
    p_b[...] = jnp.zeros(p_b.shape, BF16)
    acc_ref[...] = jnp.zeros(acc_ref.shape, F32)

    def pair(jj, carry):
        m, l, alpha_b = carry
        k0 = 2 * jj
        scores_into(s_b, k0 + 1)
        m, l, alpha_a = softmax_into(s_a, p_a, m, l)
        accumulate(p_b, alpha_b, jnp.maximum(k0 - 1, 0))
        scores_into(s_a, k0 + 2)
        m, l, alpha_b = softmax_into(s_b, p_b, m, l)
        accumulate(p_a, alpha_a, k0)
        return m, l, alpha_b

    init = (jnp.full((1, tile), NEG_INF, F32), jnp.zeros((1, tile), F32), jnp.ones((1, tile), F32))
    m, l, alpha_b = lax.fori_loop(0, qi, pair, init)

    k0 = 2 * qi
    key = lax.broadcasted_iota(jnp.int32, (kt, tile), 0)
    query = lax.broadcasted_iota(jnp.int32, (kt, tile), 1)
    scores_into(s_b, k0 + 1)
    m, l, alpha_a = softmax_into(s_a, p_a, m, l, mask=key <= query)
    accumulate(p_b, alpha_b, jnp.maximum(k0 - 1, 0))
    m, l, alpha_b = softmax_into(s_b, p_b, m, l, mask=key + kt <= query)
    accumulate(p_a, alpha_a, k0)
    accumulate(p_b, alpha_b, k0 + 1)
    o_ref[...] = (acc_ref[...] / l).T.astype(BF16)


def _attention(qkv, augk, c, batch, seq):
    t = qkv.shape[0]
    tile = ATTN_TILE
    kt = tile // 2
    nq = seq // tile
    c_rows = jnp.transpose(c.reshape(batch, seq, N_HEADS), (0, 2, 1)).reshape(
        batch * N_HEADS, nq, 1, tile)
    return pl.pallas_call(
        functools.partial(_attention_kernel, tile=tile),
        grid=(batch, N_HEADS, nq),
        in_specs=[
            pl.BlockSpec((tile, HEAD_DIM), lambda b, h, i: (b * nq + i, h)),
            pl.BlockSpec((seq, HEAD_DIM), lambda b, h, i: (b, N_HEADS + h)),
            pl.BlockSpec((seq, HEAD_DIM), lambda b, h, i: (b, 2 * N_HEADS + h)),
            pl.BlockSpec((seq, HEAD_DIM), lambda b, h, i: (b, h)),
            pl.BlockSpec((None, nq, 1, tile), lambda b, h, i: (b * N_HEADS + h, 0, 0, 0)),
        ],
        out_specs=pl.BlockSpec((tile, HEAD_DIM), lambda b, h, i: (b * nq + i, h)),
        out_shape=jax.ShapeDtypeStruct((t, N_HEADS * HEAD_DIM), BF16),
        scratch_shapes=[pltpu.VMEM((seq // kt, kt, 2 * HEAD_DIM), BF16),
                        pltpu.VMEM((seq // kt, HEAD_DIM, kt), BF16),
                        pltpu.VMEM((2 * HEAD_DIM, tile), BF16),
                        pltpu.VMEM((kt, tile), F32), pltpu.VMEM((kt, tile), F32),
                        pltpu.VMEM((kt, tile), BF16), pltpu.VMEM((kt, tile), BF16),
                        pltpu.VMEM((HEAD_DIM, tile), F32)],
        compiler_params=_params(("parallel", "parallel", "arbitrary"), 48),
        name="fox_attention",
    )(qkv, qkv, qkv, augk, c_rows)


def kernel(x, p, norm_mix_pre, w_in, forget_bias, conv_mix_w, w_branch_conv, w_branch_attn, w_out,
           norm_mix_post, norm_ffn_pre, w_up, ffn_conv_w, ffn_conv_b, w_down, norm_ffn_post,
           w_ple_proj, norm_ple_gate, w_ple_gate, norm_ple_post):
    batch, seq, d = x.shape
    t = batch * seq
    depth = w_in.shape[0]
    conv_width = conv_mix_w.shape[-1]
    attn_width = N_HEADS * HEAD_DIM
    qkv_start = 3 * conv_width
    forget_start = qkv_start + 3 * attn_width
    gate_start = forget_start + N_HEADS

    xs = x.reshape(t, d)
    for i in range(depth):
        w_in_i = w_in[i]
        w_conv = w_in_i[:, :qkv_start].astype(BF16)
        w_qkv = w_in_i[:, qkv_start:forget_start].astype(BF16)
        w_f = w_in_i[:, forget_start:gate_start].astype(BF16)
        w_gates = w_in_i[:, gate_start:].astype(BF16)

        h1 = _norm(xs, norm_mix_pre[i])
        a = _conv_branch(h1, w_conv, conv_mix_w[i], seq)
        qkv = _qkv(h1, w_qkv)
        gates = _sigmoid_proj(h1, w_gates)
        c, augk = _forget_cumsum(h1, w_f, forget_bias[i], seq)
        o = _attention(qkv, augk, c, batch, seq)
        merged = _merge(a, o, w_branch_conv[i].astype(BF16), w_branch_attn[i].astype(BF16), gates)
        y1 = _matmul(merged, w_out[i].astype(BF16), ROW_TILE_MM, 512, "out_proj")
        x1, h2 = _residual_norm(xs, y1, norm_mix_post[i], norm_ffn_pre[i])

        ffn = _ffn_up(h2, w_up[i].astype(BF16), ffn_conv_w[i], ffn_conv_b[i], seq)
        y2 = _matmul(ffn, w_down[i].astype(BF16), ROW_TILE_WIDE_K, 512, "ffn_down")
        x2, h3 = _residual_norm(x1, y2, norm_ffn_post[i], norm_ple_gate[i])

        ge = _ple(h3, p[i].reshape(t, -1), w_ple_gate[i].astype(BF16), w_ple_proj[i].astype(BF16))
        xs = _residual(x2, ge, norm_ple_post[i])
    return xs.reshape(batch, seq, d)
```

```python
import functools
import math

import jax
import jax.numpy as jnp
from jax import lax
from jax.experimental import pallas as pl
from jax.experimental.pallas import tpu as pltpu

BF16 = jnp.bfloat16
F32 = jnp.float32

LANES = 128
N_HEADS = 32
HEAD_DIM = 128
SHORT_K = 3
EPS = 1e-6
NEG_INF = -1e30
LOG2_E = math.log2(math.e)
MIB = 1024 * 1024

ROW_TILE_NORM = 256
ROW_TILE_MM = 1024
ROW_TILE_WIDE_K = 512
ATTN_TILE = 1024
CUMSUM_TILE = 512
CARRY_ROWS = 8


def _params(semantics, vmem_mib, flags=None):
    return pltpu.CompilerParams(dimension_semantics=semantics, vmem_limit_bytes=vmem_mib * MIB,
                                flags=flags)


def _dot(a, b):
    return jnp.dot(a, b, preferred_element_type=F32)


def _rms_scale(x, g):
    inv = lax.rsqrt(jnp.mean(x * x, axis=-1, keepdims=True) + EPS)
    return x * inv * g


def _norm_kernel(x_ref, g_ref, h_ref):
    h_ref[...] = _rms_scale(x_ref[...], g_ref[...]).astype(BF16)


def _residual_norm_kernel(x_ref, y_ref, g_post_ref, g_pre_ref, x_out_ref, h_ref):
    x_new = x_ref[...] + _rms_scale(y_ref[...], g_post_ref[...])
    x_out_ref[...] = x_new
    h_ref[...] = _rms_scale(x_new, g_pre_ref[...]).astype(BF16)


def _residual_kernel(x_ref, y_ref, g_post_ref, x_out_ref):
    x_out_ref[...] = x_ref[...] + _rms_scale(y_ref[...], g_post_ref[...])


def _row_spec(tr, d):
    return pl.BlockSpec((tr, d), lambda i: (i, 0))


def _gain_spec(d):
    return pl.BlockSpec((1, d), lambda i: (0, 0))


def _norm(x, g):
    t, d = x.shape
    tr = ROW_TILE_NORM
    return pl.pallas_call(
        _norm_kernel,
        grid=(t // tr,),
        in_specs=[_row_spec(tr, d), _gain_spec(d)],
        out_specs=_row_spec(tr, d),
        out_shape=jax.ShapeDtypeStruct((t, d), BF16),
        compiler_params=_params(("parallel",), 40),
        name="norm",
    )(x, g.reshape(1, d))


def _residual_norm(x, y, g_post, g_pre):
    t, d = x.shape
    tr = ROW_TILE_NORM
    return pl.pallas_call(
        _residual_norm_kernel,
        grid=(t // tr,),
        in_specs=[_row_spec(tr, d), _row_spec(tr, d), _gain_spec(d), _gain_spec(d)],
        out_specs=[_row_spec(tr, d), _row_spec(tr, d)],
        out_shape=[jax.ShapeDtypeStruct((t, d), F32), jax.ShapeDtypeStruct((t, d), BF16)],
        compiler_params=_params(("parallel",), 40),
        name="residual_norm",
    )(x, y, g_post.reshape(1, d), g_pre.reshape(1, d))


def _residual(x, y, g_post):
    t, d = x.shape
    tr = ROW_TILE_NORM
    return pl.pallas_call(
        _residual_kernel,
        grid=(t // tr,),
        in_specs=[_row_spec(tr, d), _row_spec(tr, d), _gain_spec(d)],
        out_specs=_row_spec(tr, d),
        out_shape=jax.ShapeDtypeStruct((t, d), F32),
        compiler_params=_params(("parallel",), 40),
        name="residual",
    )(x, y, g_post.reshape(1, d))


def _causal_conv3(u, carry_ref, w_ref):
    tm = u.shape[0]
    rows = lax.broadcasted_iota(jnp.int32, u.shape, 0)
    prev1 = carry_ref[CARRY_ROWS - 1:CARRY_ROWS, :]
    prev2 = carry_ref[CARRY_ROWS - 2:CARRY_ROWS - 1, :]
    back1 = jnp.where(rows == 0, prev1, pltpu.roll(u, 1, 0))
    back2 = jnp.where(rows == 0, prev2, jnp.where(rows == 1, prev1, pltpu.roll(u, 2, 0)))
    carry_ref[...] = u[tm - CARRY_ROWS:, :]
    return w_ref[0:1, :] * back2 + w_ref[1:2, :] * back1 + w_ref[2:3, :] * u


def _reset_carry_at_sequence_start(carry_refs, tiles_per_seq):
    @pl.when(pl.program_id(1) % tiles_per_seq == 0)
    def _():
        for ref in carry_refs:
            ref[...] = jnp.zeros_like(ref)


def _cast_weights_at_sweep_start(pairs):
    @pl.when(pl.program_id(1) == 0)
    def _():
        for w_ref, w_bf16_ref in pairs:
            w_bf16_ref[...] = w_ref[...].astype(BF16)


def _conv_branch_kernel(h_ref, wb_ref, wc_ref, wv_ref, cw_ref, o_ref, carry_ref,
                        wb_bf, wc_bf, wv_bf, *, tiles_per_seq):
    _cast_weights_at_sweep_start([(wb_ref, wb_bf), (wc_ref, wc_bf), (wv_ref, wv_bf)])
    _reset_carry_at_sequence_start([carry_ref], tiles_per_seq)
    h = h_ref[...]
    u = _dot(h, wc_bf[...]) * _dot(h, wv_bf[...])
    y = _causal_conv3(u, carry_ref, cw_ref)
    o_ref[...] = (_dot(h, wb_bf[...]) * y).astype(BF16)


def _conv_branch(h, w_in, conv_w, seq):
    t, d = h.shape
    width = conv_w.shape[1]
    tm, tn = ROW_TILE_WIDE_K, 256
    nb = width // tn
    return pl.pallas_call(
        functools.partial(_conv_branch_kernel, tiles_per_seq=seq // tm),
        grid=(nb, t // tm),
        in_specs=[
            pl.BlockSpec((tm, d), lambda j, i: (i, 0)),
            pl.BlockSpec((d, tn), lambda j, i: (0, j)),
            pl.BlockSpec((d, tn), lambda j, i: (0, j + nb)),
            pl.BlockSpec((d, tn), lambda j, i: (0, j + 2 * nb)),
            pl.BlockSpec((SHORT_K, tn), lambda j, i: (0, j)),
        ],
        out_specs=pl.BlockSpec((tm, tn), lambda j, i: (i, j)),
        out_shape=jax.ShapeDtypeStruct((t, width), BF16),
        scratch_shapes=[pltpu.VMEM((CARRY_ROWS, tn), F32)] + [pltpu.VMEM((d, tn), BF16)] * 3,
        compiler_params=_params(("parallel", "arbitrary"), 48),
        name="conv_branch",
    )(h, w_in, w_in, w_in, conv_w)


def _qkv_kernel(h_ref, w_ref, o_ref, w_bf, *, q_tiles, scale):
    _cast_weights_at_sweep_start([(w_ref, w_bf)])
    acc = _dot(h_ref[...], w_bf[...])
    factor = jnp.where(pl.program_id(0) < q_tiles, scale, 1.0).astype(F32)
    o_ref[...] = (acc * factor).astype(BF16)


def _qkv(h, w_in, col_start, n):
    t, d = h.shape
    tm, tn = ROW_TILE_MM, 512
    first = col_start // tn
    kern = functools.partial(_qkv_kernel, q_tiles=(n // 3) // tn, scale=LOG2_E / math.sqrt(HEAD_DIM))
    return pl.pallas_call(
        kern,
        grid=(n // tn, t // tm),
        in_specs=[pl.BlockSpec((tm, d), lambda j, i: (i, 0)),
                  pl.BlockSpec((d, tn), lambda j, i: (0, first + j))],
        out_specs=pl.BlockSpec((tm, tn), lambda j, i: (i, j)),
        out_shape=jax.ShapeDtypeStruct((t, n), BF16),
        scratch_shapes=[pltpu.VMEM((d, tn), BF16)],
        compiler_params=_params(("parallel", "arbitrary"), 48),
        name="qkv",
    )(h, w_in)


def _sigmoid_proj_kernel(h_ref, w_ref, o_ref):
    o_ref[...] = jax.nn.sigmoid(_dot(h_ref[...], w_ref[...]))


def _sigmoid_proj(h, w):
    t, d = h.shape
    n = w.shape[1]
    tm, tn = ROW_TILE_MM, 512
    return pl.pallas_call(
        _sigmoid_proj_kernel,
        grid=(n // tn, t // tm),
        in_specs=[pl.BlockSpec((tm, d), lambda j, i: (i, 0)),
                  pl.BlockSpec((d, tn), lambda j, i: (0, j))],
        out_specs=pl.BlockSpec((tm, tn), lambda j, i: (i, j)),
        out_shape=jax.ShapeDtypeStruct((t, n), F32),
        compiler_params=_params(("parallel", "arbitrary"), 48),
        name="gates",
    )(h, w)


def _matmul_kernel(a_ref, w_ref, o_ref):
    o_ref[...] = _dot(a_ref[...], w_ref[...])


def _matmul_f32_weights_kernel(a_ref, w_ref, o_ref, w_bf):
    _cast_weights_at_sweep_start([(w_ref, w_bf)])
    o_ref[...] = _dot(a_ref[...], w_bf[...])


def _matmul(a, w, tm, tn, name):
    t, k = a.shape
    n = w.shape[1]
    cast_in_kernel = w.dtype == F32
    return pl.pallas_call(
        _matmul_f32_weights_kernel if cast_in_kernel else _matmul_kernel,
        grid=(n // tn, t // tm),
        in_specs=[pl.BlockSpec((tm, k), lambda j, i: (i, 0)),
                  pl.BlockSpec((k, tn), lambda j, i: (0, j))],
        out_specs=pl.BlockSpec((tm, tn), lambda j, i: (i, j)),
        out_shape=jax.ShapeDtypeStruct((t, n), F32),
        scratch_shapes=[pltpu.VMEM((k, tn), BF16)] if cast_in_kernel else [],
        compiler_params=_params(("parallel", "arbitrary"), 52),
        name=name,
    )(a, w)


def _merge_kernel(a_ref, o_ref, wa_ref, wb_ref, ga_ref, gb_ref, out_ref):
    y_a = _dot(a_ref[...], wa_ref[...])
    y_b = _dot(o_ref[...], wb_ref[...])
    out_ref[...] = (ga_ref[...] * y_a + gb_ref[...] * y_b).astype(BF16)


def _merge(a, o, w_a, w_b, gates):
    t, d = a.shape
    n = w_a.shape[1]
    tm, tn = ROW_TILE_WIDE_K, 512
    nb = n // tn
    return pl.pallas_call(
        _merge_kernel,
        grid=(nb, t // tm),
        in_specs=[
            pl.BlockSpec((tm, d), lambda j, i: (i, 0)),
            pl.BlockSpec((tm, d), lambda j, i: (i, 0)),
            pl.BlockSpec((d, tn), lambda j, i: (0, j)),
            pl.BlockSpec((d, tn), lambda j, i: (0, j)),
            pl.BlockSpec((tm, tn), lambda j, i: (i, j)),
            pl.BlockSpec((tm, tn), lambda j, i: (i, j + nb)),
        ],
        out_specs=pl.BlockSpec((tm, tn), lambda j, i: (i, j)),
        out_shape=jax.ShapeDtypeStruct((t, n), BF16),
        compiler_params=_params(("parallel", "arbitrary"), 48),
        name="merge",
    )(a, o, w_a, w_b, gates, gates)


def _ffn_up_kernel(h_ref, wg_ref, wv_ref, cwg_ref, cwv_ref, bg_ref, bv_ref, o_ref,
                   carry_g_ref, carry_v_ref, wg_bf, wv_bf, *, tiles_per_seq):
    _cast_weights_at_sweep_start([(wg_ref, wg_bf), (wv_ref, wv_bf)])
    _reset_carry_at_sequence_start([carry_g_ref, carry_v_ref], tiles_per_seq)
    h = h_ref[...]
    u_gate = _causal_conv3(_dot(h, wg_bf[...]), carry_g_ref, cwg_ref) + bg_ref[...]
    u_val = _causal_conv3(_dot(h, wv_bf[...]), carry_v_ref, cwv_ref) + bv_ref[...]
    o_ref[...] = (jax.nn.gelu(u_gate, approximate=True) * u_val).astype(BF16)


def _ffn_up(h, w_up, conv_w, conv_b, seq):
    t, d = h.shape
    d_ff = w_up.shape[1] // 2
    tm, tn = ROW_TILE_MM, 256
    nb = d_ff // tn
    conv_b = conv_b.reshape(1, 2 * d_ff)
    return pl.pallas_call(
        functools.partial(_ffn_up_kernel, tiles_per_seq=seq // tm),
        grid=(nb, t // tm),
        in_specs=[
            pl.BlockSpec((tm, d), lambda j, i: (i, 0)),
            pl.BlockSpec((d, tn), lambda j, i: (0, j)),
            pl.BlockSpec((d, tn), lambda j, i: (0, j + nb)),
            pl.BlockSpec((SHORT_K, tn), lambda j, i: (0, j)),
            pl.BlockSpec((SHORT_K, tn), lambda j, i: (0, j + nb)),
            pl.BlockSpec((1, tn), lambda j, i: (0, j)),
            pl.BlockSpec((1, tn), lambda j, i: (0, j + nb)),
        ],
        out_specs=pl.BlockSpec((tm, tn), lambda j, i: (i, j)),
        out_shape=jax.ShapeDtypeStruct((t, d_ff), BF16),
        scratch_shapes=[pltpu.VMEM((CARRY_ROWS, tn), F32)] * 2 + [pltpu.VMEM((d, tn), BF16)] * 2,
        compiler_params=_params(("parallel", "arbitrary"), 48),
        name="ffn_up",
    )(h, w_up, w_up, conv_w, conv_w, conv_b, conv_b)


def _ple_kernel(h_ref, p_ref, wg_ref, wp_ref, o_ref, wg_bf, wp_bf):
    _cast_weights_at_sweep_start([(wg_ref, wg_bf), (wp_ref, wp_bf)])
    gate = jax.nn.sigmoid(_dot(h_ref[...], wg_bf[...]))
    o_ref[...] = gate * _dot(p_ref[...].astype(BF16), wp_bf[...])


def _ple(h, p, w_gate, w_proj):
    t, d = h.shape
    n = w_gate.shape[1]
    ple = p.shape[1]
    tm, tn = ROW_TILE_MM, 512
    return pl.pallas_call(
        _ple_kernel,
        grid=(n // tn, t // tm),
        in_specs=[
            pl.BlockSpec((tm, d), lambda j, i: (i, 0)),
            pl.BlockSpec((tm, ple), lambda j, i: (i, 0)),
            pl.BlockSpec((d, tn), lambda j, i: (0, j)),
            pl.BlockSpec((ple, tn), lambda j, i: (0, j)),
        ],
        out_specs=pl.BlockSpec((tm, tn), lambda j, i: (i, j)),
        out_shape=jax.ShapeDtypeStruct((t, n), F32),
        scratch_shapes=[pltpu.VMEM((d, tn), BF16), pltpu.VMEM((ple, tn), BF16)],
        compiler_params=_params(("parallel", "arbitrary"), 48),
        name="ple",
    )(h, p, w_gate, w_proj)


def _split_bf16x3(x):
    hi = x.astype(BF16)
    rest = x - hi.astype(F32)
    mid = rest.astype(BF16)
    lo = (rest - mid.astype(F32)).astype(BF16)
    return hi, mid, lo


def _forget_cumsum_kernel(h_ref, wf_ref, b_ref, spread_ref, ones_ref, c_ref, augk_ref, carry_ref,
                          wf_bf, *, tiles_per_seq):
    @pl.when(pl.program_id(0) == 0)
    def _():
        wf_bf[...] = wf_ref[...].astype(BF16)

    @pl.when(pl.program_id(0) % tiles_per_seq == 0)
    def _():
        carry_ref[...] = jnp.zeros_like(carry_ref)

    ts = h_ref.shape[0]
    nh = b_ref.shape[1]
    log_f = jax.nn.log_sigmoid(_dot(h_ref[...], wf_bf[...])[:, 0:nh] + b_ref[...])
    row = lax.broadcasted_iota(jnp.int32, (ts, ts), 0)
    col = lax.broadcasted_iota(jnp.int32, (ts, ts), 1)
    lower = (col <= row).astype(F32)
    csum = jnp.dot(lower, log_f, preferred_element_type=F32,
                   precision=lax.Precision.HIGHEST) + carry_ref[...]
    c_ref[...] = csum
    carry_ref[...] = c_ref[ts - 1:ts, :]

    hi, mid, lo = _split_bf16x3(csum * LOG2_E)
    aug = (_dot(hi, spread_ref[0]) + _dot(mid, spread_ref[1]) + _dot(lo, spread_ref[2])
           + ones_ref[...])
    augk_ref[...] = aug.astype(BF16)


def _forget_cumsum(h, w_in, col_start, bias, seq):
    t, d = h.shape
    nh = bias.shape[0]
    ts = CUMSUM_TILE
    assert col_start % LANES == 0 and nh <= LANES
    width = nh * HEAD_DIM
    lane = jnp.arange(width)[None, :]
    head = jnp.arange(nh)[:, None]
    spread = jnp.stack([jnp.where(lane == head * HEAD_DIM + j, -1.0, 0.0) for j in range(3)])
    ones = jnp.where((lane % HEAD_DIM >= 3) & (lane % HEAD_DIM < 6), 1.0, 0.0).astype(F32)
    return pl.pallas_call(
        functools.partial(_forget_cumsum_kernel, tiles_per_seq=seq // ts),
        grid=(t // ts,),
        in_specs=[pl.BlockSpec((ts, d), lambda i: (i, 0)),
                  pl.BlockSpec((d, LANES), lambda i: (0, col_start // LANES)),
                  pl.BlockSpec((1, nh), lambda i: (0, 0)),
                  pl.BlockSpec((3, nh, width), lambda i: (0, 0, 0)),
                  pl.BlockSpec((1, width), lambda i: (0, 0))],
        out_specs=[pl.BlockSpec((ts, nh), lambda i: (i, 0)),
                   pl.BlockSpec((ts, width), lambda i: (i, 0))],
        out_shape=[jax.ShapeDtypeStruct((t, nh), F32), jax.ShapeDtypeStruct((t, width), BF16)],
        scratch_shapes=[pltpu.VMEM((1, nh), F32), pltpu.VMEM((d, LANES), BF16)],
        compiler_params=_params(("arbitrary",), 48),
        name="forget_cumsum",
    )(h, w_in, bias.reshape(1, nh), spread.astype(BF16), ones)


AUG_ROWS = 16
SUM_ROWS = 16
HEADS_PER_STEP = 2


def _attention_kernel(q_ref, k_ref, v_ref, augk_ref, c_ref, o_ref,
                      kx_ref, vt_ref, qxt_ref, *buffers, tile):
    n = HEADS_PER_STEP
    s_ref = [buffers[2 * g:2 * g + 2] for g in range(n)]
    p_ref = [buffers[2 * n + 2 * g:2 * n + 2 * g + 2] for g in range(n)]
    acc_ref = buffers[4 * n:]
    kt = tile // 2
    qi = pl.program_id(2)
    n_key_tiles = kx_ref.shape[1]
    heads = range(HEADS_PER_STEP)

    def lanes(g):
        return slice(g * HEAD_DIM, (g + 1) * HEAD_DIM)

    @pl.when(qi == 0)
    def _():
        def stage(j, _):
            rows = pl.ds(pl.multiple_of(j * kt, kt), kt)
            for g in heads:
                kx_ref[g, j, :, 0:HEAD_DIM] = k_ref[rows, lanes(g)]
                kx_ref[g, j, :, HEAD_DIM:] = augk_ref[rows, lanes(g)]
                vt_ref[g, j, 0:HEAD_DIM, :] = v_ref[rows, lanes(g)].astype(F32).T.astype(BF16)
                vt_ref[g, j, HEAD_DIM:, :] = jnp.ones((SUM_ROWS, kt), BF16)
            return 0
        lax.fori_loop(0, n_key_tiles, stage, 0)
        for g in heads:
            qxt_ref[g, HEAD_DIM + AUG_ROWS:, :] = jnp.zeros((HEAD_DIM - AUG_ROWS, tile), BF16)

    r = lax.broadcasted_iota(jnp.int32, (AUG_ROWS, tile), 0)
    for g in heads:
        qxt_ref[g, 0:HEAD_DIM, :] = q_ref[:, lanes(g)].astype(F32).T.astype(BF16)
        hi, mid, lo = _split_bf16x3(c_ref[g, qi] * LOG2_E)
        aug = jnp.where(r < 3, 1.0,
                        jnp.where(r == 3, hi.astype(F32),
                                  jnp.where(r == 4, mid.astype(F32),
                                            jnp.where(r == 5, lo.astype(F32), 0.0))))
        qxt_ref[g, HEAD_DIM:HEAD_DIM + AUG_ROWS, :] = aug.astype(BF16)

    def scores_into(g, slot, ki, mask=None):
        s = _dot(kx_ref[g, ki], qxt_ref[g])
        if mask is not None:
            s = jnp.where(mask, s, NEG_INF)
        s_ref[g][slot][...] = s
        return jnp.max(s, axis=0, keepdims=True)

    def softmax_into(g, slot, m, tile_max):
        m_new = jnp.maximum(m, tile_max)
        p_ref[g][slot][...] = jnp.exp2(s_ref[g][slot][...] - m_new).astype(BF16)
        return m_new, jnp.exp2(m - m_new)

    def accumulate(g, slot, alpha, ki):
        acc_ref[g][...] = alpha * acc_ref[g][...] + _dot(vt_ref[g, ki], p_ref[g][slot][...])

    for g in heads:
        p_ref[g][1][...] = jnp.zeros((kt, tile), BF16)
        acc_ref[g][...] = jnp.zeros(acc_ref[g].shape, F32)
    max_0 = tuple(scores_into(g, 0, 0) for g in heads)

    def pair(jj, carry):
        m, alpha_1, max_0 = carry
        k0 = 2 * jj
        max_1 = [scores_into(g, 1, k0 + 1) for g in heads]
        m, alpha_0 = zip(*[softmax_into(g, 0, m[g], max_0[g]) for g in heads])
        for g in heads:
            accumulate(g, 1, alpha_1[g], jnp.maximum(k0 - 1, 0))
        max_0 = tuple(scores_into(g, 0, k0 + 2) for g in heads)
        m, alpha_1 = zip(*[softmax_into(g, 1, m[g], max_1[g]) for g in heads])
        for g in heads:
            accumulate(g, 0, alpha_0[g], k0)
        return m, alpha_1, max_0

    init = (tuple(jnp.full((1, tile), NEG_INF, F32) for _ in heads),
            tuple(jnp.ones((1, tile), F32) for _ in heads), max_0)
    m, alpha_1, _ = lax.fori_loop(0, qi, pair, init)

    k0 = 2 * qi
    key = lax.broadcasted_iota(jnp.int32, (kt, tile), 0)
    query = lax.broadcasted_iota(jnp.int32, (kt, tile), 1)
    max_1 = [scores_into(g, 1, k0 + 1, mask=key + kt <= query) for g in heads]
    for g in heads:
        s_ref[g][0][...] = jnp.where(key <= query, s_ref[g][0][...], NEG_INF)
    max_0 = [jnp.max(s_ref[g][0][...], axis=0, keepdims=True) for g in heads]
    m, alpha_0 = zip(*[softmax_into(g, 0, m[g], max_0[g]) for g in heads])
    for g in heads:
        accumulate(g, 1, alpha_1[g], jnp.maximum(k0 - 1, 0))
    m, alpha_1 = zip(*[softmax_into(g, 1, m[g], max_1[g]) for g in heads])
    for g in heads:
        accumulate(g, 0, alpha_0[g], k0)
    for g in heads:
        accumulate(g, 1, alpha_1[g], k0 + 1)
    for g in heads:
        o_t = acc_ref[g][0:HEAD_DIM, :] / acc_ref[g][HEAD_DIM:HEAD_DIM + 1, :]
        o_ref[:, lanes(g)] = o_t.T.astype(BF16)


def _attention(qkv, augk, c, batch, seq):
    t = qkv.shape[0]
    tile = ATTN_TILE
    kt = tile // 2
    nq = seq // tile
    g = HEADS_PER_STEP
    width = g * HEAD_DIM
    groups = N_HEADS // g
    c_rows = jnp.transpose(c.reshape(batch, seq, N_HEADS), (0, 2, 1)).reshape(
        batch * N_HEADS, nq, 1, tile)
    return pl.pallas_call(
        functools.partial(_attention_kernel, tile=tile),
        grid=(batch, groups, nq),
        in_specs=[
            pl.BlockSpec((tile, width), lambda b, h, i: (b * nq + i, h)),
            pl.BlockSpec((seq, width), lambda b, h, i: (b, groups + h), pipeline_mode=pl.Buffered(1)),
            pl.BlockSpec((seq, width), lambda b, h, i: (b, 2 * groups + h), pipeline_mode=pl.Buffered(1)),
            pl.BlockSpec((seq, width), lambda b, h, i: (b, h), pipeline_mode=pl.Buffered(1)),
            pl.BlockSpec((g, nq, 1, tile), lambda b, h, i: (b * groups + h, 0, 0, 0)),
        ],
        out_specs=pl.BlockSpec((tile, width), lambda b, h, i: (b * nq + i, h)),
        out_shape=jax.ShapeDtypeStruct((t, N_HEADS * HEAD_DIM), BF16),
        scratch_shapes=[pltpu.VMEM((g, seq // kt, kt, 2 * HEAD_DIM), BF16),
                        pltpu.VMEM((g, seq // kt, HEAD_DIM + SUM_ROWS, kt), BF16),
                        pltpu.VMEM((g, 2 * HEAD_DIM, tile), BF16),
                        *[pltpu.VMEM((kt, tile), F32)] * (2 * g),
                        *[pltpu.VMEM((kt, tile), BF16)] * (2 * g),
                        *[pltpu.VMEM((HEAD_DIM + SUM_ROWS, tile), F32)] * g],
        compiler_params=_params(("parallel", "parallel", "arbitrary"), 56),
        name="fox_attention",
    )(qkv, qkv, qkv, augk, c_rows)


def kernel(x, p, norm_mix_pre, w_in, forget_bias, conv_mix_w, w_branch_conv, w_branch_attn, w_out,
           norm_mix_post, norm_ffn_pre, w_up, ffn_conv_w, ffn_conv_b, w_down, norm_ffn_post,
           w_ple_proj, norm_ple_gate, w_ple_gate, norm_ple_post):
    batch, seq, d = x.shape
    t = batch * seq
    depth = w_in.shape[0]
    conv_width = conv_mix_w.shape[-1]
    attn_width = N_HEADS * HEAD_DIM
    qkv_start = 3 * conv_width
    forget_start = qkv_start + 3 * attn_width
    gate_start = forget_start + N_HEADS

    xs = x.reshape(t, d)
    for i in range(depth):
        w_in_i = w_in[i]
        w_gates = w_in_i[:, gate_start:].astype(BF16)

        h1 = _norm(xs, norm_mix_pre[i])
        a = _conv_branch(h1, w_in_i, conv_mix_w[i], seq)
        qkv = _qkv(h1, w_in_i, qkv_start, 3 * attn_width)
        gates = _sigmoid_proj(h1, w_gates)
        c, augk = _forget_cumsum(h1, w_in_i, forget_start, forget_bias[i], seq)
        o = _attention(qkv, augk, c, batch, seq)
        merged = _merge(a, o, w_branch_conv[i].astype(BF16), w_branch_attn[i].astype(BF16), gates)
        y1 = _matmul(merged, w_out[i], ROW_TILE_MM, 512, "out_proj")
        x1, h2 = _residual_norm(xs, y1, norm_mix_post[i], norm_ffn_pre[i])

        ffn = _ffn_up(h2, w_up[i], ffn_conv_w[i], ffn_conv_b[i], seq)
        y2 = _matmul(ffn, w_down[i].astype(BF16), ROW_TILE_WIDE_K, 512, "ffn_down")
        x2, h3 = _residual_norm(x1, y2, norm_ffn_post[i], norm_ple_gate[i])

        ge = _ple(h3, p[i].reshape(t, -1), w_ple_gate[i], w_ple_proj[i])
        xs = _residual(x2, ge, norm_ple_post[i])
    return xs.reshape(batch, seq, d)
```

```python
import functools
import math

import jax
import jax.numpy as jnp
from jax import lax
from jax.experimental import pallas as pl
from jax.experimental.pallas import tpu as pltpu

BF16 = jnp.bfloat16
F32 = jnp.float32

LANES = 128
N_HEADS = 32
HEAD_DIM = 128
SHORT_K = 3
EPS = 1e-6
NEG_INF = -1e30
LOG2_E = math.log2(math.e)
MIB = 1024 * 1024

ROW_TILE_NORM = 256
ROW_TILE_MM = 1024
ROW_TILE_WIDE_K = 512
ATTN_TILE = 1024
CUMSUM_TILE = 512
CARRY_ROWS = 8


def _params(semantics, vmem_mib, flags=None):
    return pltpu.CompilerParams(dimension_semantics=semantics, vmem_limit_bytes=vmem_mib * MIB,
                                flags=flags)


def _dot(a, b):
    return jnp.dot(a, b, preferred_element_type=F32)


def _rms_scale(x, g):
    inv = lax.rsqrt(jnp.mean(x * x, axis=-1, keepdims=True) + EPS)
    return x * inv * g


def _norm_kernel(x_ref, g_ref, h_ref):
    h_ref[...] = _rms_scale(x_ref[...], g_ref[...]).astype(BF16)


def _residual_norm_kernel(x_ref, y_ref, g_post_ref, g_pre_ref, x_out_ref, h_ref):
    x_new = x_ref[...] + _rms_scale(y_ref[...].astype(F32), g_post_ref[...])
    x_out_ref[...] = x_new
    h_ref[...] = _rms_scale(x_new, g_pre_ref[...]).astype(BF16)


def _residual_kernel(x_ref, y_ref, g_post_ref, x_out_ref):
    x_out_ref[...] = x_ref[...] + _rms_scale(y_ref[...].astype(F32), g_post_ref[...])


def _row_spec(tr, d):
    return pl.BlockSpec((tr, d), lambda i: (i, 0))


def _gain_spec(d):
    return pl.BlockSpec((1, d), lambda i: (0, 0))


def _norm(x, g):
    t, d = x.shape
    tr = ROW_TILE_NORM
    return pl.pallas_call(
        _norm_kernel,
        grid=(t // tr,),
        in_specs=[_row_spec(tr, d), _gain_spec(d)],
        out_specs=_row_spec(tr, d),
        out_shape=jax.ShapeDtypeStruct((t, d), BF16),
        compiler_params=_params(("parallel",), 40),
        name="norm",
    )(x, g.reshape(1, d))


def _residual_norm(x, y, g_post, g_pre):
    t, d = x.shape
    tr = ROW_TILE_NORM
    return pl.pallas_call(
        _residual_norm_kernel,
        grid=(t // tr,),
        in_specs=[_row_spec(tr, d), _row_spec(tr, d), _gain_spec(d), _gain_spec(d)],
        out_specs=[_row_spec(tr, d), _row_spec(tr, d)],
        out_shape=[jax.ShapeDtypeStruct((t, d), F32), jax.ShapeDtypeStruct((t, d), BF16)],
        compiler_params=_params(("parallel",), 40),
        name="residual_norm",
    )(x, y, g_post.reshape(1, d), g_pre.reshape(1, d))


def _residual(x, y, g_post):
    t, d = x.shape
    tr = ROW_TILE_NORM
    return pl.pallas_call(
        _residual_kernel,
        grid=(t // tr,),
        in_specs=[_row_spec(tr, d), _row_spec(tr, d), _gain_spec(d)],
        out_specs=_row_spec(tr, d),
        out_shape=jax.ShapeDtypeStruct((t, d), F32),
        compiler_params=_params(("parallel",), 40),
        name="residual",
    )(x, y, g_post.reshape(1, d))


def _causal_conv3(u, carry_ref, w_ref):
    tm = u.shape[0]
    rows = lax.broadcasted_iota(jnp.int32, u.shape, 0)
    prev1 = carry_ref[CARRY_ROWS - 1:CARRY_ROWS, :]
    prev2 = carry_ref[CARRY_ROWS - 2:CARRY_ROWS - 1, :]
    back1 = jnp.where(rows == 0, prev1, pltpu.roll(u, 1, 0))
    back2 = jnp.where(rows == 0, prev2, jnp.where(rows == 1, prev1, pltpu.roll(u, 2, 0)))
    carry_ref[...] = u[tm - CARRY_ROWS:, :]
    return w_ref[0:1, :] * back2 + w_ref[1:2, :] * back1 + w_ref[2:3, :] * u


def _reset_carry_at_sequence_start(carry_refs, tiles_per_seq):
    @pl.when(pl.program_id(1) % tiles_per_seq == 0)
    def _():
        for ref in carry_refs:
            ref[...] = jnp.zeros_like(ref)


def _cast_weights_at_sweep_start(pairs):
    @pl.when(pl.program_id(1) == 0)
    def _():
        for w_ref, w_bf16_ref in pairs:
            w_bf16_ref[...] = w_ref[...].astype(BF16)


def _conv_branch_kernel(h_ref, wb_ref, wc_ref, wv_ref, cw_ref, o_ref, carry_ref,
                        wb_bf, wc_bf, wv_bf, *, tiles_per_seq):
    _cast_weights_at_sweep_start([(wb_ref, wb_bf), (wc_ref, wc_bf), (wv_ref, wv_bf)])
    _reset_carry_at_sequence_start([carry_ref], tiles_per_seq)
    h = h_ref[...]
    u = _dot(h, wc_bf[...]) * _dot(h, wv_bf[...])
    y = _causal_conv3(u, carry_ref, cw_ref)
    o_ref[...] = (_dot(h, wb_bf[...]) * y).astype(BF16)


def _conv_branch(h, w_in, conv_w, seq):
    t, d = h.shape
    width = conv_w.shape[1]
    tm, tn = ROW_TILE_WIDE_K, 256
    nb = width // tn
    return pl.pallas_call(
        functools.partial(_conv_branch_kernel, tiles_per_seq=seq // tm),
        grid=(nb, t // tm),
        in_specs=[
            pl.BlockSpec((tm, d), lambda j, i: (i, 0)),
            pl.BlockSpec((d, tn), lambda j, i: (0, j)),
            pl.BlockSpec((d, tn), lambda j, i: (0, j + nb)),
            pl.BlockSpec((d, tn), lambda j, i: (0, j + 2 * nb)),
            pl.BlockSpec((SHORT_K, tn), lambda j, i: (0, j)),
        ],
        out_specs=pl.BlockSpec((tm, tn), lambda j, i: (i, j)),
        out_shape=jax.ShapeDtypeStruct((t, width), BF16),
        scratch_shapes=[pltpu.VMEM((CARRY_ROWS, tn), F32)] + [pltpu.VMEM((d, tn), BF16)] * 3,
        compiler_params=_params(("parallel", "arbitrary"), 48),
        name="conv_branch",
    )(h, w_in, w_in, w_in, conv_w)


def _qkv_kernel(h_ref, w_ref, o_ref, w_bf, *, q_tiles, scale):
    _cast_weights_at_sweep_start([(w_ref, w_bf)])
    acc = _dot(h_ref[...], w_bf[...])
    factor = jnp.where(pl.program_id(0) < q_tiles, scale, 1.0).astype(F32)
    o_ref[...] = (acc * factor).astype(BF16)


def _qkv(h, w_in, col_start, n):
    t, d = h.shape
    tm, tn = ROW_TILE_MM, 512
    first = col_start // tn
    kern = functools.partial(_qkv_kernel, q_tiles=(n // 3) // tn, scale=LOG2_E / math.sqrt(HEAD_DIM))
    return pl.pallas_call(
        kern,
        grid=(n // tn, t // tm),
        in_specs=[pl.BlockSpec((tm, d), lambda j, i: (i, 0)),
                  pl.BlockSpec((d, tn), lambda j, i: (0, first + j))],
        out_specs=pl.BlockSpec((tm, tn), lambda j, i: (i, j)),
        out_shape=jax.ShapeDtypeStruct((t, n), BF16),
        scratch_shapes=[pltpu.VMEM((d, tn), BF16)],
        compiler_params=_params(("parallel", "arbitrary"), 48),
        name="qkv",
    )(h, w_in)


GATE_CAST_ROWS = 512


def _gates_kernel(h_ref, w_ref, w_next_ref, o_ref, w_bf, *, shift):
    @pl.when(pl.program_id(1) == 0)
    def _():
        d, tn = w_bf.shape
        lane = lax.broadcasted_iota(jnp.int32, (GATE_CAST_ROWS, LANES), 1)
        for r in range(0, d, GATE_CAST_ROWS):
            rows = slice(r, r + GATE_CAST_ROWS)
            main = pltpu.roll(w_ref[rows, :], tn - shift, 1)
            tail = pltpu.roll(w_next_ref[rows, :], LANES - shift, 1)
            w_bf[rows, 0:tn - LANES] = main[:, 0:tn - LANES].astype(BF16)
            w_bf[rows, tn - LANES:] = jnp.where(lane >= LANES - shift, tail,
                                                main[:, tn - LANES:]).astype(BF16)

    o_ref[...] = jax.nn.sigmoid(_dot(h_ref[...], w_bf[...])).astype(BF16)


def _gates(h, w_in, col_start, n):
    t, d = h.shape
    tm, tn = ROW_TILE_MM, 512
    shift = col_start % LANES
    aligned = col_start - shift
    assert 0 < shift and aligned % tn == 0 and d % GATE_CAST_ROWS == 0
    return pl.pallas_call(
        functools.partial(_gates_kernel, shift=shift),
        grid=(n // tn, t // tm),
        in_specs=[pl.BlockSpec((tm, d), lambda j, i: (i, 0)),
                  pl.BlockSpec((d, tn), lambda j, i: (0, aligned // tn + j)),
                  pl.BlockSpec((d, LANES), lambda j, i: (0, (aligned + (j + 1) * tn) // LANES))],
        out_specs=pl.BlockSpec((tm, tn), lambda j, i: (i, j)),
        out_shape=jax.ShapeDtypeStruct((t, n), BF16),
        scratch_shapes=[pltpu.VMEM((d, tn), BF16)],
        compiler_params=_params(("parallel", "arbitrary"), 48),
        name="gates",
    )(h, w_in, w_in)


def _matmul_kernel(a_ref, w_ref, o_ref):
    o_ref[...] = _dot(a_ref[...], w_ref[...]).astype(BF16)


def _matmul_f32_weights_kernel(a_ref, w_ref, o_ref, w_bf):
    _cast_weights_at_sweep_start([(w_ref, w_bf)])
    o_ref[...] = _dot(a_ref[...], w_bf[...]).astype(BF16)


def _matmul(a, w, tm, tn, name):
    t, k = a.shape
    n = w.shape[1]
    cast_in_kernel = w.dtype == F32
    return pl.pallas_call(
        _matmul_f32_weights_kernel if cast_in_kernel else _matmul_kernel,
        grid=(n // tn, t // tm),
        in_specs=[pl.BlockSpec((tm, k), lambda j, i: (i, 0)),
                  pl.BlockSpec((k, tn), lambda j, i: (0, j))],
        out_specs=pl.BlockSpec((tm, tn), lambda j, i: (i, j)),
        out_shape=jax.ShapeDtypeStruct((t, n), BF16),
        scratch_shapes=[pltpu.VMEM((k, tn), BF16)] if cast_in_kernel else [],
        compiler_params=_params(("parallel", "arbitrary"), 52),
        name=name,
    )(a, w)


def _merge_kernel(a_ref, o_ref, wa_ref, wb_ref, ga_ref, gb_ref, out_ref):
    y_a = _dot(a_ref[...], wa_ref[...])
    y_b = _dot(o_ref[...], wb_ref[...])
    out_ref[...] = (ga_ref[...] * y_a + gb_ref[...] * y_b).astype(BF16)


def _merge(a, o, w_a, w_b, gates):
    t, d = a.shape
    n = w_a.shape[1]
    tm, tn = ROW_TILE_WIDE_K, 512
    nb = n // tn
    return pl.pallas_call(
        _merge_kernel,
        grid=(nb, t // tm),
        in_specs=[
            pl.BlockSpec((tm, d), lambda j, i: (i, 0)),
            pl.BlockSpec((tm, d), lambda j, i: (i, 0)),
            pl.BlockSpec((d, tn), lambda j, i: (0, j)),
            pl.BlockSpec((d, tn), lambda j, i: (0, j)),
            pl.BlockSpec((tm, tn), lambda j, i: (i, j)),
            pl.BlockSpec((tm, tn), lambda j, i: (i, j + nb)),
        ],
        out_specs=pl.BlockSpec((tm, tn), lambda j, i: (i, j)),
        out_shape=jax.ShapeDtypeStruct((t, n), BF16),
        compiler_params=_params(("parallel", "arbitrary"), 48),
        name="merge",
    )(a, o, w_a, w_b, gates, gates)


def _ffn_up_kernel(h_ref, wg_ref, wv_ref, cwg_ref, cwv_ref, bg_ref, bv_ref, o_ref,
                   carry_g_ref, carry_v_ref, wg_bf, wv_bf, *, tiles_per_seq):
    _cast_weights_at_sweep_start([(wg_ref, wg_bf), (wv_ref, wv_bf)])
    _reset_carry_at_sequence_start([carry_g_ref, carry_v_ref], tiles_per_seq)
    h = h_ref[...]
    u_gate = _causal_conv3(_dot(h, wg_bf[...]), carry_g_ref, cwg_ref) + bg_ref[...]
    u_val = _causal_conv3(_dot(h, wv_bf[...]), carry_v_ref, cwv_ref) + bv_ref[...]
    o_ref[...] = (jax.nn.gelu(u_gate, approximate=True) * u_val).astype(BF16)


def _ffn_up(h, w_up, conv_w, conv_b, seq):
    t, d = h.shape
    d_ff = w_up.shape[1] // 2
    tm, tn = ROW_TILE_MM, 256
    nb = d_ff // tn
    conv_b = conv_b.reshape(1, 2 * d_ff)
    return pl.pallas_call(
        functools.partial(_ffn_up_kernel, tiles_per_seq=seq // tm),
        grid=(nb, t // tm),
        in_specs=[
            pl.BlockSpec((tm, d), lambda j, i: (i, 0)),
            pl.BlockSpec((d, tn), lambda j, i: (0, j)),
            pl.BlockSpec((d, tn), lambda j, i: (0, j + nb)),
            pl.BlockSpec((SHORT_K, tn), lambda j, i: (0, j)),
            pl.BlockSpec((SHORT_K, tn), lambda j, i: (0, j + nb)),
            pl.BlockSpec((1, tn), lambda j, i: (0, j)),
            pl.BlockSpec((1, tn), lambda j, i: (0, j + nb)),
        ],
        out_specs=pl.BlockSpec((tm, tn), lambda j, i: (i, j)),
        out_shape=jax.ShapeDtypeStruct((t, d_ff), BF16),
        scratch_shapes=[pltpu.VMEM((CARRY_ROWS, tn), F32)] * 2 + [pltpu.VMEM((d, tn), BF16)] * 2,
        compiler_params=_params(("parallel", "arbitrary"), 48),
        name="ffn_up",
    )(h, w_up, w_up, conv_w, conv_w, conv_b, conv_b)


def _ple_kernel(h_ref, p_ref, wg_ref, wp_ref, o_ref, wg_bf, wp_bf):
    _cast_weights_at_sweep_start([(wg_ref, wg_bf), (wp_ref, wp_bf)])
    gate = jax.nn.sigmoid(_dot(h_ref[...], wg_bf[...]))
    o_ref[...] = (gate * _dot(p_ref[...].astype(BF16), wp_bf[...])).astype(BF16)


def _ple(h, p, w_gate, w_proj):
    t, d = h.shape
    n = w_gate.shape[1]
    ple = p.shape[1]
    tm, tn = ROW_TILE_MM, 512
    return pl.pallas_call(
        _ple_kernel,
        grid=(n // tn, t // tm),
        in_specs=[
            pl.BlockSpec((tm, d), lambda j, i: (i, 0)),
            pl.BlockSpec((tm, ple), lambda j, i: (i, 0)),
            pl.BlockSpec((d, tn), lambda j, i: (0, j)),
            pl.BlockSpec((ple, tn), lambda j, i: (0, j)),
        ],
        out_specs=pl.BlockSpec((tm, tn), lambda j, i: (i, j)),
        out_shape=jax.ShapeDtypeStruct((t, n), BF16),
        scratch_shapes=[pltpu.VMEM((d, tn), BF16), pltpu.VMEM((ple, tn), BF16)],
        compiler_params=_params(("parallel", "arbitrary"), 48),
        name="ple",
    )(h, p, w_gate, w_proj)


def _split_bf16x3(x):
    hi = x.astype(BF16)
    rest = x - hi.astype(F32)
    mid = rest.astype(BF16)
    lo = (rest - mid.astype(F32)).astype(BF16)
    return hi, mid, lo


def _forget_cumsum_kernel(h_ref, wf_ref, b_ref, spread_ref, ones_ref, c_ref, augk_ref, carry_ref,
                          wf_bf, *, tiles_per_seq):
    @pl.when(pl.program_id(0) == 0)
    def _():
        wf_bf[...] = wf_ref[...].astype(BF16)

    @pl.when(pl.program_id(0) % tiles_per_seq == 0)
    def _():
        carry_ref[...] = jnp.zeros_like(carry_ref)

    ts = h_ref.shape[0]
    nh = b_ref.shape[1]
    log_f = jax.nn.log_sigmoid(_dot(h_ref[...], wf_bf[...])[:, 0:nh] + b_ref[...])
    row = lax.broadcasted_iota(jnp.int32, (ts, ts), 0)
    col = lax.broadcasted_iota(jnp.int32, (ts, ts), 1)
    lower = (col <= row).astype(F32)
    csum = jnp.dot(lower, log_f, preferred_element_type=F32,
                   precision=lax.Precision.HIGHEST) + carry_ref[...]
    c_ref[...] = csum
    carry_ref[...] = c_ref[ts - 1:ts, :]

    hi, mid, lo = _split_bf16x3(csum * LOG2_E)
    aug = (_dot(hi, spread_ref[0]) + _dot(mid, spread_ref[1]) + _dot(lo, spread_ref[2])
           + ones_ref[...])
    augk_ref[...] = aug.astype(BF16)


def _forget_cumsum(h, w_in, col_start, bias, seq):
    t, d = h.shape
    nh = bias.shape[0]
    ts = CUMSUM_TILE
    assert col_start % LANES == 0 and nh <= LANES
    width = nh * HEAD_DIM
    lane = jnp.arange(width)[None, :]
    head = jnp.arange(nh)[:, None]
    spread = jnp.stack([jnp.where(lane == head * HEAD_DIM + j, -1.0, 0.0) for j in range(3)])
    ones = jnp.where((lane % HEAD_DIM >= 3) & (lane % HEAD_DIM < 6), 1.0, 0.0).astype(F32)
    return pl.pallas_call(
        functools.partial(_forget_cumsum_kernel, tiles_per_seq=seq // ts),
        grid=(t // ts,),
        in_specs=[pl.BlockSpec((ts, d), lambda i: (i, 0)),
                  pl.BlockSpec((d, LANES), lambda i: (0, col_start // LANES)),
                  pl.BlockSpec((1, nh), lambda i: (0, 0)),
                  pl.BlockSpec((3, nh, width), lambda i: (0, 0, 0)),
                  pl.BlockSpec((1, width), lambda i: (0, 0))],
        out_specs=[pl.BlockSpec((ts, nh), lambda i: (i, 0)),
                   pl.BlockSpec((ts, width), lambda i: (i, 0))],
        out_shape=[jax.ShapeDtypeStruct((t, nh), F32), jax.ShapeDtypeStruct((t, width), BF16)],
        scratch_shapes=[pltpu.VMEM((1, nh), F32), pltpu.VMEM((d, LANES), BF16)],
        compiler_params=_params(("arbitrary",), 48),
        name="forget_cumsum",
    )(h, w_in, bias.reshape(1, nh), spread.astype(BF16), ones)


AUG_ROWS = 16
SUM_ROWS = 16
HEADS_PER_STEP = 2


def _attention_kernel(q_ref, k_ref, v_ref, augk_ref, c_ref, o_ref,
                      kx_ref, vt_ref, qxt_ref, *buffers, tile):
    n = HEADS_PER_STEP
    s_ref = [buffers[2 * g:2 * g + 2] for g in range(n)]
    p_ref = [buffers[2 * n + 2 * g:2 * n + 2 * g + 2] for g in range(n)]
    acc_ref = buffers[4 * n:]
    kt = tile // 2
    qi = pl.program_id(2)
    n_key_tiles = kx_ref.shape[1]
    heads = range(HEADS_PER_STEP)

    def lanes(g):
        return slice(g * HEAD_DIM, (g + 1) * HEAD_DIM)

    @pl.when(qi == 0)
    def _():
        def stage(j, _):
            rows = pl.ds(pl.multiple_of(j * kt, kt), kt)
            for g in heads:
                kx_ref[g, j, :, 0:HEAD_DIM] = k_ref[rows, lanes(g)]
                kx_ref[g, j, :, HEAD_DIM:] = augk_ref[rows, lanes(g)]
                vt_ref[g, j, 0:HEAD_DIM, :] = v_ref[rows, lanes(g)].astype(F32).T.astype(BF16)
                vt_ref[g, j, HEAD_DIM:, :] = jnp.ones((SUM_ROWS, kt), BF16)
            return 0
        lax.fori_loop(0, n_key_tiles, stage, 0)
        for g in heads:
            qxt_ref[g, HEAD_DIM + AUG_ROWS:, :] = jnp.zeros((HEAD_DIM - AUG_ROWS, tile), BF16)

    r = lax.broadcasted_iota(jnp.int32, (AUG_ROWS, tile), 0)
    for g in heads:
        qxt_ref[g, 0:HEAD_DIM, :] = q_ref[:, lanes(g)].astype(F32).T.astype(BF16)
        hi, mid, lo = _split_bf16x3(c_ref[g, qi] * LOG2_E)
        aug = jnp.where(r < 3, 1.0,
                        jnp.where(r == 3, hi.astype(F32),
                                  jnp.where(r == 4, mid.astype(F32),
                                            jnp.where(r == 5, lo.astype(F32), 0.0))))
        qxt_ref[g, HEAD_DIM:HEAD_DIM + AUG_ROWS, :] = aug.astype(BF16)

    def scores_into(g, slot, ki, mask=None):
        s = _dot(kx_ref[g, ki], qxt_ref[g])
        if mask is not None:
            s = jnp.where(mask, s, NEG_INF)
        s_ref[g][slot][...] = s
        return jnp.max(s, axis=0, keepdims=True)

    def softmax_into(g, slot, m, tile_max):
        m_new = jnp.maximum(m, tile_max)
        p_ref[g][slot][...] = jnp.exp2(s_ref[g][slot][...] - m_new).astype(BF16)
        return m_new, jnp.exp2(m - m_new)

    def accumulate(g, slot, alpha, ki):
        acc_ref[g][...] = alpha * acc_ref[g][...] + _dot(vt_ref[g, ki], p_ref[g][slot][...])

    for g in heads:
        p_ref[g][1][...] = jnp.zeros((kt, tile), BF16)
        acc_ref[g][...] = jnp.zeros(acc_ref[g].shape, F32)
    max_0 = tuple(scores_into(g, 0, 0) for g in heads)

    def pair(jj, carry):
        m, alpha_1, max_0 = carry
        k0 = 2 * jj
        max_1 = [scores_into(g, 1, k0 + 1) for g in heads]
        m, alpha_0 = zip(*[softmax_into(g, 0, m[g], max_0[g]) for g in heads])
        for g in heads:
            accumulate(g, 1, alpha_1[g], jnp.maximum(k0 - 1, 0))
        max_0 = tuple(scores_into(g, 0, k0 + 2) for g in heads)
        m, alpha_1 = zip(*[softmax_into(g, 1, m[g], max_1[g]) for g in heads])
        for g in heads:
            accumulate(g, 0, alpha_0[g], k0)
        return m, alpha_1, max_0

    init = (tuple(jnp.full((1, tile), NEG_INF, F32) for _ in heads),
            tuple(jnp.ones((1, tile), F32) for _ in heads), max_0)
    m, alpha_1, _ = lax.fori_loop(0, qi, pair, init)

    k0 = 2 * qi
    key = lax.broadcasted_iota(jnp.int32, (kt, tile), 0)
    query = lax.broadcasted_iota(jnp.int32, (kt, tile), 1)
    max_1 = [scores_into(g, 1, k0 + 1, mask=key + kt <= query) for g in heads]
    for g in heads:
        s_ref[g][0][...] = jnp.where(key <= query, s_ref[g][0][...], NEG_INF)
    max_0 = [jnp.max(s_ref[g][0][...], axis=0, keepdims=True) for g in heads]
    m, alpha_0 = zip(*[softmax_into(g, 0, m[g], max_0[g]) for g in heads])
    for g in heads:
        accumulate(g, 1, alpha_1[g], jnp.maximum(k0 - 1, 0))
    m, alpha_1 = zip(*[softmax_into(g, 1, m[g], max_1[g]) for g in heads])
    for g in heads:
        accumulate(g, 0, alpha_0[g], k0)
    for g in heads:
        accumulate(g, 1, alpha_1[g], k0 + 1)
    for g in heads:
        o_t = acc_ref[g][0:HEAD_DIM, :] / acc_ref[g][HEAD_DIM:HEAD_DIM + 1, :]
        o_ref[:, lanes(g)] = o_t.T.astype(BF16)


def _attention(qkv, augk, c, batch, seq):
    t = qkv.shape[0]
    tile = ATTN_TILE
    kt = tile // 2
    nq = seq // tile
    g = HEADS_PER_STEP
    width = g * HEAD_DIM
    groups = N_HEADS // g
    c_rows = jnp.transpose(c.reshape(batch, seq, N_HEADS), (0, 2, 1)).reshape(
        batch * N_HEADS, nq, 1, tile)
    return pl.pallas_call(
        functools.partial(_attention_kernel, tile=tile),
        grid=(batch, groups, nq),
        in_specs=[
            pl.BlockSpec((tile, width), lambda b, h, i: (b * nq + i, h)),
            pl.BlockSpec((seq, width), lambda b, h, i: (b, groups + h), pipeline_mode=pl.Buffered(1)),
            pl.BlockSpec((seq, width), lambda b, h, i: (b, 2 * groups + h), pipeline_mode=pl.Buffered(1)),
            pl.BlockSpec((seq, width), lambda b, h, i: (b, h), pipeline_mode=pl.Buffered(1)),
            pl.BlockSpec((g, nq, 1, tile), lambda b, h, i: (b * groups + h, 0, 0, 0)),
        ],
        out_specs=pl.BlockSpec((tile, width), lambda b, h, i: (b * nq + i, h)),
        out_shape=jax.ShapeDtypeStruct((t, N_HEADS * HEAD_DIM), BF16),
        scratch_shapes=[pltpu.VMEM((g, seq // kt, kt, 2 * HEAD_DIM), BF16),
                        pltpu.VMEM((g, seq // kt, HEAD_DIM + SUM_ROWS, kt), BF16),
                        pltpu.VMEM((g, 2 * HEAD_DIM, tile), BF16),
                        *[pltpu.VMEM((kt, tile), F32)] * (2 * g),
                        *[pltpu.VMEM((kt, tile), BF16)] * (2 * g),
                        *[pltpu.VMEM((HEAD_DIM + SUM_ROWS, tile), F32)] * g],
        compiler_params=_params(("parallel", "parallel", "arbitrary"), 56),
        name="fox_attention",
    )(qkv, qkv, qkv, augk, c_rows)


def kernel(x, p, norm_mix_pre, w_in, forget_bias, conv_mix_w, w_branch_conv, w_branch_attn, w_out,
           norm_mix_post, norm_ffn_pre, w_up, ffn_conv_w, ffn_conv_b, w_down, norm_ffn_post,
           w_ple_proj, norm_ple_gate, w_ple_gate, norm_ple_post):
    batch, seq, d = x.shape
    t = batch * seq
    depth = w_in.shape[0]
    conv_width = conv_mix_w.shape[-1]
    attn_width = N_HEADS * HEAD_DIM
    qkv_start = 3 * conv_width
    forget_start = qkv_start + 3 * attn_width
    gate_start = forget_start + N_HEADS

    xs = x.reshape(t, d)
    for i in range(depth):
        w_in_i = w_in[i]

        h1 = _norm(xs, norm_mix_pre[i])
        a = _conv_branch(h1, w_in_i, conv_mix_w[i], seq)
        qkv = _qkv(h1, w_in_i, qkv_start, 3 * attn_width)
        gates = _gates(h1, w_in_i, gate_start, 2 * d)
        c, augk = _forget_cumsum(h1, w_in_i, forget_start, forget_bias[i], seq)
        o = _attention(qkv, augk, c, batch, seq)
        merged = _merge(a, o, w_branch_conv[i].astype(BF16), w_branch_attn[i].astype(BF16), gates)
        y1 = _matmul(merged, w_out[i], ROW_TILE_MM, 512, "out_proj")
        x1, h2 = _residual_norm(xs, y1, norm_mix_post[i], norm_ffn_pre[i])

        ffn = _ffn_up(h2, w_up[i], ffn_conv_w[i], ffn_conv_b[i], seq)
        y2 = _matmul(ffn, w_down[i].astype(BF16), ROW_TILE_WIDE_K, 512, "ffn_down")
        x2, h3 = _residual_norm(x1, y2, norm_ffn_post[i], norm_ple_gate[i])

        ge = _ple(h3, p[i].reshape(t, -1), w_ple_gate[i], w_ple_proj[i])
        xs = _residual(x2, ge, norm_ple_post[i])
    return xs.reshape(batch, seq, d)
```

```python
import functools
import math

import jax
import jax.numpy as jnp
from jax import lax
from jax.experimental import pallas as pl
from jax.experimental.pallas import tpu as pltpu

BF16 = jnp.bfloat16
F32 = jnp.float32

LANES = 128
BF16_SUBLANES = 16
N_HEADS = 32
HEAD_DIM = 128
SHORT_K = 3
EPS = 1e-6
NEG_INF = -1e30
LOG2_E = math.log2(math.e)
MIB = 1024 * 1024

ROW_TILE_NORM = 256
ROW_TILE_MM = 1024
ROW_TILE_WIDE_K = 512
ATTN_TILE = 1024
CUMSUM_TILE = 512
CARRY_ROWS = 8


def _params(semantics, vmem_mib, flags=None):
    return pltpu.CompilerParams(dimension_semantics=semantics, vmem_limit_bytes=vmem_mib * MIB,
                                flags=flags)


def _dot(a, b):
    return jnp.dot(a, b, preferred_element_type=F32)


def _dot_nt(a, b):
    return lax.dot_general(a, b, (((1,), (1,)), ((), ())), preferred_element_type=F32)


def _rms_scale(x, g):
    inv = lax.rsqrt(jnp.mean(x * x, axis=-1, keepdims=True) + EPS)
    return x * inv * g


def _norm_kernel(x_ref, g_ref, h_ref):
    h_ref[...] = _rms_scale(x_ref[...], g_ref[...]).astype(BF16)


def _residual_norm_kernel(x_ref, y_ref, g_post_ref, g_pre_ref, x_out_ref, h_ref):
    x_new = x_ref[...] + _rms_scale(y_ref[...].astype(F32), g_post_ref[...])
    x_out_ref[...] = x_new
    h_ref[...] = _rms_scale(x_new, g_pre_ref[...]).astype(BF16)


def _residual_kernel(x_ref, y_ref, g_post_ref, x_out_ref):
    x_out_ref[...] = x_ref[...] + _rms_scale(y_ref[...].astype(F32), g_post_ref[...])


def _row_spec(tr, d):
    return pl.BlockSpec((tr, d), lambda i: (i, 0))


def _gain_spec(d):
    return pl.BlockSpec((1, d), lambda i: (0, 0))


def _norm(x, g):
    t, d = x.shape
    tr = ROW_TILE_NORM
    return pl.pallas_call(
        _norm_kernel,
        grid=(t // tr,),
        in_specs=[_row_spec(tr, d), _gain_spec(d)],
        out_specs=_row_spec(tr, d),
        out_shape=jax.ShapeDtypeStruct((t, d), BF16),
        compiler_params=_params(("parallel",), 40),
        name="norm",
    )(x, g.reshape(1, d))


def _residual_norm(x, y, g_post, g_pre):
    t, d = x.shape
    tr = ROW_TILE_NORM
    return pl.pallas_call(
        _residual_norm_kernel,
        grid=(t // tr,),
        in_specs=[_row_spec(tr, d), _row_spec(tr, d), _gain_spec(d), _gain_spec(d)],
        out_specs=[_row_spec(tr, d), _row_spec(tr, d)],
        out_shape=[jax.ShapeDtypeStruct((t, d), F32), jax.ShapeDtypeStruct((t, d), BF16)],
        compiler_params=_params(("parallel",), 40),
        name="residual_norm",
    )(x, y, g_post.reshape(1, d), g_pre.reshape(1, d))


def _residual(x, y, g_post):
    t, d = x.shape
    tr = ROW_TILE_NORM
    return pl.pallas_call(
        _residual_kernel,
        grid=(t // tr,),
        in_specs=[_row_spec(tr, d), _row_spec(tr, d), _gain_spec(d)],
        out_specs=_row_spec(tr, d),
        out_shape=jax.ShapeDtypeStruct((t, d), F32),
        compiler_params=_params(("parallel",), 40),
        name="residual",
    )(x, y, g_post.reshape(1, d))


def _causal_conv3(u, carry_ref, w_ref):
    tm = u.shape[0]
    rows = lax.broadcasted_iota(jnp.int32, u.shape, 0)
    prev1 = carry_ref[CARRY_ROWS - 1:CARRY_ROWS, :]
    prev2 = carry_ref[CARRY_ROWS - 2:CARRY_ROWS - 1, :]
    back1 = jnp.where(rows == 0, prev1, pltpu.roll(u, 1, 0))
    back2 = jnp.where(rows == 0, prev2, jnp.where(rows == 1, prev1, pltpu.roll(u, 2, 0)))
    carry_ref[...] = u[tm - CARRY_ROWS:, :]
    return w_ref[0:1, :] * back2 + w_ref[1:2, :] * back1 + w_ref[2:3, :] * u


def _reset_carry_at_sequence_start(carry_refs, tiles_per_seq):
    @pl.when(pl.program_id(1) % tiles_per_seq == 0)
    def _():
        for ref in carry_refs:
            ref[...] = jnp.zeros_like(ref)


def _cast_weights_at_sweep_start(pairs):
    @pl.when(pl.program_id(1) == 0)
    def _():
        for w_ref, w_bf16_ref in pairs:
            w_bf16_ref[...] = w_ref[...].astype(BF16)


def _conv_branch_kernel(h_ref, wb_ref, wc_ref, wv_ref, cw_ref, o_ref, carry_ref,
                        wb_bf, wc_bf, wv_bf, *, tiles_per_seq):
    _cast_weights_at_sweep_start([(wb_ref, wb_bf), (wc_ref, wc_bf), (wv_ref, wv_bf)])
    _reset_carry_at_sequence_start([carry_ref], tiles_per_seq)
    h = h_ref[...]
    u = _dot_nt(h, wc_bf[...]) * _dot_nt(h, wv_bf[...])
    y = _causal_conv3(u, carry_ref, cw_ref)
    o_ref[...] = (_dot_nt(h, wb_bf[...]) * y).astype(BF16)


def _conv_branch(h, w_in_t, conv_w, seq):
    t, d = h.shape
    width = conv_w.shape[1]
    tm, tn = ROW_TILE_WIDE_K, 256
    nb = width // tn
    return pl.pallas_call(
        functools.partial(_conv_branch_kernel, tiles_per_seq=seq // tm),
        grid=(nb, t // tm),
        in_specs=[
            pl.BlockSpec((tm, d), lambda j, i: (i, 0)),
            pl.BlockSpec((tn, d), lambda j, i: (j, 0)),
            pl.BlockSpec((tn, d), lambda j, i: (j + nb, 0)),
            pl.BlockSpec((tn, d), lambda j, i: (j + 2 * nb, 0)),
            pl.BlockSpec((SHORT_K, tn), lambda j, i: (0, j)),
        ],
        out_specs=pl.BlockSpec((tm, tn), lambda j, i: (i, j)),
        out_shape=jax.ShapeDtypeStruct((t, width), BF16),
        scratch_shapes=[pltpu.VMEM((CARRY_ROWS, tn), F32)] + [pltpu.VMEM((tn, d), BF16)] * 3,
        compiler_params=_params(("parallel", "arbitrary"), 48),
        name="conv_branch",
    )(h, w_in_t, w_in_t, w_in_t, conv_w)


def _qkv_kernel(h_ref, w_ref, o_ref, w_bf, *, q_tiles, scale):
    _cast_weights_at_sweep_start([(w_ref, w_bf)])
    acc = _dot_nt(h_ref[...], w_bf[...])
    factor = jnp.where(pl.program_id(0) < q_tiles, scale, 1.0).astype(F32)
    o_ref[...] = (acc * factor).astype(BF16)


def _qkv(h, w_in_t, row_start, n):
    t, d = h.shape
    tm, tn = ROW_TILE_MM, 512
    first = row_start // tn
    kern = functools.partial(_qkv_kernel, q_tiles=(n // 3) // tn, scale=LOG2_E / math.sqrt(HEAD_DIM))
    return pl.pallas_call(
        kern,
        grid=(n // tn, t // tm),
        in_specs=[pl.BlockSpec((tm, d), lambda j, i: (i, 0)),
                  pl.BlockSpec((tn, d), lambda j, i: (first + j, 0))],
        out_specs=pl.BlockSpec((tm, tn), lambda j, i: (i, j)),
        out_shape=jax.ShapeDtypeStruct((t, n), BF16),
        scratch_shapes=[pltpu.VMEM((tn, d), BF16)],
        compiler_params=_params(("parallel", "arbitrary"), 48),
        name="qkv",
    )(h, w_in_t)


def _gates_kernel(h_ref, w_ref, w_next_ref, o_ref, w_bf, *, shift):
    @pl.when(pl.program_id(1) == 0)
    def _():
        tn = w_bf.shape[0]
        w_bf[0:tn - shift, :] = w_ref[shift:tn, :].astype(BF16)
        w_bf[tn - shift:tn, :] = w_next_ref[0:shift, :].astype(BF16)

    o_ref[...] = jax.nn.sigmoid(_dot_nt(h_ref[...], w_bf[...])).astype(BF16)


def _gates(h, w_in_t, row_start, n):
    t, d = h.shape
    tm, tn = ROW_TILE_MM, 512
    shift = row_start % LANES
    aligned = row_start - shift
    assert 0 < shift and shift % BF16_SUBLANES == 0 and aligned % tn == 0
    return pl.pallas_call(
        functools.partial(_gates_kernel, shift=shift),
        grid=(n // tn, t // tm),
        in_specs=[pl.BlockSpec((tm, d), lambda j, i: (i, 0)),
                  pl.BlockSpec((tn, d), lambda j, i: (aligned // tn + j, 0)),
                  pl.BlockSpec((LANES, d), lambda j, i: ((aligned + (j + 1) * tn) // LANES, 0))],
        out_specs=pl.BlockSpec((tm, tn), lambda j, i: (i, j)),
        out_shape=jax.ShapeDtypeStruct((t, n), BF16),
        scratch_shapes=[pltpu.VMEM((tn, d), BF16)],
        compiler_params=_params(("parallel", "arbitrary"), 48),
        name="gates",
    )(h, w_in_t, w_in_t)


def _matmul_kernel(a_ref, w_ref, o_ref):
    o_ref[...] = _dot(a_ref[...], w_ref[...]).astype(BF16)


def _matmul_f32_weights_kernel(a_ref, w_ref, o_ref, w_bf):
    _cast_weights_at_sweep_start([(w_ref, w_bf)])
    o_ref[...] = _dot(a_ref[...], w_bf[...]).astype(BF16)


def _matmul(a, w, tm, tn, name):
    t, k = a.shape
    n = w.shape[1]
    cast_in_kernel = w.dtype == F32
    return pl.pallas_call(
        _matmul_f32_weights_kernel if cast_in_kernel else _matmul_kernel,
        grid=(n // tn, t // tm),
        in_specs=[pl.BlockSpec((tm, k), lambda j, i: (i, 0)),
                  pl.BlockSpec((k, tn), lambda j, i: (0, j))],
        out_specs=pl.BlockSpec((tm, tn), lambda j, i: (i, j)),
        out_shape=jax.ShapeDtypeStruct((t, n), BF16),
        scratch_shapes=[pltpu.VMEM((k, tn), BF16)] if cast_in_kernel else [],
        compiler_params=_params(("parallel", "arbitrary"), 52),
        name=name,
    )(a, w)


def _merge_kernel(a_ref, o_ref, wa_ref, wb_ref, ga_ref, gb_ref, out_ref):
    y_a = _dot(a_ref[...], wa_ref[...])
    y_b = _dot(o_ref[...], wb_ref[...])
    out_ref[...] = (ga_ref[...] * y_a + gb_ref[...] * y_b).astype(BF16)


def _merge(a, o, w_a, w_b, gates):
    t, d = a.shape
    n = w_a.shape[1]
    tm, tn = ROW_TILE_WIDE_K, 512
    nb = n // tn
    return pl.pallas_call(
        _merge_kernel,
        grid=(nb, t // tm),
        in_specs=[
            pl.BlockSpec((tm, d), lambda j, i: (i, 0)),
            pl.BlockSpec((tm, d), lambda j, i: (i, 0)),
            pl.BlockSpec((d, tn), lambda j, i: (0, j)),
            pl.BlockSpec((d, tn), lambda j, i: (0, j)),
            pl.BlockSpec((tm, tn), lambda j, i: (i, j)),
            pl.BlockSpec((tm, tn), lambda j, i: (i, j + nb)),
        ],
        out_specs=pl.BlockSpec((tm, tn), lambda j, i: (i, j)),
        out_shape=jax.ShapeDtypeStruct((t, n), BF16),
        compiler_params=_params(("parallel", "arbitrary"), 48),
        name="merge",
    )(a, o, w_a, w_b, gates, gates)


def _ffn_up_kernel(h_ref, wg_ref, wv_ref, cwg_ref, cwv_ref, bg_ref, bv_ref, o_ref,
                   carry_g_ref, carry_v_ref, wg_bf, wv_bf, *, tiles_per_seq):
    _cast_weights_at_sweep_start([(wg_ref, wg_bf), (wv_ref, wv_bf)])
    _reset_carry_at_sequence_start([carry_g_ref, carry_v_ref], tiles_per_seq)
    h = h_ref[...]
    u_gate = _causal_conv3(_dot(h, wg_bf[...]), carry_g_ref, cwg_ref) + bg_ref[...]
    u_val = _causal_conv3(_dot(h, wv_bf[...]), carry_v_ref, cwv_ref) + bv_ref[...]
    o_ref[...] = (jax.nn.gelu(u_gate, approximate=True) * u_val).astype(BF16)


def _ffn_up(h, w_up, conv_w, conv_b, seq):
    t, d = h.shape
    d_ff = w_up.shape[1] // 2
    tm, tn = ROW_TILE_MM, 256
    nb = d_ff // tn
    conv_b = conv_b.reshape(1, 2 * d_ff)
    return pl.pallas_call(
        functools.partial(_ffn_up_kernel, tiles_per_seq=seq // tm),
        grid=(nb, t // tm),
        in_specs=[
            pl.BlockSpec((tm, d), lambda j, i: (i, 0)),
            pl.BlockSpec((d, tn), lambda j, i: (0, j)),
            pl.BlockSpec((d, tn), lambda j, i: (0, j + nb)),
            pl.BlockSpec((SHORT_K, tn), lambda j, i: (0, j)),
            pl.BlockSpec((SHORT_K, tn), lambda j, i: (0, j + nb)),
            pl.BlockSpec((1, tn), lambda j, i: (0, j)),
            pl.BlockSpec((1, tn), lambda j, i: (0, j + nb)),
        ],
        out_specs=pl.BlockSpec((tm, tn), lambda j, i: (i, j)),
        out_shape=jax.ShapeDtypeStruct((t, d_ff), BF16),
        scratch_shapes=[pltpu.VMEM((CARRY_ROWS, tn), F32)] * 2 + [pltpu.VMEM((d, tn), BF16)] * 2,
        compiler_params=_params(("parallel", "arbitrary"), 48),
        name="ffn_up",
    )(h, w_up, w_up, conv_w, conv_w, conv_b, conv_b)


def _ple_kernel(h_ref, p_ref, wg_ref, wp_ref, o_ref, wg_bf, wp_bf):
    _cast_weights_at_sweep_start([(wg_ref, wg_bf), (wp_ref, wp_bf)])
    gate = jax.nn.sigmoid(_dot(h_ref[...], wg_bf[...]))
    o_ref[...] = (gate * _dot(p_ref[...].astype(BF16), wp_bf[...])).astype(BF16)


def _ple(h, p, w_gate, w_proj):
    t, d = h.shape
    n = w_gate.shape[1]
    ple = p.shape[1]
    tm, tn = ROW_TILE_MM, 512
    return pl.pallas_call(
        _ple_kernel,
        grid=(n // tn, t // tm),
        in_specs=[
            pl.BlockSpec((tm, d), lambda j, i: (i, 0)),
            pl.BlockSpec((tm, ple), lambda j, i: (i, 0)),
            pl.BlockSpec((d, tn), lambda j, i: (0, j)),
            pl.BlockSpec((ple, tn), lambda j, i: (0, j)),
        ],
        out_specs=pl.BlockSpec((tm, tn), lambda j, i: (i, j)),
        out_shape=jax.ShapeDtypeStruct((t, n), BF16),
        scratch_shapes=[pltpu.VMEM((d, tn), BF16), pltpu.VMEM((ple, tn), BF16)],
        compiler_params=_params(("parallel", "arbitrary"), 48),
        name="ple",
    )(h, p, w_gate, w_proj)


def _split_bf16x3(x):
    hi = x.astype(BF16)
    rest = x - hi.astype(F32)
    mid = rest.astype(BF16)
    lo = (rest - mid.astype(F32)).astype(BF16)
    return hi, mid, lo


def _forget_cumsum_kernel(h_ref, wf_ref, b_ref, spread_ref, ones_ref, c_ref, augk_ref, carry_ref,
                          wf_bf, *, tiles_per_seq):
    @pl.when(pl.program_id(0) == 0)
    def _():
        wf_bf[...] = wf_ref[...].astype(BF16)

    @pl.when(pl.program_id(0) % tiles_per_seq == 0)
    def _():
        carry_ref[...] = jnp.zeros_like(carry_ref)

    ts = h_ref.shape[0]
    nh = b_ref.shape[1]
    log_f = jax.nn.log_sigmoid(_dot_nt(h_ref[...], wf_bf[...])[:, 0:nh] + b_ref[...])
    row = lax.broadcasted_iota(jnp.int32, (ts, ts), 0)
    col = lax.broadcasted_iota(jnp.int32, (ts, ts), 1)
    lower = (col <= row).astype(F32)
    csum = jnp.dot(lower, log_f, preferred_element_type=F32,
                   precision=lax.Precision.HIGHEST) + carry_ref[...]
    c_ref[...] = csum
    carry_ref[...] = c_ref[ts - 1:ts, :]

    hi, mid, lo = _split_bf16x3(csum * LOG2_E)
    aug = (_dot(hi, spread_ref[0]) + _dot(mid, spread_ref[1]) + _dot(lo, spread_ref[2])
           + ones_ref[...])
    augk_ref[...] = aug.astype(BF16)


def _forget_cumsum(h, w_in_t, row_start, bias, seq):
    t, d = h.shape
    nh = bias.shape[0]
    ts = CUMSUM_TILE
    assert row_start % LANES == 0 and nh <= LANES
    width = nh * HEAD_DIM
    lane = jnp.arange(width)[None, :]
    head = jnp.arange(nh)[:, None]
    spread = jnp.stack([jnp.where(lane == head * HEAD_DIM + j, -1.0, 0.0) for j in range(3)])
    ones = jnp.where((lane % HEAD_DIM >= 3) & (lane % HEAD_DIM < 6), 1.0, 0.0).astype(F32)
    return pl.pallas_call(
        functools.partial(_forget_cumsum_kernel, tiles_per_seq=seq // ts),
        grid=(t // ts,),
        in_specs=[pl.BlockSpec((ts, d), lambda i: (i, 0)),
                  pl.BlockSpec((LANES, d), lambda i: (row_start // LANES, 0)),
                  pl.BlockSpec((1, nh), lambda i: (0, 0)),
                  pl.BlockSpec((3, nh, width), lambda i: (0, 0, 0)),
                  pl.BlockSpec((1, width), lambda i: (0, 0))],
        out_specs=[pl.BlockSpec((ts, nh), lambda i: (i, 0)),
                   pl.BlockSpec((ts, width), lambda i: (i, 0))],
        out_shape=[jax.ShapeDtypeStruct((t, nh), F32), jax.ShapeDtypeStruct((t, width), BF16)],
        scratch_shapes=[pltpu.VMEM((1, nh), F32), pltpu.VMEM((LANES, d), BF16)],
        compiler_params=_params(("arbitrary",), 48),
        name="forget_cumsum",
    )(h, w_in_t, bias.reshape(1, nh), spread.astype(BF16), ones)


AUG_ROWS = 16
SUM_ROWS = 16
HEADS_PER_STEP = 2


def _attention_kernel(q_ref, k_ref, v_ref, augk_ref, c_ref, o_ref,
                      kx_ref, vt_ref, qxt_ref, *buffers, tile):
    n = HEADS_PER_STEP
    s_ref = [buffers[2 * g:2 * g + 2] for g in range(n)]
    p_ref = [buffers[2 * n + 2 * g:2 * n + 2 * g + 2] for g in range(n)]
    acc_ref = buffers[4 * n:]
    kt = tile // 2
    qi = pl.program_id(2)
    n_key_tiles = kx_ref.shape[1]
    heads = range(HEADS_PER_STEP)

    def lanes(g):
        return slice(g * HEAD_DIM, (g + 1) * HEAD_DIM)

    @pl.when(qi == 0)
    def _():
        def stage(j, _):
            rows = pl.ds(pl.multiple_of(j * kt, kt), kt)
            for g in heads:
                kx_ref[g, j, :, 0:HEAD_DIM] = k_ref[rows, lanes(g)]
                kx_ref[g, j, :, HEAD_DIM:] = augk_ref[rows, lanes(g)]
                vt_ref[g, j, 0:HEAD_DIM, :] = v_ref[rows, lanes(g)].astype(F32).T.astype(BF16)
                vt_ref[g, j, HEAD_DIM:, :] = jnp.ones((SUM_ROWS, kt), BF16)
            return 0
        lax.fori_loop(0, n_key_tiles, stage, 0)
        for g in heads:
            qxt_ref[g, HEAD_DIM + AUG_ROWS:, :] = jnp.zeros((HEAD_DIM - AUG_ROWS, tile), BF16)

    r = lax.broadcasted_iota(jnp.int32, (AUG_ROWS, tile), 0)
    for g in heads:
        qxt_ref[g, 0:HEAD_DIM, :] = q_ref[:, lanes(g)].astype(F32).T.astype(BF16)
        hi, mid, lo = _split_bf16x3(c_ref[g, qi] * LOG2_E)
        aug = jnp.where(r < 3, 1.0,
                        jnp.where(r == 3, hi.astype(F32),
                                  jnp.where(r == 4, mid.astype(F32),
                                            jnp.where(r == 5, lo.astype(F32), 0.0))))
        qxt_ref[g, HEAD_DIM:HEAD_DIM + AUG_ROWS, :] = aug.astype(BF16)

    def scores_into(g, slot, ki, mask=None):
        s = _dot(kx_ref[g, ki], qxt_ref[g])
        if mask is not None:
            s = jnp.where(mask, s, NEG_INF)
        s_ref[g][slot][...] = s
        return jnp.max(s, axis=0, keepdims=True)

    def softmax_into(g, slot, m, tile_max):
        m_new = jnp.maximum(m, tile_max)
        p_ref[g][slot][...] = jnp.exp2(s_ref[g][slot][...] - m_new).astype(BF16)
        return m_new, jnp.exp2(m - m_new)

    def accumulate(g, slot, alpha, ki):
        acc_ref[g][...] = alpha * acc_ref[g][...] + _dot(vt_ref[g, ki], p_ref[g][slot][...])

    for g in heads:
        p_ref[g][1][...] = jnp.zeros((kt, tile), BF16)
        acc_ref[g][...] = jnp.zeros(acc_ref[g].shape, F32)
    max_0 = tuple(scores_into(g, 0, 0) for g in heads)

    def pair(jj, carry):
        m, alpha_1, max_0 = carry
        k0 = 2 * jj
        max_1 = [scores_into(g, 1, k0 + 1) for g in heads]
        m, alpha_0 = zip(*[softmax_into(g, 0, m[g], max_0[g]) for g in heads])
        for g in heads:
            accumulate(g, 1, alpha_1[g], jnp.maximum(k0 - 1, 0))
        max_0 = tuple(scores_into(g, 0, k0 + 2) for g in heads)
        m, alpha_1 = zip(*[softmax_into(g, 1, m[g], max_1[g]) for g in heads])
        for g in heads:
            accumulate(g, 0, alpha_0[g], k0)
        return m, alpha_1, max_0

    init = (tuple(jnp.full((1, tile), NEG_INF, F32) for _ in heads),
            tuple(jnp.ones((1, tile), F32) for _ in heads), max_0)
    m, alpha_1, _ = lax.fori_loop(0, qi, pair, init)

    k0 = 2 * qi
    key = lax.broadcasted_iota(jnp.int32, (kt, tile), 0)
    query = lax.broadcasted_iota(jnp.int32, (kt, tile), 1)
    max_1 = [scores_into(g, 1, k0 + 1, mask=key + kt <= query) for g in heads]
    for g in heads:
        s_ref[g][0][...] = jnp.where(key <= query, s_ref[g][0][...], NEG_INF)
    max_0 = [jnp.max(s_ref[g][0][...], axis=0, keepdims=True) for g in heads]
    m, alpha_0 = zip(*[softmax_into(g, 0, m[g], max_0[g]) for g in heads])
    for g in heads:
        accumulate(g, 1, alpha_1[g], jnp.maximum(k0 - 1, 0))
    m, alpha_1 = zip(*[softmax_into(g, 1, m[g], max_1[g]) for g in heads])
    for g in heads:
        accumulate(g, 0, alpha_0[g], k0)
    for g in heads:
        accumulate(g, 1, alpha_1[g], k0 + 1)
    for g in heads:
        o_t = acc_ref[g][0:HEAD_DIM, :] / acc_ref[g][HEAD_DIM:HEAD_DIM + 1, :]
        o_ref[:, lanes(g)] = o_t.T.astype(BF16)


def _attention(qkv, augk, c, batch, seq):
    t = qkv.shape[0]
    tile = ATTN_TILE
    kt = tile // 2
    nq = seq // tile
    g = HEADS_PER_STEP
    width = g * HEAD_DIM
    groups = N_HEADS // g
    c_rows = jnp.transpose(c.reshape(batch, seq, N_HEADS), (0, 2, 1)).reshape(
        batch * N_HEADS, nq, 1, tile)
    return pl.pallas_call(
        functools.partial(_attention_kernel, tile=tile),
        grid=(batch, groups, nq),
        in_specs=[
            pl.BlockSpec((tile, width), lambda b, h, i: (b * nq + i, h)),
            pl.BlockSpec((seq, width), lambda b, h, i: (b, groups + h), pipeline_mode=pl.Buffered(1)),
            pl.BlockSpec((seq, width), lambda b, h, i: (b, 2 * groups + h), pipeline_mode=pl.Buffered(1)),
            pl.BlockSpec((seq, width), lambda b, h, i: (b, h), pipeline_mode=pl.Buffered(1)),
            pl.BlockSpec((g, nq, 1, tile), lambda b, h, i: (b * groups + h, 0, 0, 0)),
        ],
        out_specs=pl.BlockSpec((tile, width), lambda b, h, i: (b * nq + i, h)),
        out_shape=jax.ShapeDtypeStruct((t, N_HEADS * HEAD_DIM), BF16),
        scratch_shapes=[pltpu.VMEM((g, seq // kt, kt, 2 * HEAD_DIM), BF16),
                        pltpu.VMEM((g, seq // kt, HEAD_DIM + SUM_ROWS, kt), BF16),
                        pltpu.VMEM((g, 2 * HEAD_DIM, tile), BF16),
                        *[pltpu.VMEM((kt, tile), F32)] * (2 * g),
                        *[pltpu.VMEM((kt, tile), BF16)] * (2 * g),
                        *[pltpu.VMEM((HEAD_DIM + SUM_ROWS, tile), F32)] * g],
        compiler_params=_params(("parallel", "parallel", "arbitrary"), 56),
        name="fox_attention",
    )(qkv, qkv, qkv, augk, c_rows)


def kernel(x, p, norm_mix_pre, w_in, forget_bias, conv_mix_w, w_branch_conv, w_branch_attn, w_out,
           norm_mix_post, norm_ffn_pre, w_up, ffn_conv_w, ffn_conv_b, w_down, norm_ffn_post,
           w_ple_proj, norm_ple_gate, w_ple_gate, norm_ple_post):
    batch, seq, d = x.shape
    t = batch * seq
    depth = w_in.shape[0]
    conv_width = conv_mix_w.shape[-1]
    attn_width = N_HEADS * HEAD_DIM
    qkv_start = 3 * conv_width
    forget_start = qkv_start + 3 * attn_width
    gate_start = forget_start + N_HEADS

    xs = x.reshape(t, d)
    for i in range(depth):
        w_in_t = jnp.transpose(w_in[i])

        h1 = _norm(xs, norm_mix_pre[i])
        a = _conv_branch(h1, w_in_t, conv_mix_w[i], seq)
        qkv = _qkv(h1, w_in_t, qkv_start, 3 * attn_width)
        gates = _gates(h1, w_in_t, gate_start, 2 * d)
        c, augk = _forget_cumsum(h1, w_in_t, forget_start, forget_bias[i], seq)
        o = _attention(qkv, augk, c, batch, seq)
        merged = _merge(a, o, w_branch_conv[i].astype(BF16), w_branch_attn[i].astype(BF16), gates)
        y1 = _matmul(merged, w_out[i], ROW_TILE_MM, 512, "out_proj")
        x1, h2 = _residual_norm(xs, y1, norm_mix_post[i], norm_ffn_pre[i])

        ffn = _ffn_up(h2, w_up[i], ffn_conv_w[i], ffn_conv_b[i], seq)
        y2 = _matmul(ffn, w_down[i].astype(BF16), ROW_TILE_WIDE_K, 512, "ffn_down")
        x2, h3 = _residual_norm(x1, y2, norm_ffn_post[i], norm_ple_gate[i])

        ge = _ple(h3, p[i].reshape(t, -1), w_ple_gate[i], w_ple_proj[i])
        xs = _residual(x2, ge, norm_ple_post[i])
    return xs.reshape(batch, seq, d)
```

```python
import functools
import math

import jax
import jax.numpy as jnp
from jax import lax
from jax.experimental import pallas as pl
from jax.experimental.pallas import tpu as pltpu

BF16 = jnp.bfloat16
F32 = jnp.float32

LANES = 128
BF16_SUBLANES = 16
N_HEADS = 32
HEAD_DIM = 128
SHORT_K = 3
EPS = 1e-6
NEG_INF = -1e30
LOG2_E = math.log2(math.e)
GELU_C0 = math.sqrt(2.0 / math.pi)
GELU_C1 = GELU_C0 * 0.044715
MIB = 1024 * 1024

ROW_TILE_NORM = 256
ROW_TILE_MM = 1024
ROW_TILE_WIDE_K = 512
ATTN_TILE = 1024
CUMSUM_TILE = 512
CARRY_ROWS = 8


def _params(semantics, vmem_mib, flags=None):
    return pltpu.CompilerParams(dimension_semantics=semantics, vmem_limit_bytes=vmem_mib * MIB,
                                flags=flags)


def _dot(a, b):
    return jnp.dot(a, b, preferred_element_type=F32)


def _dot_nt(a, b):
    return lax.dot_general(a, b, (((1,), (1,)), ((), ())), preferred_element_type=F32)


def _rms_scale(x, g):
    inv = lax.rsqrt(jnp.mean(x * x, axis=-1, keepdims=True) + EPS)
    return x * inv * g


def _norm_kernel(x_ref, g_ref, h_ref):
    h_ref[...] = _rms_scale(x_ref[...], g_ref[...]).astype(BF16)


def _residual_norm_kernel(x_ref, y_ref, g_post_ref, g_pre_ref, x_out_ref, h_ref):
    x_new = x_ref[...] + _rms_scale(y_ref[...].astype(F32), g_post_ref[...])
    x_out_ref[...] = x_new
    h_ref[...] = _rms_scale(x_new, g_pre_ref[...]).astype(BF16)


def _residual_kernel(x_ref, y_ref, g_post_ref, x_out_ref):
    x_out_ref[...] = x_ref[...] + _rms_scale(y_ref[...].astype(F32), g_post_ref[...])


def _row_spec(tr, d):
    return pl.BlockSpec((tr, d), lambda i: (i, 0))


def _gain_spec(d):
    return pl.BlockSpec((1, d), lambda i: (0, 0))


def _norm(x, g):
    t, d = x.shape
    tr = ROW_TILE_NORM
    return pl.pallas_call(
        _norm_kernel,
        grid=(t // tr,),
        in_specs=[_row_spec(tr, d), _gain_spec(d)],
        out_specs=_row_spec(tr, d),
        out_shape=jax.ShapeDtypeStruct((t, d), BF16),
        compiler_params=_params(("parallel",), 40),
        name="norm",
    )(x, g.reshape(1, d))


def _residual_norm(x, y, g_post, g_pre):
    t, d = x.shape
    tr = ROW_TILE_NORM
    return pl.pallas_call(
        _residual_norm_kernel,
        grid=(t // tr,),
        in_specs=[_row_spec(tr, d), _row_spec(tr, d), _gain_spec(d), _gain_spec(d)],
        out_specs=[_row_spec(tr, d), _row_spec(tr, d)],
        out_shape=[jax.ShapeDtypeStruct((t, d), F32), jax.ShapeDtypeStruct((t, d), BF16)],
        compiler_params=_params(("parallel",), 40),
        name="residual_norm",
    )(x, y, g_post.reshape(1, d), g_pre.reshape(1, d))


def _residual(x, y, g_post):
    t, d = x.shape
    tr = ROW_TILE_NORM
    return pl.pallas_call(
        _residual_kernel,
        grid=(t // tr,),
        in_specs=[_row_spec(tr, d), _row_spec(tr, d), _gain_spec(d)],
        out_specs=_row_spec(tr, d),
        out_shape=jax.ShapeDtypeStruct((t, d), F32),
        compiler_params=_params(("parallel",), 40),
        name="residual",
    )(x, y, g_post.reshape(1, d))


def _causal_conv3(u, carry_ref, w_ref):
    tm = u.shape[0]
    w0, w1, w2 = w_ref[0:1, :], w_ref[1:2, :], w_ref[2:3, :]
    y = w0 * pltpu.roll(u, 2, 0) + w1 * pltpu.roll(u, 1, 0) + w2 * u
    head = u[0:CARRY_ROWS, :]
    prev = carry_ref[...]
    r = lax.broadcasted_iota(jnp.int32, head.shape, 0)
    back1 = jnp.where(r < 1, pltpu.roll(prev, 1, 0), pltpu.roll(head, 1, 0))
    back2 = jnp.where(r < 2, pltpu.roll(prev, 2, 0), pltpu.roll(head, 2, 0))
    y_head = w0 * back2 + w1 * back1 + w2 * head
    carry_ref[...] = u[tm - CARRY_ROWS:, :]
    return jnp.concatenate([y_head, y[CARRY_ROWS:, :]], axis=0)


def _reset_carry_at_sequence_start(carry_refs, tiles_per_seq):
    @pl.when(pl.program_id(1) % tiles_per_seq == 0)
    def _():
        for ref in carry_refs:
            ref[...] = jnp.zeros_like(ref)


def _cast_weights_at_sweep_start(pairs):
    @pl.when(pl.program_id(1) == 0)
    def _():
        for w_ref, w_bf16_ref in pairs:
            w_bf16_ref[...] = w_ref[...].astype(BF16)


def _conv_branch_kernel(h_ref, wb_ref, wc_ref, wv_ref, cw_ref, o_ref, carry_ref,
                        wb_bf, wc_bf, wv_bf, *, tiles_per_seq):
    _cast_weights_at_sweep_start([(wb_ref, wb_bf), (wc_ref, wc_bf), (wv_ref, wv_bf)])
    _reset_carry_at_sequence_start([carry_ref], tiles_per_seq)
    h = h_ref[...]
    u = _dot_nt(h, wc_bf[...]) * _dot_nt(h, wv_bf[...])
    y = _causal_conv3(u, carry_ref, cw_ref)
    o_ref[...] = (_dot_nt(h, wb_bf[...]) * y).astype(BF16)


def _conv_branch(h, w_in_t, conv_w, seq):
    t, d = h.shape
    width = conv_w.shape[1]
    tm, tn = ROW_TILE_WIDE_K, 256
    nb = width // tn
    return pl.pallas_call(
        functools.partial(_conv_branch_kernel, tiles_per_seq=seq // tm),
        grid=(nb, t // tm),
        in_specs=[
            pl.BlockSpec((tm, d), lambda j, i: (i, 0)),
            pl.BlockSpec((tn, d), lambda j, i: (j, 0)),
            pl.BlockSpec((tn, d), lambda j, i: (j + nb, 0)),
            pl.BlockSpec((tn, d), lambda j, i: (j + 2 * nb, 0)),
            pl.BlockSpec((SHORT_K, tn), lambda j, i: (0, j)),
        ],
        out_specs=pl.BlockSpec((tm, tn), lambda j, i: (i, j)),
        out_shape=jax.ShapeDtypeStruct((t, width), BF16),
        scratch_shapes=[pltpu.VMEM((CARRY_ROWS, tn), F32)] + [pltpu.VMEM((tn, d), BF16)] * 3,
        compiler_params=_params(("parallel", "arbitrary"), 48),
        name="conv_branch",
    )(h, w_in_t, w_in_t, w_in_t, conv_w)


def _qkv_kernel(h_ref, w_ref, o_ref, w_bf, *, q_tiles, scale):
    _cast_weights_at_sweep_start([(w_ref, w_bf)])
    acc = _dot_nt(h_ref[...], w_bf[...])
    factor = jnp.where(pl.program_id(0) < q_tiles, scale, 1.0).astype(F32)
    o_ref[...] = (acc * factor).astype(BF16)


def _qkv(h, w_in_t, row_start, n):
    t, d = h.shape
    tm, tn = ROW_TILE_MM, 512
    first = row_start // tn
    kern = functools.partial(_qkv_kernel, q_tiles=(n // 3) // tn, scale=LOG2_E / math.sqrt(HEAD_DIM))
    return pl.pallas_call(
        kern,
        grid=(n // tn, t // tm),
        in_specs=[pl.BlockSpec((tm, d), lambda j, i: (i, 0)),
                  pl.BlockSpec((tn, d), lambda j, i: (first + j, 0))],
        out_specs=pl.BlockSpec((tm, tn), lambda j, i: (i, j)),
        out_shape=jax.ShapeDtypeStruct((t, n), BF16),
        scratch_shapes=[pltpu.VMEM((tn, d), BF16)],
        compiler_params=_params(("parallel", "arbitrary"), 48),
        name="qkv",
    )(h, w_in_t)


def _gates_kernel(h_ref, w_ref, w_next_ref, o_ref, w_bf, *, shift):
    @pl.when(pl.program_id(1) == 0)
    def _():
        tn = w_bf.shape[0]
        w_bf[0:tn - shift, :] = w_ref[shift:tn, :].astype(BF16)
        w_bf[tn - shift:tn, :] = w_next_ref[0:shift, :].astype(BF16)

    o_ref[...] = jax.nn.sigmoid(_dot_nt(h_ref[...], w_bf[...])).astype(BF16)


def _gates(h, w_in_t, row_start, n):
    t, d = h.shape
    tm, tn = ROW_TILE_MM, 512
    shift = row_start % LANES
    aligned = row_start - shift
    assert 0 < shift and shift % BF16_SUBLANES == 0 and aligned % tn == 0
    return pl.pallas_call(
        functools.partial(_gates_kernel, shift=shift),
        grid=(n // tn, t // tm),
        in_specs=[pl.BlockSpec((tm, d), lambda j, i: (i, 0)),
                  pl.BlockSpec((tn, d), lambda j, i: (aligned // tn + j, 0)),
                  pl.BlockSpec((LANES, d), lambda j, i: ((aligned + (j + 1) * tn) // LANES, 0))],
        out_specs=pl.BlockSpec((tm, tn), lambda j, i: (i, j)),
        out_shape=jax.ShapeDtypeStruct((t, n), BF16),
        scratch_shapes=[pltpu.VMEM((tn, d), BF16)],
        compiler_params=_params(("parallel", "arbitrary"), 48),
        name="gates",
    )(h, w_in_t, w_in_t)


def _matmul_kernel(a_ref, w_ref, o_ref):
    o_ref[...] = _dot(a_ref[...], w_ref[...]).astype(BF16)


def _matmul_f32_weights_kernel(a_ref, w_ref, o_ref, w_bf):
    _cast_weights_at_sweep_start([(w_ref, w_bf)])
    o_ref[...] = _dot(a_ref[...], w_bf[...]).astype(BF16)


def _matmul(a, w, tm, tn, name):
    t, k = a.shape
    n = w.shape[1]
    cast_in_kernel = w.dtype == F32
    return pl.pallas_call(
        _matmul_f32_weights_kernel if cast_in_kernel else _matmul_kernel,
        grid=(n // tn, t // tm),
        in_specs=[pl.BlockSpec((tm, k), lambda j, i: (i, 0)),
                  pl.BlockSpec((k, tn), lambda j, i: (0, j))],
        out_specs=pl.BlockSpec((tm, tn), lambda j, i: (i, j)),
        out_shape=jax.ShapeDtypeStruct((t, n), BF16),
        scratch_shapes=[pltpu.VMEM((k, tn), BF16)] if cast_in_kernel else [],
        compiler_params=_params(("parallel", "arbitrary"), 52),
        name=name,
    )(a, w)


def _merge_kernel(a_ref, o_ref, wa_ref, wb_ref, ga_ref, gb_ref, out_ref):
    y_a = _dot(a_ref[...], wa_ref[...])
    y_b = _dot(o_ref[...], wb_ref[...])
    out_ref[...] = (ga_ref[...] * y_a + gb_ref[...] * y_b).astype(BF16)


def _merge(a, o, w_a, w_b, gates):
    t, d = a.shape
    n = w_a.shape[1]
    tm, tn = ROW_TILE_WIDE_K, 512
    nb = n // tn
    return pl.pallas_call(
        _merge_kernel,
        grid=(nb, t // tm),
        in_specs=[
            pl.BlockSpec((tm, d), lambda j, i: (i, 0)),
            pl.BlockSpec((tm, d), lambda j, i: (i, 0)),
            pl.BlockSpec((d, tn), lambda j, i: (0, j)),
            pl.BlockSpec((d, tn), lambda j, i: (0, j)),
            pl.BlockSpec((tm, tn), lambda j, i: (i, j)),
            pl.BlockSpec((tm, tn), lambda j, i: (i, j + nb)),
        ],
        out_specs=pl.BlockSpec((tm, tn), lambda j, i: (i, j)),
        out_shape=jax.ShapeDtypeStruct((t, n), BF16),
        compiler_params=_params(("parallel", "arbitrary"), 48),
        name="merge",
    )(a, o, w_a, w_b, gates, gates)


def _ffn_up_kernel(h_ref, wg_ref, wv_ref, cwg_ref, cwv_ref, bg_ref, bv_ref, o_ref,
                   carry_g_ref, carry_v_ref, wg_bf, wv_bf, *, tiles_per_seq):
    _cast_weights_at_sweep_start([(wg_ref, wg_bf), (wv_ref, wv_bf)])
    _reset_carry_at_sequence_start([carry_g_ref, carry_v_ref], tiles_per_seq)
    h = h_ref[...]
    u_gate = _causal_conv3(_dot(h, wg_bf[...]), carry_g_ref, cwg_ref) + bg_ref[...]
    u_val = _causal_conv3(_dot(h, wv_bf[...]), carry_v_ref, cwv_ref) + bv_ref[...]
    half_gate = 0.5 * u_gate
    t = jnp.tanh(u_gate * (GELU_C0 + GELU_C1 * (u_gate * u_gate)))
    o_ref[...] = ((half_gate + half_gate * t) * u_val).astype(BF16)


def _ffn_up(h, w_up, conv_w, conv_b, seq):
    t, d = h.shape
    d_ff = w_up.shape[1] // 2
    tm, tn = ROW_TILE_MM, 256
    nb = d_ff // tn
    conv_b = conv_b.reshape(1, 2 * d_ff)
    return pl.pallas_call(
        functools.partial(_ffn_up_kernel, tiles_per_seq=seq // tm),
        grid=(nb, t // tm),
        in_specs=[
            pl.BlockSpec((tm, d), lambda j, i: (i, 0)),
            pl.BlockSpec((d, tn), lambda j, i: (0, j)),
            pl.BlockSpec((d, tn), lambda j, i: (0, j + nb)),
            pl.BlockSpec((SHORT_K, tn), lambda j, i: (0, j)),
            pl.BlockSpec((SHORT_K, tn), lambda j, i: (0, j + nb)),
            pl.BlockSpec((1, tn), lambda j, i: (0, j)),
            pl.BlockSpec((1, tn), lambda j, i: (0, j + nb)),
        ],
        out_specs=pl.BlockSpec((tm, tn), lambda j, i: (i, j)),
        out_shape=jax.ShapeDtypeStruct((t, d_ff), BF16),
        scratch_shapes=[pltpu.VMEM((CARRY_ROWS, tn), F32)] * 2 + [pltpu.VMEM((d, tn), BF16)] * 2,
        compiler_params=_params(("parallel", "arbitrary"), 48),
        name="ffn_up",
    )(h, w_up, w_up, conv_w, conv_w, conv_b, conv_b)


def _ple_kernel(h_ref, p_ref, wg_ref, wp_ref, o_ref, wg_bf, wp_bf):
    _cast_weights_at_sweep_start([(wg_ref, wg_bf), (wp_ref, wp_bf)])
    gate = jax.nn.sigmoid(_dot(h_ref[...], wg_bf[...]))
    o_ref[...] = (gate * _dot(p_ref[...].astype(BF16), wp_bf[...])).astype(BF16)


def _ple(h, p, w_gate, w_proj):
    t, d = h.shape
    n = w_gate.shape[1]
    ple = p.shape[1]
    tm, tn = ROW_TILE_MM, 512
    return pl.pallas_call(
        _ple_kernel,
        grid=(n // tn, t // tm),
        in_specs=[
            pl.BlockSpec((tm, d), lambda j, i: (i, 0)),
            pl.BlockSpec((tm, ple), lambda j, i: (i, 0)),
            pl.BlockSpec((d, tn), lambda j, i: (0, j)),
            pl.BlockSpec((ple, tn), lambda j, i: (0, j)),
        ],
        out_specs=pl.BlockSpec((tm, tn), lambda j, i: (i, j)),
        out_shape=jax.ShapeDtypeStruct((t, n), BF16),
        scratch_shapes=[pltpu.VMEM((d, tn), BF16), pltpu.VMEM((ple, tn), BF16)],
        compiler_params=_params(("parallel", "arbitrary"), 48),
        name="ple",
    )(h, p, w_gate, w_proj)


def _split_bf16x3(x):
    hi = x.astype(BF16)
    rest = x - hi.astype(F32)
    mid = rest.astype(BF16)
    lo = (rest - mid.astype(F32)).astype(BF16)
    return hi, mid, lo


def _forget_cumsum_kernel(h_ref, wf_ref, b_ref, spread_ref, ones_ref, c_ref, augk_ref, carry_ref,
                          wf_bf, *, tiles_per_seq):
    @pl.when(pl.program_id(0) == 0)
    def _():
        wf_bf[...] = wf_ref[...].astype(BF16)

    @pl.when(pl.program_id(0) % tiles_per_seq == 0)
    def _():
        carry_ref[...] = jnp.zeros_like(carry_ref)

    ts = h_ref.shape[0]
    nh = b_ref.shape[1]
    log_f = jax.nn.log_sigmoid(_dot_nt(h_ref[...], wf_bf[...])[:, 0:nh] + b_ref[...])
    row = lax.broadcasted_iota(jnp.int32, (ts, ts), 0)
    col = lax.broadcasted_iota(jnp.int32, (ts, ts), 1)
    lower = (col <= row).astype(F32)
    csum = jnp.dot(lower, log_f, preferred_element_type=F32,
                   precision=lax.Precision.HIGHEST) + carry_ref[...]
    c_ref[...] = csum
    carry_ref[...] = c_ref[ts - 1:ts, :]

    hi, mid, lo = _split_bf16x3(csum * LOG2_E)
    aug = (_dot(hi, spread_ref[0]) + _dot(mid, spread_ref[1]) + _dot(lo, spread_ref[2])
           + ones_ref[...])
    augk_ref[...] = aug.astype(BF16)


def _forget_cumsum(h, w_in_t, row_start, bias, seq):
    t, d = h.shape
    nh = bias.shape[0]
    ts = CUMSUM_TILE
    assert row_start % LANES == 0 and nh <= LANES
    width = nh * HEAD_DIM
    lane = jnp.arange(width)[None, :]
    head = jnp.arange(nh)[:, None]
    spread = jnp.stack([jnp.where(lane == head * HEAD_DIM + j, -1.0, 0.0) for j in range(3)])
    ones = jnp.where((lane % HEAD_DIM >= 3) & (lane % HEAD_DIM < 6), 1.0, 0.0).astype(F32)
    return pl.pallas_call(
        functools.partial(_forget_cumsum_kernel, tiles_per_seq=seq // ts),
        grid=(t // ts,),
        in_specs=[pl.BlockSpec((ts, d), lambda i: (i, 0)),
                  pl.BlockSpec((LANES, d), lambda i: (row_start // LANES, 0)),
                  pl.BlockSpec((1, nh), lambda i: (0, 0)),
                  pl.BlockSpec((3, nh, width), lambda i: (0, 0, 0)),
                  pl.BlockSpec((1, width), lambda i: (0, 0))],
        out_specs=[pl.BlockSpec((ts, nh), lambda i: (i, 0)),
                   pl.BlockSpec((ts, width), lambda i: (i, 0))],
        out_shape=[jax.ShapeDtypeStruct((t, nh), F32), jax.ShapeDtypeStruct((t, width), BF16)],
        scratch_shapes=[pltpu.VMEM((1, nh), F32), pltpu.VMEM((LANES, d), BF16)],
        compiler_params=_params(("arbitrary",), 48),
        name="forget_cumsum",
    )(h, w_in_t, bias.reshape(1, nh), spread.astype(BF16), ones)


AUG_ROWS = 16
SUM_ROWS = 16
HEADS_PER_STEP = 2


def _attention_kernel(q_ref, k_ref, v_ref, augk_ref, c_ref, o_ref,
                      kx_ref, vt_ref, qxt_ref, m_ref, *buffers, tile):
    n = HEADS_PER_STEP
    s_ref = [buffers[2 * g:2 * g + 2] for g in range(n)]
    p_ref = [buffers[2 * n + 2 * g:2 * n + 2 * g + 2] for g in range(n)]
    acc_ref = buffers[4 * n:]
    kt = tile // 2
    qi = pl.program_id(2)
    n_key_tiles = kx_ref.shape[1]
    heads = range(HEADS_PER_STEP)

    def lanes(g):
        return slice(g * HEAD_DIM, (g + 1) * HEAD_DIM)

    @pl.when(qi == 0)
    def _():
        def stage(j, _):
            rows = pl.ds(pl.multiple_of(j * kt, kt), kt)
            for g in heads:
                kx_ref[g, j, :, 0:HEAD_DIM] = k_ref[rows, lanes(g)]
                kx_ref[g, j, :, HEAD_DIM:] = augk_ref[rows, lanes(g)]
                vt_ref[g, j, 0:HEAD_DIM, :] = v_ref[rows, lanes(g)].astype(F32).T.astype(BF16)
                vt_ref[g, j, HEAD_DIM:, :] = jnp.ones((SUM_ROWS, kt), BF16)
            return 0
        lax.fori_loop(0, n_key_tiles, stage, 0)
        for g in heads:
            qxt_ref[g, HEAD_DIM + AUG_ROWS:, :] = jnp.zeros((HEAD_DIM - AUG_ROWS, tile), BF16)

    r = lax.broadcasted_iota(jnp.int32, (AUG_ROWS, tile), 0)
    for g in heads:
        qxt_ref[g, 0:HEAD_DIM, :] = q_ref[:, lanes(g)].astype(F32).T.astype(BF16)
        hi, mid, lo = _split_bf16x3(c_ref[g, qi] * LOG2_E)
        aug = jnp.where(r < 3, 1.0,
                        jnp.where(r == 3, hi.astype(F32),
                                  jnp.where(r == 4, mid.astype(F32),
                                            jnp.where(r == 5, lo.astype(F32), 0.0))))
        qxt_ref[g, HEAD_DIM:HEAD_DIM + AUG_ROWS, :] = aug.astype(BF16)

    def scores_into(g, slot, ki):
        s = _dot(kx_ref[g, ki], qxt_ref[g])
        s_ref[g][slot][...] = s
        return jnp.max(s, axis=0, keepdims=True)

    def softmax_into(g, slot, m, tile_max):
        m_new = jnp.maximum(m, tile_max)
        p_ref[g][slot][...] = jnp.exp2(s_ref[g][slot][...] - m_new).astype(BF16)
        return m_new, jnp.exp2(m - m_new)

    def accumulate(g, slot, alpha, ki):
        acc_ref[g][...] = alpha * acc_ref[g][...] + _dot(vt_ref[g, ki], p_ref[g][slot][...])

    for g in heads:
        p_ref[g][1][...] = jnp.zeros((kt, tile), BF16)
        acc_ref[g][...] = jnp.zeros(acc_ref[g].shape, F32)
    max_0 = tuple(scores_into(g, 0, 0) for g in heads)

    def pair(jj, carry):
        m, alpha_1, max_0 = carry
        k0 = 2 * jj
        max_1 = [scores_into(g, 1, k0 + 1) for g in heads]
        m, alpha_0 = zip(*[softmax_into(g, 0, m[g], max_0[g]) for g in heads])
        for g in heads:
            accumulate(g, 1, alpha_1[g], jnp.maximum(k0 - 1, 0))
        max_0 = tuple(scores_into(g, 0, k0 + 2) for g in heads)
        m, alpha_1 = zip(*[softmax_into(g, 1, m[g], max_1[g]) for g in heads])
        for g in heads:
            accumulate(g, 0, alpha_0[g], k0)
        return m, alpha_1, max_0

    init = (tuple(jnp.full((1, tile), NEG_INF, F32) for _ in heads),
            tuple(jnp.ones((1, tile), F32) for _ in heads), max_0)
    m, alpha_1, _ = lax.fori_loop(0, qi, pair, init)

    k0 = 2 * qi
    key = lax.broadcasted_iota(jnp.int32, (kt, tile), 0)
    query = lax.broadcasted_iota(jnp.int32, (kt, tile), 1)
    upper = slice(kt, tile)

    def upper_scores_into(g):
        s = _dot(kx_ref[g, k0 + 1], qxt_ref[g, :, upper])
        s = jnp.where(lax.broadcasted_iota(jnp.int32, (kt, kt), 0)
                      <= lax.broadcasted_iota(jnp.int32, (kt, kt), 1), s, NEG_INF)
        s_ref[g][1][:, 0:kt] = s
        return jnp.max(s, axis=0, keepdims=True)

    max_1 = [upper_scores_into(g) for g in heads]
    for g in heads:
        s_ref[g][0][...] = jnp.where(key <= query, s_ref[g][0][...], NEG_INF)
    max_0 = [jnp.max(s_ref[g][0][...], axis=0, keepdims=True) for g in heads]
    m, alpha_0 = zip(*[softmax_into(g, 0, m[g], max_0[g]) for g in heads])
    for g in heads:
        accumulate(g, 1, alpha_1[g], jnp.maximum(k0 - 1, 0))
    alpha_upper = []
    for g in heads:
        m_ref[g, 0:1, :] = m[g]
        m_old = m_ref[g, 0:1, upper]
        m_upper = jnp.maximum(m_old, max_1[g])
        alpha_upper.append(jnp.exp2(m_old - m_upper))
        p_ref[g][1][:, 0:kt] = jnp.exp2(s_ref[g][1][:, 0:kt] - m_upper).astype(BF16)
    for g in heads:
        accumulate(g, 0, alpha_0[g], k0)
    for g in heads:
        acc_ref[g][:, upper] = (alpha_upper[g] * acc_ref[g][:, upper]
                                + _dot(vt_ref[g, k0 + 1], p_ref[g][1][:, 0:kt]))
    for g in heads:
        o_t = acc_ref[g][0:HEAD_DIM, :] / acc_ref[g][HEAD_DIM:HEAD_DIM + 1, :]
        o_ref[:, lanes(g)] = o_t.T.astype(BF16)


def _attention(qkv, augk, c, batch, seq):
    t = qkv.shape[0]
    tile = ATTN_TILE
    kt = tile // 2
    nq = seq // tile
    g = HEADS_PER_STEP
    width = g * HEAD_DIM
    groups = N_HEADS // g
    c_rows = jnp.transpose(c.reshape(batch, seq, N_HEADS), (0, 2, 1)).reshape(
        batch * N_HEADS, nq, 1, tile)
    return pl.pallas_call(
        functools.partial(_attention_kernel, tile=tile),
        grid=(batch, groups, nq),
        in_specs=[
            pl.BlockSpec((tile, width), lambda b, h, i: (b * nq + i, h)),
            pl.BlockSpec((seq, width), lambda b, h, i: (b, groups + h), pipeline_mode=pl.Buffered(1)),
            pl.BlockSpec((seq, width), lambda b, h, i: (b, 2 * groups + h), pipeline_mode=pl.Buffered(1)),
            pl.BlockSpec((seq, width), lambda b, h, i: (b, h), pipeline_mode=pl.Buffered(1)),
            pl.BlockSpec((g, nq, 1, tile), lambda b, h, i: (b * groups + h, 0, 0, 0)),
        ],
        out_specs=pl.BlockSpec((tile, width), lambda b, h, i: (b * nq + i, h)),
        out_shape=jax.ShapeDtypeStruct((t, N_HEADS * HEAD_DIM), BF16),
        scratch_shapes=[pltpu.VMEM((g, seq // kt, kt, 2 * HEAD_DIM), BF16),
                        pltpu.VMEM((g, seq // kt, HEAD_DIM + SUM_ROWS, kt), BF16),
                        pltpu.VMEM((g, 2 * HEAD_DIM, tile), BF16),
                        pltpu.VMEM((g, 8, tile), F32),
                        *[pltpu.VMEM((kt, tile), F32)] * (2 * g),
                        *[pltpu.VMEM((kt, tile), BF16)] * (2 * g),
                        *[pltpu.VMEM((HEAD_DIM + SUM_ROWS, tile), F32)] * g],
        compiler_params=_params(("parallel", "parallel", "arbitrary"), 56),
        name="fox_attention",
    )(qkv, qkv, qkv, augk, c_rows)


def kernel(x, p, norm_mix_pre, w_in, forget_bias, conv_mix_w, w_branch_conv, w_branch_attn, w_out,
           norm_mix_post, norm_ffn_pre, w_up, ffn_conv_w, ffn_conv_b, w_down, norm_ffn_post,
           w_ple_proj, norm_ple_gate, w_ple_gate, norm_ple_post):
    batch, seq, d = x.shape
    t = batch * seq
    depth = w_in.shape[0]
    conv_width = conv_mix_w.shape[-1]
    attn_width = N_HEADS * HEAD_DIM
    qkv_start = 3 * conv_width
    forget_start = qkv_start + 3 * attn_width
    gate_start = forget_start + N_HEADS

    xs = x.reshape(t, d)
    for i in range(depth):
        w_in_t = jnp.transpose(w_in[i])

        h1 = _norm(xs, norm_mix_pre[i])
        a = _conv_branch(h1, w_in_t, conv_mix_w[i], seq)
        qkv = _qkv(h1, w_in_t, qkv_start, 3 * attn_width)
        gates = _gates(h1, w_in_t, gate_start, 2 * d)
        c, augk = _forget_cumsum(h1, w_in_t, forget_start, forget_bias[i], seq)
        o = _attention(qkv, augk, c, batch, seq)
        merged = _merge(a, o, w_branch_conv[i].astype(BF16), w_branch_attn[i].astype(BF16), gates)
        y1 = _matmul(merged, w_out[i], ROW_TILE_MM, 512, "out_proj")
        x1, h2 = _residual_norm(xs, y1, norm_mix_post[i], norm_ffn_pre[i])

        ffn = _ffn_up(h2, w_up[i], ffn_conv_w[i], ffn_conv_b[i], seq)
        y2 = _matmul(ffn, w_down[i].astype(BF16), ROW_TILE_WIDE_K, 512, "ffn_down")
        x2, h3 = _residual_norm(x1, y2, norm_ffn_post[i], norm_ple_gate[i])

        ge = _ple(h3, p[i].reshape(t, -1), w_ple_gate[i], w_ple_proj[i])
        xs = _residual(x2, ge, norm_ple_post[i])
    return xs.reshape(batch, seq, d)
```

```python
import functools
import math

import jax
import jax.numpy as jnp
from jax import lax
from jax.experimental import pallas as pl
from jax.experimental.pallas import tpu as pltpu

BF16 = jnp.bfloat16
F32 = jnp.float32

LANES = 128
BF16_SUBLANES = 16
N_HEADS = 32
HEAD_DIM = 128
SHORT_K = 3
EPS = 1e-6
NEG_INF = -1e30
LOG2_E = math.log2(math.e)
GELU_C0 = math.sqrt(2.0 / math.pi)
GELU_C1 = GELU_C0 * 0.044715
MIB = 1024 * 1024

ROW_TILE_NORM = 256
ROW_TILE_MM = 1024
ROW_TILE_WIDE_K = 512
ATTN_TILE = 1024
CUMSUM_TILE = 512
CARRY_ROWS = 8


def _params(semantics, vmem_mib, flags=None):
    return pltpu.CompilerParams(dimension_semantics=semantics, vmem_limit_bytes=vmem_mib * MIB,
                                flags=flags)


def _dot(a, b):
    return jnp.dot(a, b, preferred_element_type=F32)


def _dot_nt(a, b):
    return lax.dot_general(a, b, (((1,), (1,)), ((), ())), preferred_element_type=F32)


def _sigmoid(x):
    return 0.5 * jnp.tanh(0.5 * x) + 0.5


def _rms_scale(x, g):
    inv = lax.rsqrt(jnp.mean(x * x, axis=-1, keepdims=True) + EPS)
    return x * inv * g


def _norm_kernel(x_ref, g_ref, h_ref):
    h_ref[...] = _rms_scale(x_ref[...], g_ref[...]).astype(BF16)


def _residual_norm_kernel(x_ref, y_ref, g_post_ref, g_pre_ref, x_out_ref, h_ref):
    x_new = x_ref[...] + _rms_scale(y_ref[...].astype(F32), g_post_ref[...])
    x_out_ref[...] = x_new
    h_ref[...] = _rms_scale(x_new, g_pre_ref[...]).astype(BF16)


def _residual_kernel(x_ref, y_ref, g_post_ref, x_out_ref):
    x_out_ref[...] = x_ref[...] + _rms_scale(y_ref[...].astype(F32), g_post_ref[...])


def _row_spec(tr, d):
    return pl.BlockSpec((tr, d), lambda i: (i, 0))


def _gain_spec(d):
    return pl.BlockSpec((1, d), lambda i: (0, 0))


def _norm(x, g):
    t, d = x.shape
    tr = ROW_TILE_NORM
    return pl.pallas_call(
        _norm_kernel,
        grid=(t // tr,),
        in_specs=[_row_spec(tr, d), _gain_spec(d)],
        out_specs=_row_spec(tr, d),
        out_shape=jax.ShapeDtypeStruct((t, d), BF16),
        compiler_params=_params(("parallel",), 40),
        name="norm",
    )(x, g.reshape(1, d))


def _residual_norm(x, y, g_post, g_pre):
    t, d = x.shape
    tr = ROW_TILE_NORM
    return pl.pallas_call(
        _residual_norm_kernel,
        grid=(t // tr,),
        in_specs=[_row_spec(tr, d), _row_spec(tr, d), _gain_spec(d), _gain_spec(d)],
        out_specs=[_row_spec(tr, d), _row_spec(tr, d)],
        out_shape=[jax.ShapeDtypeStruct((t, d), F32), jax.ShapeDtypeStruct((t, d), BF16)],
        compiler_params=_params(("parallel",), 40),
        name="residual_norm",
    )(x, y, g_post.reshape(1, d), g_pre.reshape(1, d))


def _residual(x, y, g_post):
    t, d = x.shape
    tr = ROW_TILE_NORM
    return pl.pallas_call(
        _residual_kernel,
        grid=(t // tr,),
        in_specs=[_row_spec(tr, d), _row_spec(tr, d), _gain_spec(d)],
        out_specs=_row_spec(tr, d),
        out_shape=jax.ShapeDtypeStruct((t, d), F32),
        compiler_params=_params(("parallel",), 40),
        name="residual",
    )(x, y, g_post.reshape(1, d))


def _causal_conv3(u, carry_ref, w_ref):
    tm = u.shape[0]
    w0, w1, w2 = w_ref[0:1, :], w_ref[1:2, :], w_ref[2:3, :]
    y = w0 * pltpu.roll(u, 2, 0) + w1 * pltpu.roll(u, 1, 0) + w2 * u
    head = u[0:CARRY_ROWS, :]
    prev = carry_ref[...]
    r = lax.broadcasted_iota(jnp.int32, head.shape, 0)
    back1 = jnp.where(r < 1, pltpu.roll(prev, 1, 0), pltpu.roll(head, 1, 0))
    back2 = jnp.where(r < 2, pltpu.roll(prev, 2, 0), pltpu.roll(head, 2, 0))
    y_head = w0 * back2 + w1 * back1 + w2 * head
    carry_ref[...] = u[tm - CARRY_ROWS:, :]
    return jnp.concatenate([y_head, y[CARRY_ROWS:, :]], axis=0)


def _reset_carry_at_sequence_start(carry_refs, tiles_per_seq):
    @pl.when(pl.program_id(1) % tiles_per_seq == 0)
    def _():
        for ref in carry_refs:
            ref[...] = jnp.zeros_like(ref)


def _cast_weights_at_sweep_start(pairs):
    @pl.when(pl.program_id(1) == 0)
    def _():
        for w_ref, w_bf16_ref in pairs:
            w_bf16_ref[...] = w_ref[...].astype(BF16)


def _conv_branch_kernel(h_ref, wb_ref, wc_ref, wv_ref, cw_ref, o_ref, carry_ref,
                        wb_bf, wc_bf, wv_bf, *, tiles_per_seq):
    _cast_weights_at_sweep_start([(wb_ref, wb_bf), (wc_ref, wc_bf), (wv_ref, wv_bf)])
    _reset_carry_at_sequence_start([carry_ref], tiles_per_seq)
    h = h_ref[...]
    u = _dot_nt(h, wc_bf[...]) * _dot_nt(h, wv_bf[...])
    y = _causal_conv3(u, carry_ref, cw_ref)
    o_ref[...] = (_dot_nt(h, wb_bf[...]) * y).astype(BF16)


def _conv_branch(h, w_in_t, conv_w, seq):
    t, d = h.shape
    width = conv_w.shape[1]
    tm, tn = ROW_TILE_WIDE_K, 256
    nb = width // tn
    return pl.pallas_call(
        functools.partial(_conv_branch_kernel, tiles_per_seq=seq // tm),
        grid=(nb, t // tm),
        in_specs=[
            pl.BlockSpec((tm, d), lambda j, i: (i, 0)),
            pl.BlockSpec((tn, d), lambda j, i: (j, 0)),
            pl.BlockSpec((tn, d), lambda j, i: (j + nb, 0)),
            pl.BlockSpec((tn, d), lambda j, i: (j + 2 * nb, 0)),
            pl.BlockSpec((SHORT_K, tn), lambda j, i: (0, j)),
        ],
        out_specs=pl.BlockSpec((tm, tn), lambda j, i: (i, j)),
        out_shape=jax.ShapeDtypeStruct((t, width), BF16),
        scratch_shapes=[pltpu.VMEM((CARRY_ROWS, tn), F32)] + [pltpu.VMEM((tn, d), BF16)] * 3,
        compiler_params=_params(("parallel", "arbitrary"), 48),
        name="conv_branch",
    )(h, w_in_t, w_in_t, w_in_t, conv_w)


def _qkv_kernel(h_ref, w_ref, o_ref, w_bf, *, q_tiles, scale):
    _cast_weights_at_sweep_start([(w_ref, w_bf)])
    acc = _dot_nt(h_ref[...], w_bf[...])
    factor = jnp.where(pl.program_id(0) < q_tiles, scale, 1.0).astype(F32)
    o_ref[...] = (acc * factor).astype(BF16)


def _qkv(h, w_in_t, row_start, n):
    t, d = h.shape
    tm, tn = ROW_TILE_MM, 512
    first = row_start // tn
    kern = functools.partial(_qkv_kernel, q_tiles=(n // 3) // tn, scale=LOG2_E / math.sqrt(HEAD_DIM))
    return pl.pallas_call(
        kern,
        grid=(n // tn, t // tm),
        in_specs=[pl.BlockSpec((tm, d), lambda j, i: (i, 0)),
                  pl.BlockSpec((tn, d), lambda j, i: (first + j, 0))],
        out_specs=pl.BlockSpec((tm, tn), lambda j, i: (i, j)),
        out_shape=jax.ShapeDtypeStruct((t, n), BF16),
        scratch_shapes=[pltpu.VMEM((tn, d), BF16)],
        compiler_params=_params(("parallel", "arbitrary"), 48),
        name="qkv",
    )(h, w_in_t)


def _gates_kernel(h_ref, w_ref, w_next_ref, o_ref, w_bf, *, shift):
    @pl.when(pl.program_id(1) == 0)
    def _():
        tn = w_bf.shape[0]
        w_bf[0:tn - shift, :] = w_ref[shift:tn, :].astype(BF16)
        w_bf[tn - shift:tn, :] = w_next_ref[0:shift, :].astype(BF16)

    o_ref[...] = _sigmoid(_dot_nt(h_ref[...], w_bf[...])).astype(BF16)


def _gates(h, w_in_t, row_start, n):
    t, d = h.shape
    tm, tn = ROW_TILE_MM, 512
    shift = row_start % LANES
    aligned = row_start - shift
    assert 0 < shift and shift % BF16_SUBLANES == 0 and aligned % tn == 0
    return pl.pallas_call(
        functools.partial(_gates_kernel, shift=shift),
        grid=(n // tn, t // tm),
        in_specs=[pl.BlockSpec((tm, d), lambda j, i: (i, 0)),
                  pl.BlockSpec((tn, d), lambda j, i: (aligned // tn + j, 0)),
                  pl.BlockSpec((LANES, d), lambda j, i: ((aligned + (j + 1) * tn) // LANES, 0))],
        out_specs=pl.BlockSpec((tm, tn), lambda j, i: (i, j)),
        out_shape=jax.ShapeDtypeStruct((t, n), BF16),
        scratch_shapes=[pltpu.VMEM((tn, d), BF16)],
        compiler_params=_params(("parallel", "arbitrary"), 48),
        name="gates",
    )(h, w_in_t, w_in_t)


def _matmul_kernel(a_ref, w_ref, o_ref):
    o_ref[...] = _dot(a_ref[...], w_ref[...]).astype(BF16)


def _matmul_f32_weights_kernel(a_ref, w_ref, o_ref, w_bf):
    _cast_weights_at_sweep_start([(w_ref, w_bf)])
    o_ref[...] = _dot(a_ref[...], w_bf[...]).astype(BF16)


def _matmul(a, w, tm, tn, name):
    t, k = a.shape
    n = w.shape[1]
    cast_in_kernel = w.dtype == F32
    return pl.pallas_call(
        _matmul_f32_weights_kernel if cast_in_kernel else _matmul_kernel,
        grid=(n // tn, t // tm),
        in_specs=[pl.BlockSpec((tm, k), lambda j, i: (i, 0)),
                  pl.BlockSpec((k, tn), lambda j, i: (0, j))],
        out_specs=pl.BlockSpec((tm, tn), lambda j, i: (i, j)),
        out_shape=jax.ShapeDtypeStruct((t, n), BF16),
        scratch_shapes=[pltpu.VMEM((k, tn), BF16)] if cast_in_kernel else [],
        compiler_params=_params(("parallel", "arbitrary"), 52),
        name=name,
    )(a, w)


def _merge_kernel(a_ref, o_ref, wa_ref, wb_ref, ga_ref, gb_ref, out_ref):
    y_a = _dot(a_ref[...], wa_ref[...])
    y_b = _dot(o_ref[...], wb_ref[...])
    out_ref[...] = (ga_ref[...] * y_a + gb_ref[...] * y_b).astype(BF16)


def _merge(a, o, w_a, w_b, gates):
    t, d = a.shape
    n = w_a.shape[1]
    tm, tn = ROW_TILE_WIDE_K, 512
    nb = n // tn
    return pl.pallas_call(
        _merge_kernel,
        grid=(nb, t // tm),
        in_specs=[
            pl.BlockSpec((tm, d), lambda j, i: (i, 0)),
            pl.BlockSpec((tm, d), lambda j, i: (i, 0)),
            pl.BlockSpec((d, tn), lambda j, i: (0, j)),
            pl.BlockSpec((d, tn), lambda j, i: (0, j)),
            pl.BlockSpec((tm, tn), lambda j, i: (i, j)),
            pl.BlockSpec((tm, tn), lambda j, i: (i, j + nb)),
        ],
        out_specs=pl.BlockSpec((tm, tn), lambda j, i: (i, j)),
        out_shape=jax.ShapeDtypeStruct((t, n), BF16),
        compiler_params=_params(("parallel", "arbitrary"), 48),
        name="merge",
    )(a, o, w_a, w_b, gates, gates)


def _ffn_up_kernel(h_ref, wg_ref, wv_ref, cwg_ref, cwv_ref, bg_ref, bv_ref, o_ref,
                   carry_g_ref, carry_v_ref, wg_bf, wv_bf, *, tiles_per_seq):
    _cast_weights_at_sweep_start([(wg_ref, wg_bf), (wv_ref, wv_bf)])
    _reset_carry_at_sequence_start([carry_g_ref, carry_v_ref], tiles_per_seq)
    h = h_ref[...]
    u_gate = _causal_conv3(_dot(h, wg_bf[...]), carry_g_ref, cwg_ref) + bg_ref[...]
    u_val = _causal_conv3(_dot(h, wv_bf[...]), carry_v_ref, cwv_ref) + bv_ref[...]
    half_gate = 0.5 * u_gate
    t = jnp.tanh(u_gate * (GELU_C0 + GELU_C1 * (u_gate * u_gate)))
    o_ref[...] = ((half_gate + half_gate * t) * u_val).astype(BF16)


def _ffn_up(h, w_up, conv_w, conv_b, seq):
    t, d = h.shape
    d_ff = w_up.shape[1] // 2
    tm, tn = ROW_TILE_MM, 256
    nb = d_ff // tn
    conv_b = conv_b.reshape(1, 2 * d_ff)
    return pl.pallas_call(
        functools.partial(_ffn_up_kernel, tiles_per_seq=seq // tm),
        grid=(nb, t // tm),
        in_specs=[
            pl.BlockSpec((tm, d), lambda j, i: (i, 0)),
            pl.BlockSpec((d, tn), lambda j, i: (0, j)),
            pl.BlockSpec((d, tn), lambda j, i: (0, j + nb)),
            pl.BlockSpec((SHORT_K, tn), lambda j, i: (0, j)),
            pl.BlockSpec((SHORT_K, tn), lambda j, i: (0, j + nb)),
            pl.BlockSpec((1, tn), lambda j, i: (0, j)),
            pl.BlockSpec((1, tn), lambda j, i: (0, j + nb)),
        ],
        out_specs=pl.BlockSpec((tm, tn), lambda j, i: (i, j)),
        out_shape=jax.ShapeDtypeStruct((t, d_ff), BF16),
        scratch_shapes=[pltpu.VMEM((CARRY_ROWS, tn), F32)] * 2 + [pltpu.VMEM((d, tn), BF16)] * 2,
        compiler_params=_params(("parallel", "arbitrary"), 48),
        name="ffn_up",
    )(h, w_up, w_up, conv_w, conv_w, conv_b, conv_b)


def _ple_kernel(h_ref, p_ref, wg_ref, wp_ref, o_ref, wg_bf, wp_bf):
    _cast_weights_at_sweep_start([(wg_ref, wg_bf), (wp_ref, wp_bf)])
    gate = _sigmoid(_dot(h_ref[...], wg_bf[...]))
    o_ref[...] = (gate * _dot(p_ref[...].astype(BF16), wp_bf[...])).astype(BF16)


def _ple(h, p, w_gate, w_proj):
    t, d = h.shape
    n = w_gate.shape[1]
    ple = p.shape[1]
    tm, tn = ROW_TILE_MM, 512
    return pl.pallas_call(
        _ple_kernel,
        grid=(n // tn, t // tm),
        in_specs=[
            pl.BlockSpec((tm, d), lambda j, i: (i, 0)),
            pl.BlockSpec((tm, ple), lambda j, i: (i, 0)),
            pl.BlockSpec((d, tn), lambda j, i: (0, j)),
            pl.BlockSpec((ple, tn), lambda j, i: (0, j)),
        ],
        out_specs=pl.BlockSpec((tm, tn), lambda j, i: (i, j)),
        out_shape=jax.ShapeDtypeStruct((t, n), BF16),
        scratch_shapes=[pltpu.VMEM((d, tn), BF16), pltpu.VMEM((ple, tn), BF16)],
        compiler_params=_params(("parallel", "arbitrary"), 48),
        name="ple",
    )(h, p, w_gate, w_proj)


def _split_bf16x3(x):
    hi = x.astype(BF16)
    rest = x - hi.astype(F32)
    mid = rest.astype(BF16)
    lo = (rest - mid.astype(F32)).astype(BF16)
    return hi, mid, lo


def _forget_cumsum_kernel(h_ref, wf_ref, b_ref, spread_ref, ones_ref, c_ref, augk_ref, carry_ref,
                          wf_bf, *, tiles_per_seq):
    @pl.when(pl.program_id(0) == 0)
    def _():
        wf_bf[...] = wf_ref[...].astype(BF16)

    @pl.when(pl.program_id(0) % tiles_per_seq == 0)
    def _():
        carry_ref[...] = jnp.zeros_like(carry_ref)

    ts = h_ref.shape[0]
    nh = b_ref.shape[1]
    log_f = jax.nn.log_sigmoid(_dot_nt(h_ref[...], wf_bf[...])[:, 0:nh] + b_ref[...])
    row = lax.broadcasted_iota(jnp.int32, (ts, ts), 0)
    col = lax.broadcasted_iota(jnp.int32, (ts, ts), 1)
    lower = (col <= row).astype(F32)
    csum = jnp.dot(lower, log_f, preferred_element_type=F32,
                   precision=lax.Precision.HIGHEST) + carry_ref[...]
    c_ref[...] = csum
    carry_ref[...] = c_ref[ts - 1:ts, :]

    hi, mid, lo = _split_bf16x3(csum * LOG2_E)
    aug = (_dot(hi, spread_ref[0]) + _dot(mid, spread_ref[1]) + _dot(lo, spread_ref[2])
           + ones_ref[...])
    augk_ref[...] = aug.astype(BF16)


def _forget_cumsum(h, w_in_t, row_start, bias, seq):
    t, d = h.shape
    nh = bias.shape[0]
    ts = CUMSUM_TILE
    assert row_start % LANES == 0 and nh <= LANES
    width = nh * HEAD_DIM
    lane = jnp.arange(width)[None, :]
    head = jnp.arange(nh)[:, None]
    spread = jnp.stack([jnp.where(lane == head * HEAD_DIM + j, -1.0, 0.0) for j in range(3)])
    ones = jnp.where((lane % HEAD_DIM >= 3) & (lane % HEAD_DIM < 6), 1.0, 0.0).astype(F32)
    return pl.pallas_call(
        functools.partial(_forget_cumsum_kernel, tiles_per_seq=seq // ts),
        grid=(t // ts,),
        in_specs=[pl.BlockSpec((ts, d), lambda i: (i, 0)),
                  pl.BlockSpec((LANES, d), lambda i: (row_start // LANES, 0)),
                  pl.BlockSpec((1, nh), lambda i: (0, 0)),
                  pl.BlockSpec((3, nh, width), lambda i: (0, 0, 0)),
                  pl.BlockSpec((1, width), lambda i: (0, 0))],
        out_specs=[pl.BlockSpec((ts, nh), lambda i: (i, 0)),
                   pl.BlockSpec((ts, width), lambda i: (i, 0))],
        out_shape=[jax.ShapeDtypeStruct((t, nh), F32), jax.ShapeDtypeStruct((t, width), BF16)],
        scratch_shapes=[pltpu.VMEM((1, nh), F32), pltpu.VMEM((LANES, d), BF16)],
        compiler_params=_params(("arbitrary",), 48),
        name="forget_cumsum",
    )(h, w_in_t, bias.reshape(1, nh), spread.astype(BF16), ones)


AUG_ROWS = 16
SUM_ROWS = 16
HEADS_PER_STEP = 2


def _attention_kernel(q_ref, k_ref, v_ref, augk_ref, c_ref, o_ref,
                      kx_ref, vt_ref, qxt_ref, m_ref, *buffers, tile):
    n = HEADS_PER_STEP
    buffers = [buf.at[:, 0:tile] for buf in buffers]
    s_ref = [buffers[2 * g:2 * g + 2] for g in range(n)]
    p_ref = [buffers[2 * n + 2 * g:2 * n + 2 * g + 2] for g in range(n)]
    acc_ref = buffers[4 * n:]
    kt = tile // 2
    qi = pl.program_id(2)
    n_key_tiles = kx_ref.shape[1]
    heads = range(HEADS_PER_STEP)

    def lanes(g):
        return slice(g * HEAD_DIM, (g + 1) * HEAD_DIM)

    @pl.when(qi == 0)
    def _():
        def stage(j, _):
            rows = pl.ds(pl.multiple_of(j * kt, kt), kt)
            for g in heads:
                kx_ref[g, j, :, 0:HEAD_DIM] = k_ref[rows, lanes(g)]
                kx_ref[g, j, :, HEAD_DIM:] = augk_ref[rows, lanes(g)]
                vt_ref[g, j, 0:HEAD_DIM, :] = v_ref[rows, lanes(g)].astype(F32).T.astype(BF16)
                vt_ref[g, j, HEAD_DIM:, :] = jnp.ones((SUM_ROWS, kt), BF16)
            return 0
        lax.fori_loop(0, n_key_tiles, stage, 0)
        for g in heads:
            qxt_ref[g, HEAD_DIM + AUG_ROWS:, :] = jnp.zeros((HEAD_DIM - AUG_ROWS, tile), BF16)

    r = lax.broadcasted_iota(jnp.int32, (AUG_ROWS, tile), 0)
    for g in heads:
        qxt_ref[g, 0:HEAD_DIM, :] = q_ref[:, lanes(g)].astype(F32).T.astype(BF16)
        hi, mid, lo = _split_bf16x3(c_ref[g, qi] * LOG2_E)
        aug = jnp.where(r < 3, 1.0,
                        jnp.where(r == 3, hi.astype(F32),
                                  jnp.where(r == 4, mid.astype(F32),
                                            jnp.where(r == 5, lo.astype(F32), 0.0))))
        qxt_ref[g, HEAD_DIM:HEAD_DIM + AUG_ROWS, :] = aug.astype(BF16)

    def scores_into(g, slot, ki):
        s = _dot(kx_ref[g, ki], qxt_ref[g])
        s_ref[g][slot][...] = s
        return jnp.max(s, axis=0, keepdims=True)

    def softmax_into(g, slot, m, tile_max):
        m_new = jnp.maximum(m, tile_max)
        p_ref[g][slot][...] = jnp.exp2(s_ref[g][slot][...] - m_new).astype(BF16)
        return m_new, jnp.exp2(m - m_new)

    def accumulate(g, slot, alpha, ki):
        acc_ref[g][...] = alpha * acc_ref[g][...] + _dot(vt_ref[g, ki], p_ref[g][slot][...])

    for g in heads:
        p_ref[g][1][...] = jnp.zeros((kt, tile), BF16)
        acc_ref[g][...] = jnp.zeros(acc_ref[g].shape, F32)
    max_0 = tuple(scores_into(g, 0, 0) for g in heads)

    def pair(jj, carry):
        m, alpha_1, max_0 = carry
        k0 = 2 * jj
        max_1 = [scores_into(g, 1, k0 + 1) for g in heads]
        m, alpha_0 = zip(*[softmax_into(g, 0, m[g], max_0[g]) for g in heads])
        for g in heads:
            accumulate(g, 1, alpha_1[g], jnp.maximum(k0 - 1, 0))
        max_0 = tuple(scores_into(g, 0, k0 + 2) for g in heads)
        m, alpha_1 = zip(*[softmax_into(g, 1, m[g], max_1[g]) for g in heads])
        for g in heads:
            accumulate(g, 0, alpha_0[g], k0)
        return m, alpha_1, max_0

    init = (tuple(jnp.full((1, tile), NEG_INF, F32) for _ in heads),
            tuple(jnp.ones((1, tile), F32) for _ in heads), max_0)
    m, alpha_1, _ = lax.fori_loop(0, qi, pair, init)

    k0 = 2 * qi
    key = lax.broadcasted_iota(jnp.int32, (kt, tile), 0)
    query = lax.broadcasted_iota(jnp.int32, (kt, tile), 1)
    upper = slice(kt, tile)

    def upper_scores_into(g):
        s = _dot(kx_ref[g, k0 + 1], qxt_ref[g, :, upper])
        s = jnp.where(lax.broadcasted_iota(jnp.int32, (kt, kt), 0)
                      <= lax.broadcasted_iota(jnp.int32, (kt, kt), 1), s, NEG_INF)
        s_ref[g][1][:, 0:kt] = s
        return jnp.max(s, axis=0, keepdims=True)

    max_1 = [upper_scores_into(g) for g in heads]
    for g in heads:
        s_ref[g][0][...] = jnp.where(key <= query, s_ref[g][0][...], NEG_INF)
    max_0 = [jnp.max(s_ref[g][0][...], axis=0, keepdims=True) for g in heads]
    m, alpha_0 = zip(*[softmax_into(g, 0, m[g], max_0[g]) for g in heads])
    for g in heads:
        accumulate(g, 1, alpha_1[g], jnp.maximum(k0 - 1, 0))
    alpha_upper = []
    for g in heads:
        m_ref[g, 0:1, :] = m[g]
        m_old = m_ref[g, 0:1, upper]
        m_upper = jnp.maximum(m_old, max_1[g])
        alpha_upper.append(jnp.exp2(m_old - m_upper))
        p_ref[g][1][:, 0:kt] = jnp.exp2(s_ref[g][1][:, 0:kt] - m_upper).astype(BF16)
    for g in heads:
        accumulate(g, 0, alpha_0[g], k0)
    for g in heads:
        acc_ref[g][:, upper] = (alpha_upper[g] * acc_ref[g][:, upper]
                                + _dot(vt_ref[g, k0 + 1], p_ref[g][1][:, 0:kt]))
    for g in heads:
        o_t = acc_ref[g][0:HEAD_DIM, :] / acc_ref[g][HEAD_DIM:HEAD_DIM + 1, :]
        o_ref[:, lanes(g)] = o_t.T.astype(BF16)


def _attention(qkv, augk, c, batch, seq):
    t = qkv.shape[0]
    tile = ATTN_TILE
    kt = tile // 2
    nq = seq // tile
    g = HEADS_PER_STEP
    width = g * HEAD_DIM
    groups = N_HEADS // g
    c_rows = jnp.transpose(c.reshape(batch, seq, N_HEADS), (0, 2, 1)).reshape(
        batch * N_HEADS, nq, 1, tile)
    return pl.pallas_call(
        functools.partial(_attention_kernel, tile=tile),
        grid=(batch, groups, nq),
        in_specs=[
            pl.BlockSpec((tile, width), lambda b, h, i: (b * nq + i, h)),
            pl.BlockSpec((seq, width), lambda b, h, i: (b, groups + h), pipeline_mode=pl.Buffered(1)),
            pl.BlockSpec((seq, width), lambda b, h, i: (b, 2 * groups + h), pipeline_mode=pl.Buffered(1)),
            pl.BlockSpec((seq, width), lambda b, h, i: (b, h), pipeline_mode=pl.Buffered(1)),
            pl.BlockSpec((g, nq, 1, tile), lambda b, h, i: (b * groups + h, 0, 0, 0)),
        ],
        out_specs=pl.BlockSpec((tile, width), lambda b, h, i: (b * nq + i, h)),
        out_shape=jax.ShapeDtypeStruct((t, N_HEADS * HEAD_DIM), BF16),
        scratch_shapes=[pltpu.VMEM((g, seq // kt, kt, 2 * HEAD_DIM), BF16),
                        pltpu.VMEM((g, seq // kt, HEAD_DIM + SUM_ROWS, kt), BF16),
                        pltpu.VMEM((g, 2 * HEAD_DIM, tile), BF16),
                        pltpu.VMEM((g, 8, tile), F32),
                        *[pltpu.VMEM((kt, tile + LANES), F32)] * (2 * g),
                        *[pltpu.VMEM((kt, tile + LANES), BF16)] * (2 * g),
                        *[pltpu.VMEM((HEAD_DIM + SUM_ROWS, tile + LANES), F32)] * g],
        compiler_params=_params(("parallel", "parallel", "arbitrary"), 56),
        name="fox_attention",
    )(qkv, qkv, qkv, augk, c_rows)


def kernel(x, p, norm_mix_pre, w_in, forget_bias, conv_mix_w, w_branch_conv, w_branch_attn, w_out,
           norm_mix_post, norm_ffn_pre, w_up, ffn_conv_w, ffn_conv_b, w_down, norm_ffn_post,
           w_ple_proj, norm_ple_gate, w_ple_gate, norm_ple_post):
    batch, seq, d = x.shape
    t = batch * seq
    depth = w_in.shape[0]
    conv_width = conv_mix_w.shape[-1]
    attn_width = N_HEADS * HEAD_DIM
    qkv_start = 3 * conv_width
    forget_start = qkv_start + 3 * attn_width
    gate_start = forget_start + N_HEADS

    xs = x.reshape(t, d)
    for i in range(depth):
        w_in_t = jnp.transpose(w_in[i])

        h1 = _norm(xs, norm_mix_pre[i])
        a = _conv_branch(h1, w_in_t, conv_mix_w[i], seq)
        qkv = _qkv(h1, w_in_t, qkv_start, 3 * attn_width)
        gates = _gates(h1, w_in_t, gate_start, 2 * d)
        c, augk = _forget_cumsum(h1, w_in_t, forget_start, forget_bias[i], seq)
        o = _attention(qkv, augk, c, batch, seq)
        merged = _merge(a, o, w_branch_conv[i].astype(BF16), w_branch_attn[i].astype(BF16), gates)
        y1 = _matmul(merged, w_out[i], ROW_TILE_MM, 512, "out_proj")
        x1, h2 = _residual_norm(xs, y1, norm_mix_post[i], norm_ffn_pre[i])

        ffn = _ffn_up(h2, w_up[i], ffn_conv_w[i], ffn_conv_b[i], seq)
        y2 = _matmul(ffn, w_down[i].astype(BF16), ROW_TILE_WIDE_K, 512, "ffn_down")
        x2, h3 = _residual_norm(x1, y2, norm_ffn_post[i], norm_ple_gate[i])

        ge = _ple(h3, p[i].reshape(t, -1), w_ple_gate[i], w_ple_proj[i])
        xs = _residual(x2, ge, norm_ple_post[i])
    return xs.reshape(batch, seq, d)
```

```python
import functools
import math

import jax
import jax.numpy as jnp
from jax import lax
from jax.experimental import pallas as pl
from jax.experimental.pallas import tpu as pltpu

BF16 = jnp.bfloat16
F32 = jnp.float32

LANES = 128
BF16_SUBLANES = 16
N_HEADS = 32
HEAD_DIM = 128
SHORT_K = 3
EPS = 1e-6
NEG_INF = -1e30
LOG2_E = math.log2(math.e)
GELU_C0 = math.sqrt(2.0 / math.pi)
GELU_C1 = GELU_C0 * 0.044715
MIB = 1024 * 1024

ROW_TILE_NORM = 256
ROW_TILE_MM = 1024
ROW_TILE_WIDE_K = 512
ATTN_TILE = 1024
CUMSUM_TILE = 512
CARRY_ROWS = 8


def _params(semantics, vmem_mib, flags=None):
    return pltpu.CompilerParams(dimension_semantics=semantics, vmem_limit_bytes=vmem_mib * MIB,
                                flags=flags)


def _dot(a, b):
    return jnp.dot(a, b, preferred_element_type=F32)


def _dot_nt(a, b):
    return lax.dot_general(a, b, (((1,), (1,)), ((), ())), preferred_element_type=F32)


def _sigmoid(x):
    return 0.5 * jnp.tanh(0.5 * x) + 0.5


def _rms_scale(x, g):
    inv = lax.rsqrt(jnp.mean(x * x, axis=-1, keepdims=True) + EPS)
    return x * inv * g


def _norm_kernel(x_ref, g_ref, h_ref):
    h_ref[...] = _rms_scale(x_ref[...], g_ref[...]).astype(BF16)


def _residual_norm_kernel(x_ref, y_ref, g_post_ref, g_pre_ref, x_out_ref, h_ref):
    x_new = x_ref[...] + _rms_scale(y_ref[...].astype(F32), g_post_ref[...])
    x_out_ref[...] = x_new
    h_ref[...] = _rms_scale(x_new, g_pre_ref[...]).astype(BF16)


def _residual_kernel(x_ref, y_ref, g_post_ref, x_out_ref):
    x_out_ref[...] = x_ref[...] + _rms_scale(y_ref[...].astype(F32), g_post_ref[...])


def _row_spec(tr, d):
    return pl.BlockSpec((tr, d), lambda i: (i, 0))


def _gain_spec(d):
    return pl.BlockSpec((1, d), lambda i: (0, 0))


def _norm(x, g):
    t, d = x.shape
    tr = ROW_TILE_NORM
    return pl.pallas_call(
        _norm_kernel,
        grid=(t // tr,),
        in_specs=[_row_spec(tr, d), _gain_spec(d)],
        out_specs=_row_spec(tr, d),
        out_shape=jax.ShapeDtypeStruct((t, d), BF16),
        compiler_params=_params(("parallel",), 40),
        name="norm",
    )(x, g.reshape(1, d))


def _residual_norm(x, y, g_post, g_pre):
    t, d = x.shape
    tr = ROW_TILE_NORM
    return pl.pallas_call(
        _residual_norm_kernel,
        grid=(t // tr,),
        in_specs=[_row_spec(tr, d), _row_spec(tr, d), _gain_spec(d), _gain_spec(d)],
        out_specs=[_row_spec(tr, d), _row_spec(tr, d)],
        out_shape=[jax.ShapeDtypeStruct((t, d), F32), jax.ShapeDtypeStruct((t, d), BF16)],
        compiler_params=_params(("parallel",), 40),
        name="residual_norm",
    )(x, y, g_post.reshape(1, d), g_pre.reshape(1, d))


def _residual(x, y, g_post):
    t, d = x.shape
    tr = ROW_TILE_NORM
    return pl.pallas_call(
        _residual_kernel,
        grid=(t // tr,),
        in_specs=[_row_spec(tr, d), _row_spec(tr, d), _gain_spec(d)],
        out_specs=_row_spec(tr, d),
        out_shape=jax.ShapeDtypeStruct((t, d), F32),
        compiler_params=_params(("parallel",), 40),
        name="residual",
    )(x, y, g_post.reshape(1, d))


def _causal_conv3(u, carry_ref, w_ref):
    tm = u.shape[0]
    w0, w1, w2 = w_ref[0:1, :], w_ref[1:2, :], w_ref[2:3, :]
    y = w0 * pltpu.roll(u, 2, 0) + w1 * pltpu.roll(u, 1, 0) + w2 * u
    head = u[0:CARRY_ROWS, :]
    prev = carry_ref[...]
    r = lax.broadcasted_iota(jnp.int32, head.shape, 0)
    back1 = jnp.where(r < 1, pltpu.roll(prev, 1, 0), pltpu.roll(head, 1, 0))
    back2 = jnp.where(r < 2, pltpu.roll(prev, 2, 0), pltpu.roll(head, 2, 0))
    y_head = w0 * back2 + w1 * back1 + w2 * head
    carry_ref[...] = u[tm - CARRY_ROWS:, :]
    return jnp.concatenate([y_head, y[CARRY_ROWS:, :]], axis=0)


def _reset_carry_at_sequence_start(carry_refs, tiles_per_seq):
    @pl.when(pl.program_id(1) % tiles_per_seq == 0)
    def _():
        for ref in carry_refs:
            ref[...] = jnp.zeros_like(ref)


def _cast_weights_at_sweep_start(pairs):
    @pl.when(pl.program_id(1) == 0)
    def _():
        for w_ref, w_bf16_ref in pairs:
            w_bf16_ref[...] = w_ref[...].astype(BF16)


def _conv_branch_kernel(h_ref, wb_ref, wc_ref, wv_ref, cw_ref, o_ref, carry_ref,
                        wb_bf, wc_bf, wv_bf, *, tiles_per_seq):
    _cast_weights_at_sweep_start([(wb_ref, wb_bf), (wc_ref, wc_bf), (wv_ref, wv_bf)])
    _reset_carry_at_sequence_start([carry_ref], tiles_per_seq)
    h = h_ref[...]
    u = _dot_nt(h, wc_bf[...]) * _dot_nt(h, wv_bf[...])
    y = _causal_conv3(u, carry_ref, cw_ref)
    o_ref[...] = (_dot_nt(h, wb_bf[...]) * y).astype(BF16)


def _conv_branch(h, w_in_t, conv_w, seq):
    t, d = h.shape
    width = conv_w.shape[1]
    tm, tn = ROW_TILE_WIDE_K, 256
    nb = width // tn
    return pl.pallas_call(
        functools.partial(_conv_branch_kernel, tiles_per_seq=seq // tm),
        grid=(nb, t // tm),
        in_specs=[
            pl.BlockSpec((tm, d), lambda j, i: (i, 0)),
            pl.BlockSpec((tn, d), lambda j, i: (j, 0)),
            pl.BlockSpec((tn, d), lambda j, i: (j + nb, 0)),
            pl.BlockSpec((tn, d), lambda j, i: (j + 2 * nb, 0)),
            pl.BlockSpec((SHORT_K, tn), lambda j, i: (0, j)),
        ],
        out_specs=pl.BlockSpec((tm, tn), lambda j, i: (i, j)),
        out_shape=jax.ShapeDtypeStruct((t, width), BF16),
        scratch_shapes=[pltpu.VMEM((CARRY_ROWS, tn), F32)] + [pltpu.VMEM((tn, d), BF16)] * 3,
        compiler_params=_params(("parallel", "arbitrary"), 48),
        name="conv_branch",
    )(h, w_in_t, w_in_t, w_in_t, conv_w)


def _qkv_kernel(h_ref, w_ref, o_ref, w_bf, *, q_tiles, scale):
    _cast_weights_at_sweep_start([(w_ref, w_bf)])
    acc = _dot_nt(h_ref[...], w_bf[...])
    factor = jnp.where(pl.program_id(0) < q_tiles, scale, 1.0).astype(F32)
    o_ref[...] = (acc * factor).astype(BF16)


def _qkv(h, w_in_t, row_start, n):
    t, d = h.shape
    tm, tn = ROW_TILE_MM, 512
    first = row_start // tn
    kern = functools.partial(_qkv_kernel, q_tiles=(n // 3) // tn, scale=LOG2_E / math.sqrt(HEAD_DIM))
    return pl.pallas_call(
        kern,
        grid=(n // tn, t // tm),
        in_specs=[pl.BlockSpec((tm, d), lambda j, i: (i, 0)),
                  pl.BlockSpec((tn, d), lambda j, i: (first + j, 0))],
        out_specs=pl.BlockSpec((tm, tn), lambda j, i: (i, j)),
        out_shape=jax.ShapeDtypeStruct((t, n), BF16),
        scratch_shapes=[pltpu.VMEM((tn, d), BF16)],
        compiler_params=_params(("parallel", "arbitrary"), 48),
        name="qkv",
    )(h, w_in_t)


def _gates_kernel(h_ref, w_ref, w_next_ref, o_ref, w_bf, *, shift):
    @pl.when(pl.program_id(1) == 0)
    def _():
        tn = w_bf.shape[0]
        w_bf[0:tn - shift, :] = w_ref[shift:tn, :].astype(BF16)
        w_bf[tn - shift:tn, :] = w_next_ref[0:shift, :].astype(BF16)

    o_ref[...] = _sigmoid(_dot_nt(h_ref[...], w_bf[...])).astype(BF16)


def _gates(h, w_in_t, row_start, n):
    t, d = h.shape
    tm, tn = ROW_TILE_MM, 512
    shift = row_start % LANES
    aligned = row_start - shift
    assert 0 < shift and shift % BF16_SUBLANES == 0 and aligned % tn == 0
    return pl.pallas_call(
        functools.partial(_gates_kernel, shift=shift),
        grid=(n // tn, t // tm),
        in_specs=[pl.BlockSpec((tm, d), lambda j, i: (i, 0)),
                  pl.BlockSpec((tn, d), lambda j, i: (aligned // tn + j, 0)),
                  pl.BlockSpec((LANES, d), lambda j, i: ((aligned + (j + 1) * tn) // LANES, 0))],
        out_specs=pl.BlockSpec((tm, tn), lambda j, i: (i, j)),
        out_shape=jax.ShapeDtypeStruct((t, n), BF16),
        scratch_shapes=[pltpu.VMEM((tn, d), BF16)],
        compiler_params=_params(("parallel", "arbitrary"), 48),
        name="gates",
    )(h, w_in_t, w_in_t)


def _matmul_kernel(a_ref, w_ref, o_ref):
    o_ref[...] = _dot(a_ref[...], w_ref[...]).astype(BF16)


def _matmul_f32_weights_kernel(a_ref, w_ref, o_ref, w_bf):
    _cast_weights_at_sweep_start([(w_ref, w_bf)])
    o_ref[...] = _dot(a_ref[...], w_bf[...]).astype(BF16)


def _matmul(a, w, tm, tn, name):
    t, k = a.shape
    n = w.shape[1]
    cast_in_kernel = w.dtype == F32
    return pl.pallas_call(
        _matmul_f32_weights_kernel if cast_in_kernel else _matmul_kernel,
        grid=(n // tn, t // tm),
        in_specs=[pl.BlockSpec((tm, k), lambda j, i: (i, 0)),
                  pl.BlockSpec((k, tn), lambda j, i: (0, j))],
        out_specs=pl.BlockSpec((tm, tn), lambda j, i: (i, j)),
        out_shape=jax.ShapeDtypeStruct((t, n), BF16),
        scratch_shapes=[pltpu.VMEM((k, tn), BF16)] if cast_in_kernel else [],
        compiler_params=_params(("parallel", "arbitrary"), 52),
        name=name,
    )(a, w)


def _merge_kernel(a_ref, o_ref, wa_ref, wb_ref, ga_ref, gb_ref, out_ref):
    y_a = _dot(a_ref[...], wa_ref[...])
    y_b = _dot(o_ref[...], wb_ref[...])
    out_ref[...] = (ga_ref[...] * y_a + gb_ref[...] * y_b).astype(BF16)


def _merge(a, o, w_a, w_b, gates):
    t, d = a.shape
    n = w_a.shape[1]
    tm, tn = ROW_TILE_WIDE_K, 512
    nb = n // tn
    return pl.pallas_call(
        _merge_kernel,
        grid=(nb, t // tm),
        in_specs=[
            pl.BlockSpec((tm, d), lambda j, i: (i, 0)),
            pl.BlockSpec((tm, d), lambda j, i: (i, 0)),
            pl.BlockSpec((d, tn), lambda j, i: (0, j)),
            pl.BlockSpec((d, tn), lambda j, i: (0, j)),
            pl.BlockSpec((tm, tn), lambda j, i: (i, j)),
            pl.BlockSpec((tm, tn), lambda j, i: (i, j + nb)),
        ],
        out_specs=pl.BlockSpec((tm, tn), lambda j, i: (i, j)),
        out_shape=jax.ShapeDtypeStruct((t, n), BF16),
        compiler_params=_params(("parallel", "arbitrary"), 48),
        name="merge",
    )(a, o, w_a, w_b, gates, gates)


def _ffn_up_kernel(h_ref, wg_ref, wv_ref, cwg_ref, cwv_ref, bg_ref, bv_ref, o_ref,
                   carry_g_ref, carry_v_ref, wg_bf, wv_bf, *, tiles_per_seq):
    _cast_weights_at_sweep_start([(wg_ref, wg_bf), (wv_ref, wv_bf)])
    _reset_carry_at_sequence_start([carry_g_ref, carry_v_ref], tiles_per_seq)
    h = h_ref[...]
    u_gate = _causal_conv3(_dot(h, wg_bf[...]), carry_g_ref, cwg_ref) + bg_ref[...]
    u_val = _causal_conv3(_dot(h, wv_bf[...]), carry_v_ref, cwv_ref) + bv_ref[...]
    half_gate = 0.5 * u_gate
    t = jnp.tanh(u_gate * (GELU_C0 + GELU_C1 * (u_gate * u_gate)))
    o_ref[...] = ((half_gate + half_gate * t) * u_val).astype(BF16)


def _ffn_up(h, w_up, conv_w, conv_b, seq):
    t, d = h.shape
    d_ff = w_up.shape[1] // 2
    tm, tn = ROW_TILE_MM, 256
    nb = d_ff // tn
    conv_b = conv_b.reshape(1, 2 * d_ff)
    return pl.pallas_call(
        functools.partial(_ffn_up_kernel, tiles_per_seq=seq // tm),
        grid=(nb, t // tm),
        in_specs=[
            pl.BlockSpec((tm, d), lambda j, i: (i, 0)),
            pl.BlockSpec((d, tn), lambda j, i: (0, j)),
            pl.BlockSpec((d, tn), lambda j, i: (0, j + nb)),
            pl.BlockSpec((SHORT_K, tn), lambda j, i: (0, j)),
            pl.BlockSpec((SHORT_K, tn), lambda j, i: (0, j + nb)),
            pl.BlockSpec((1, tn), lambda j, i: (0, j)),
            pl.BlockSpec((1, tn), lambda j, i: (0, j + nb)),
        ],
        out_specs=pl.BlockSpec((tm, tn), lambda j, i: (i, j)),
        out_shape=jax.ShapeDtypeStruct((t, d_ff), BF16),
        scratch_shapes=[pltpu.VMEM((CARRY_ROWS, tn), F32)] * 2 + [pltpu.VMEM((d, tn), BF16)] * 2,
        compiler_params=_params(("parallel", "arbitrary"), 48),
        name="ffn_up",
    )(h, w_up, w_up, conv_w, conv_w, conv_b, conv_b)


def _ple_kernel(h_ref, p_ref, wg_ref, wp_ref, o_ref, wg_bf, wp_bf):
    _cast_weights_at_sweep_start([(wg_ref, wg_bf), (wp_ref, wp_bf)])
    gate = _sigmoid(_dot(h_ref[...], wg_bf[...]))
    o_ref[...] = (gate * _dot(p_ref[...].astype(BF16), wp_bf[...])).astype(BF16)


def _ple(h, p, w_gate, w_proj):
    t, d = h.shape
    n = w_gate.shape[1]
    ple = p.shape[1]
    tm, tn = ROW_TILE_MM, 512
    return pl.pallas_call(
        _ple_kernel,
        grid=(n // tn, t // tm),
        in_specs=[
            pl.BlockSpec((tm, d), lambda j, i: (i, 0)),
            pl.BlockSpec((tm, ple), lambda j, i: (i, 0)),
            pl.BlockSpec((d, tn), lambda j, i: (0, j)),
            pl.BlockSpec((ple, tn), lambda j, i: (0, j)),
        ],
        out_specs=pl.BlockSpec((tm, tn), lambda j, i: (i, j)),
        out_shape=jax.ShapeDtypeStruct((t, n), BF16),
        scratch_shapes=[pltpu.VMEM((d, tn), BF16), pltpu.VMEM((ple, tn), BF16)],
        compiler_params=_params(("parallel", "arbitrary"), 48),
        name="ple",
    )(h, p, w_gate, w_proj)


def _split_bf16x3(x):
    hi = x.astype(BF16)
    rest = x - hi.astype(F32)
    mid = rest.astype(BF16)
    lo = (rest - mid.astype(F32)).astype(BF16)
    return hi, mid, lo


def _forget_cumsum_kernel(h_ref, wf_ref, b_ref, spread_ref, ones_ref, c_ref, augk_ref, carry_ref,
                          wf_bf, *, tiles_per_seq):
    @pl.when(pl.program_id(0) == 0)
    def _():
        wf_bf[...] = wf_ref[...].astype(BF16)

    @pl.when(pl.program_id(0) % tiles_per_seq == 0)
    def _():
        carry_ref[...] = jnp.zeros_like(carry_ref)

    ts = h_ref.shape[0]
    nh = b_ref.shape[1]
    log_f = jax.nn.log_sigmoid(_dot_nt(h_ref[...], wf_bf[...])[:, 0:nh] + b_ref[...])
    row = lax.broadcasted_iota(jnp.int32, (ts, ts), 0)
    col = lax.broadcasted_iota(jnp.int32, (ts, ts), 1)
    lower = (col <= row).astype(F32)
    csum = jnp.dot(lower, log_f, preferred_element_type=F32,
                   precision=lax.Precision.HIGHEST) + carry_ref[...]
    c_ref[...] = csum
    carry_ref[...] = c_ref[ts - 1:ts, :]

    hi, mid, lo = _split_bf16x3(csum * LOG2_E)
    aug = (_dot(hi, spread_ref[0]) + _dot(mid, spread_ref[1]) + _dot(lo, spread_ref[2])
           + ones_ref[...])
    augk_ref[...] = aug.astype(BF16)


def _forget_cumsum(h, w_in_t, row_start, bias, seq):
    t, d = h.shape
    nh = bias.shape[0]
    ts = CUMSUM_TILE
    assert row_start % LANES == 0 and nh <= LANES
    width = nh * HEAD_DIM
    lane = jnp.arange(width)[None, :]
    head = jnp.arange(nh)[:, None]
    spread = jnp.stack([jnp.where(lane == head * HEAD_DIM + j, -1.0, 0.0) for j in range(3)])
    ones = jnp.where((lane % HEAD_DIM >= 3) & (lane % HEAD_DIM < 6), 1.0, 0.0).astype(F32)
    return pl.pallas_call(
        functools.partial(_forget_cumsum_kernel, tiles_per_seq=seq // ts),
        grid=(t // ts,),
        in_specs=[pl.BlockSpec((ts, d), lambda i: (i, 0)),
                  pl.BlockSpec((LANES, d), lambda i: (row_start // LANES, 0)),
                  pl.BlockSpec((1, nh), lambda i: (0, 0)),
                  pl.BlockSpec((3, nh, width), lambda i: (0, 0, 0)),
                  pl.BlockSpec((1, width), lambda i: (0, 0))],
        out_specs=[pl.BlockSpec((ts, nh), lambda i: (i, 0)),
                   pl.BlockSpec((ts, width), lambda i: (i, 0))],
        out_shape=[jax.ShapeDtypeStruct((t, nh), F32), jax.ShapeDtypeStruct((t, width), BF16)],
        scratch_shapes=[pltpu.VMEM((1, nh), F32), pltpu.VMEM((LANES, d), BF16)],
        compiler_params=_params(("arbitrary",), 48),
        name="forget_cumsum",
    )(h, w_in_t, bias.reshape(1, nh), spread.astype(BF16), ones)


AUG_ROWS = 16
SUM_ROWS = 16
HEADS_PER_STEP = 2


def _attention_kernel(q_ref, k_ref, v_ref, augk_ref, c_ref, o_ref,
                      kx_ref, vt_ref, qxt_ref, m_ref, *buffers, tile):
    n = HEADS_PER_STEP
    buffers = [buf.at[:, 0:tile] for buf in buffers]
    s_ref = [buffers[2 * g:2 * g + 2] for g in range(n)]
    p_ref = [buffers[2 * n + 2 * g:2 * n + 2 * g + 2] for g in range(n)]
    acc_ref = buffers[4 * n:]
    kt = tile // 2
    qi = pl.program_id(2)
    n_key_tiles = kx_ref.shape[1]
    heads = range(HEADS_PER_STEP)

    def lanes(g):
        return slice(g * HEAD_DIM, (g + 1) * HEAD_DIM)

    @pl.when(qi == 0)
    def _():
        def stage(j, _):
            rows = pl.ds(pl.multiple_of(j * kt, kt), kt)
            for g in heads:
                kx_ref[g, j, :, 0:HEAD_DIM] = k_ref[rows, lanes(g)]
                kx_ref[g, j, :, HEAD_DIM:] = augk_ref[rows, lanes(g)]
                vt_ref[g, j, 0:HEAD_DIM, :] = v_ref[rows, lanes(g)].astype(F32).T.astype(BF16)
                vt_ref[g, j, HEAD_DIM:, :] = jnp.ones((SUM_ROWS, kt), BF16)
            return 0
        lax.fori_loop(0, n_key_tiles, stage, 0)
        for g in heads:
            qxt_ref[g, HEAD_DIM + AUG_ROWS:, :] = jnp.zeros((HEAD_DIM - AUG_ROWS, tile), BF16)

    r = lax.broadcasted_iota(jnp.int32, (AUG_ROWS, tile), 0)
    for g in heads:
        qxt_ref[g, 0:HEAD_DIM, :] = q_ref[:, lanes(g)].astype(F32).T.astype(BF16)
        hi, mid, lo = _split_bf16x3(c_ref[g, qi] * LOG2_E)
        aug = jnp.where(r < 3, 1.0,
                        jnp.where(r == 3, hi.astype(F32),
                                  jnp.where(r == 4, mid.astype(F32),
                                            jnp.where(r == 5, lo.astype(F32), 0.0))))
        qxt_ref[g, HEAD_DIM:HEAD_DIM + AUG_ROWS, :] = aug.astype(BF16)

    def scores_into(g, slot, ki):
        s = _dot(kx_ref[g, ki], qxt_ref[g])
        s_ref[g][slot][...] = s
        return jnp.max(s, axis=0, keepdims=True)

    def softmax_into(g, slot, m, tile_max):
        m_new = jnp.maximum(m, tile_max)
        p_ref[g][slot][...] = jnp.exp2(s_ref[g][slot][...] - m_new).astype(BF16)
        return m_new, jnp.exp2(m - m_new)

    def accumulate(g, slot, alpha, ki):
        acc_ref[g][...] = alpha * acc_ref[g][...] + _dot(vt_ref[g, ki], p_ref[g][slot][...])

    for g in heads:
        p_ref[g][1][...] = jnp.zeros((kt, tile), BF16)
        acc_ref[g][...] = jnp.zeros(acc_ref[g].shape, F32)
    max_0 = tuple(scores_into(g, 0, 0) for g in heads)

    def pair(jj, carry):
        m, alpha_1, max_0 = carry
        k0 = 2 * jj
        max_1 = [scores_into(g, 1, k0 + 1) for g in heads]
        m, alpha_0 = zip(*[softmax_into(g, 0, m[g], max_0[g]) for g in heads])
        for g in heads:
            accumulate(g, 1, alpha_1[g], jnp.maximum(k0 - 1, 0))
        max_0 = tuple(scores_into(g, 0, k0 + 2) for g in heads)
        m, alpha_1 = zip(*[softmax_into(g, 1, m[g], max_1[g]) for g in heads])
        for g in heads:
            accumulate(g, 0, alpha_0[g], k0)
        return m, alpha_1, max_0

    init = (tuple(jnp.full((1, tile), NEG_INF, F32) for _ in heads),
            tuple(jnp.ones((1, tile), F32) for _ in heads), max_0)
    m, alpha_1, _ = lax.fori_loop(0, qi, pair, init)

    k0 = 2 * qi
    key = lax.broadcasted_iota(jnp.int32, (kt, tile), 0)
    query = lax.broadcasted_iota(jnp.int32, (kt, tile), 1)
    upper = slice(kt, tile)

    def upper_scores_into(g):
        s = _dot(kx_ref[g, k0 + 1], qxt_ref[g, :, upper])
        s = jnp.where(lax.broadcasted_iota(jnp.int32, (kt, kt), 0)
                      <= lax.broadcasted_iota(jnp.int32, (kt, kt), 1), s, NEG_INF)
        s_ref[g][1][:, 0:kt] = s
        return jnp.max(s, axis=0, keepdims=True)

    max_1 = [upper_scores_into(g) for g in heads]
    for g in heads:
        s_ref[g][0][...] = jnp.where(key <= query, s_ref[g][0][...], NEG_INF)
    max_0 = [jnp.max(s_ref[g][0][...], axis=0, keepdims=True) for g in heads]
    m, alpha_0 = zip(*[softmax_into(g, 0, m[g], max_0[g]) for g in heads])
    for g in heads:
        accumulate(g, 1, alpha_1[g], jnp.maximum(k0 - 1, 0))
    alpha_upper = []
    for g in heads:
        m_ref[g, 0:1, :] = m[g]
        m_old = m_ref[g, 0:1, upper]
        m_upper = jnp.maximum(m_old, max_1[g])
        alpha_upper.append(jnp.exp2(m_old - m_upper))
        p_ref[g][1][:, 0:kt] = jnp.exp2(s_ref[g][1][:, 0:kt] - m_upper).astype(BF16)
    for g in heads:
        accumulate(g, 0, alpha_0[g], k0)
    for g in heads:
        acc_ref[g][:, upper] = (alpha_upper[g] * acc_ref[g][:, upper]
                                + _dot(vt_ref[g, k0 + 1], p_ref[g][1][:, 0:kt]))
    for g in heads:
        o_t = acc_ref[g][0:HEAD_DIM, :] / acc_ref[g][HEAD_DIM:HEAD_DIM + 1, :]
        o_ref[:, lanes(g)] = o_t.T.astype(BF16)


def _attention(qkv, augk, c, batch, seq):
    t = qkv.shape[0]
    tile = ATTN_TILE
    kt = tile // 2
    nq = seq // tile
    g = HEADS_PER_STEP
    width = g * HEAD_DIM
    groups = N_HEADS // g
    c_rows = jnp.transpose(c.reshape(batch, seq, N_HEADS), (0, 2, 1)).reshape(
        batch * N_HEADS, nq, 1, tile)
    return pl.pallas_call(
        functools.partial(_attention_kernel, tile=tile),
        grid=(batch, groups, nq),
        in_specs=[
            pl.BlockSpec((tile, width), lambda b, h, i: (b * nq + i, h)),
            pl.BlockSpec((seq, width), lambda b, h, i: (b, groups + h)),
            pl.BlockSpec((seq, width), lambda b, h, i: (b, 2 * groups + h)),
            pl.BlockSpec((seq, width), lambda b, h, i: (b, h)),
            pl.BlockSpec((g, nq, 1, tile), lambda b, h, i: (b * groups + h, 0, 0, 0)),
        ],
        out_specs=pl.BlockSpec((tile, width), lambda b, h, i: (b * nq + i, h)),
        out_shape=jax.ShapeDtypeStruct((t, N_HEADS * HEAD_DIM), BF16),
        scratch_shapes=[pltpu.VMEM((g, seq // kt, kt, 2 * HEAD_DIM), BF16),
                        pltpu.VMEM((g, seq // kt, HEAD_DIM + SUM_ROWS, kt), BF16),
                        pltpu.VMEM((g, 2 * HEAD_DIM, tile), BF16),
                        pltpu.VMEM((g, 8, tile), F32),
                        *[pltpu.VMEM((kt, tile + LANES), F32)] * (2 * g),
                        *[pltpu.VMEM((kt, tile + LANES), BF16)] * (2 * g),
                        *[pltpu.VMEM((HEAD_DIM + SUM_ROWS, tile + LANES), F32)] * g],
        compiler_params=_params(("parallel", "parallel", "arbitrary"), 60),
        name="fox_attention",
    )(qkv, qkv, qkv, augk, c_rows)


def kernel(x, p, norm_mix_pre, w_in, forget_bias, conv_mix_w, w_branch_conv, w_branch_attn, w_out,
           norm_mix_post, norm_ffn_pre, w_up, ffn_conv_w, ffn_conv_b, w_down, norm_ffn_post,
           w_ple_proj, norm_ple_gate, w_ple_gate, norm_ple_post):
    batch, seq, d = x.shape
    t = batch * seq
    depth = w_in.shape[0]
    conv_width = conv_mix_w.shape[-1]
    attn_width = N_HEADS * HEAD_DIM
    qkv_start = 3 * conv_width
    forget_start = qkv_start + 3 * attn_width
    gate_start = forget_start + N_HEADS

    xs = x.reshape(t, d)
    for i in range(depth):
        w_in_t = jnp.transpose(w_in[i])

        h1 = _norm(xs, norm_mix_pre[i])
        a = _conv_branch(h1, w_in_t, conv_mix_w[i], seq)
        qkv = _qkv(h1, w_in_t, qkv_start, 3 * attn_width)
        gates = _gates(h1, w_in_t, gate_start, 2 * d)
        c, augk = _forget_cumsum(h1, w_in_t, forget_start, forget_bias[i], seq)
        o = _attention(qkv, augk, c, batch, seq)
        merged = _merge(a, o, w_branch_conv[i].astype(BF16), w_branch_attn[i].astype(BF16), gates)
        y1 = _matmul(merged, w_out[i], ROW_TILE_MM, 512, "out_proj")
        x1, h2 = _residual_norm(xs, y1, norm_mix_post[i], norm_ffn_pre[i])

        ffn = _ffn_up(h2, w_up[i], ffn_conv_w[i], ffn_conv_b[i], seq)
        y2 = _matmul(ffn, w_down[i].astype(BF16), ROW_TILE_WIDE_K, 512, "ffn_down")
        x2, h3 = _residual_norm(x1, y2, norm_ffn_post[i], norm_ple_gate[i])

        ge = _ple(h3, p[i].reshape(t, -1), w_ple_gate[i], w_ple_proj[i])
        xs = _residual(x2, ge, norm_ple_post[i])
    return xs.reshape(batch, seq, d)
```

```python
import functools
import math

import jax
import jax.numpy as jnp
from jax import lax
from jax.experimental import pallas as pl
from jax.experimental.pallas import tpu as pltpu

BF16 = jnp.bfloat16
F32 = jnp.float32

LANES = 128
BF16_SUBLANES = 16
N_HEADS = 32
HEAD_DIM = 128
SHORT_K = 3
EPS = 1e-6
NEG_INF = -1e30
LOG2_E = math.log2(math.e)
GELU_C0 = math.sqrt(2.0 / math.pi)
GELU_C1 = GELU_C0 * 0.044715
MIB = 1024 * 1024

ROW_TILE_NORM = 256
ROW_TILE_MM = 1024
ROW_TILE_WIDE_K = 512
ATTN_TILE = 1024
CUMSUM_TILE = 512
CARRY_ROWS = 8


def _params(semantics, vmem_mib, flags=None):
    return pltpu.CompilerParams(dimension_semantics=semantics, vmem_limit_bytes=vmem_mib * MIB,
                                flags=flags)


def _dot(a, b):
    return jnp.dot(a, b, preferred_element_type=F32)


def _dot_nt(a, b):
    return lax.dot_general(a, b, (((1,), (1,)), ((), ())), preferred_element_type=F32)


def _sigmoid(x):
    return 0.5 * jnp.tanh(0.5 * x) + 0.5


def _rms_scale(x, g):
    inv = lax.rsqrt(jnp.mean(x * x, axis=-1, keepdims=True) + EPS)
    return x * inv * g


def _norm_kernel(x_ref, g_ref, h_ref):
    h_ref[...] = _rms_scale(x_ref[...], g_ref[...]).astype(BF16)


def _residual_norm_kernel(x_ref, y_ref, g_post_ref, g_pre_ref, x_out_ref, h_ref):
    x_new = x_ref[...] + _rms_scale(y_ref[...].astype(F32), g_post_ref[...])
    x_out_ref[...] = x_new
    h_ref[...] = _rms_scale(x_new, g_pre_ref[...]).astype(BF16)


def _residual_kernel(x_ref, y_ref, g_post_ref, x_out_ref):
    x_out_ref[...] = x_ref[...] + _rms_scale(y_ref[...].astype(F32), g_post_ref[...])


def _row_spec(tr, d):
    return pl.BlockSpec((tr, d), lambda i: (i, 0))


def _gain_spec(d):
    return pl.BlockSpec((1, d), lambda i: (0, 0))


def _norm(x, g):
    t, d = x.shape
    tr = ROW_TILE_NORM
    return pl.pallas_call(
        _norm_kernel,
        grid=(t // tr,),
        in_specs=[_row_spec(tr, d), _gain_spec(d)],
        out_specs=_row_spec(tr, d),
        out_shape=jax.ShapeDtypeStruct((t, d), BF16),
        compiler_params=_params(("parallel",), 40),
        name="norm",
    )(x, g.reshape(1, d))


def _residual_norm(x, y, g_post, g_pre):
    t, d = x.shape
    tr = ROW_TILE_NORM
    return pl.pallas_call(
        _residual_norm_kernel,
        grid=(t // tr,),
        in_specs=[_row_spec(tr, d), _row_spec(tr, d), _gain_spec(d), _gain_spec(d)],
        out_specs=[_row_spec(tr, d), _row_spec(tr, d)],
        out_shape=[jax.ShapeDtypeStruct((t, d), F32), jax.ShapeDtypeStruct((t, d), BF16)],
        compiler_params=_params(("parallel",), 40),
        name="residual_norm",
    )(x, y, g_post.reshape(1, d), g_pre.reshape(1, d))


def _residual(x, y, g_post):
    t, d = x.shape
    tr = ROW_TILE_NORM
    return pl.pallas_call(
        _residual_kernel,
        grid=(t // tr,),
        in_specs=[_row_spec(tr, d), _row_spec(tr, d), _gain_spec(d)],
        out_specs=_row_spec(tr, d),
        out_shape=jax.ShapeDtypeStruct((t, d), F32),
        compiler_params=_params(("parallel",), 40),
        name="residual",
    )(x, y, g_post.reshape(1, d))


def _causal_conv3(u, carry_ref, w_ref):
    tm = u.shape[0]
    w0, w1, w2 = w_ref[0:1, :], w_ref[1:2, :], w_ref[2:3, :]
    y = w0 * pltpu.roll(u, 2, 0) + w1 * pltpu.roll(u, 1, 0) + w2 * u
    head = u[0:CARRY_ROWS, :]
    prev = carry_ref[...]
    r = lax.broadcasted_iota(jnp.int32, head.shape, 0)
    back1 = jnp.where(r < 1, pltpu.roll(prev, 1, 0), pltpu.roll(head, 1, 0))
    back2 = jnp.where(r < 2, pltpu.roll(prev, 2, 0), pltpu.roll(head, 2, 0))
    y_head = w0 * back2 + w1 * back1 + w2 * head
    carry_ref[...] = u[tm - CARRY_ROWS:, :]
    return jnp.concatenate([y_head, y[CARRY_ROWS:, :]], axis=0)


def _reset_carry_at_sequence_start(carry_refs, tiles_per_seq):
    @pl.when(pl.program_id(1) % tiles_per_seq == 0)
    def _():
        for ref in carry_refs:
            ref[...] = jnp.zeros_like(ref)


def _serpentine(j, i, n_tiles):
    return jnp.where(j % 2 == 0, i, n_tiles - 1 - i)


def _cast_weights_at_sweep_start(pairs):
    @pl.when(pl.program_id(1) == 0)
    def _():
        for w_ref, w_bf16_ref in pairs:
            w_bf16_ref[...] = w_ref[...].astype(BF16)


def _conv_branch_kernel(h_ref, wb_ref, wc_ref, wv_ref, cw_ref, o_ref, carry_ref,
                        wb_bf, wc_bf, wv_bf, *, tiles_per_seq):
    _cast_weights_at_sweep_start([(wb_ref, wb_bf), (wc_ref, wc_bf), (wv_ref, wv_bf)])
    _reset_carry_at_sequence_start([carry_ref], tiles_per_seq)
    h = h_ref[...]
    u = _dot_nt(h, wc_bf[...]) * _dot_nt(h, wv_bf[...])
    y = _causal_conv3(u, carry_ref, cw_ref)
    o_ref[...] = (_dot_nt(h, wb_bf[...]) * y).astype(BF16)


def _conv_branch(h, w_in_t, conv_w, seq):
    t, d = h.shape
    width = conv_w.shape[1]
    tm, tn = ROW_TILE_MM, 256
    nb = width // tn
    return pl.pallas_call(
        functools.partial(_conv_branch_kernel, tiles_per_seq=seq // tm),
        grid=(nb, t // tm),
        in_specs=[
            pl.BlockSpec((tm, d), lambda j, i: (i, 0)),
            pl.BlockSpec((tn, d), lambda j, i: (j, 0)),
            pl.BlockSpec((tn, d), lambda j, i: (j + nb, 0)),
            pl.BlockSpec((tn, d), lambda j, i: (j + 2 * nb, 0)),
            pl.BlockSpec((SHORT_K, tn), lambda j, i: (0, j)),
        ],
        out_specs=pl.BlockSpec((tm, tn), lambda j, i: (i, j)),
        out_shape=jax.ShapeDtypeStruct((t, width), BF16),
        scratch_shapes=[pltpu.VMEM((CARRY_ROWS, tn), F32)] + [pltpu.VMEM((tn, d), BF16)] * 3,
        compiler_params=_params(("parallel", "arbitrary"), 60),
        name="conv_branch",
    )(h, w_in_t, w_in_t, w_in_t, conv_w)


def _qkv_kernel(h_ref, w_ref, o_ref, w_bf, *, q_tiles, scale):
    _cast_weights_at_sweep_start([(w_ref, w_bf)])
    acc = _dot_nt(h_ref[...], w_bf[...])
    factor = jnp.where(pl.program_id(0) < q_tiles, scale, 1.0).astype(F32)
    o_ref[...] = (acc * factor).astype(BF16)


def _qkv(h, w_in_t, row_start, n):
    t, d = h.shape
    tm, tn = ROW_TILE_MM, 512
    first = row_start // tn
    kern = functools.partial(_qkv_kernel, q_tiles=(n // 3) // tn, scale=LOG2_E / math.sqrt(HEAD_DIM))
    return pl.pallas_call(
        kern,
        grid=(n // tn, t // tm),
        in_specs=[pl.BlockSpec((tm, d), lambda j, i: (_serpentine(j, i, t // tm), 0)),
                  pl.BlockSpec((tn, d), lambda j, i: (first + j, 0))],
        out_specs=pl.BlockSpec((tm, tn), lambda j, i: (_serpentine(j, i, t // tm), j)),
        out_shape=jax.ShapeDtypeStruct((t, n), BF16),
        scratch_shapes=[pltpu.VMEM((tn, d), BF16)],
        compiler_params=_params(("parallel", "arbitrary"), 48),
        name="qkv",
    )(h, w_in_t)


def _gates_kernel(h_ref, w_ref, w_next_ref, o_ref, w_bf, *, shift):
    @pl.when(pl.program_id(1) == 0)
    def _():
        tn = w_bf.shape[0]
        w_bf[0:tn - shift, :] = w_ref[shift:tn, :].astype(BF16)
        w_bf[tn - shift:tn, :] = w_next_ref[0:shift, :].astype(BF16)

    o_ref[...] = _sigmoid(_dot_nt(h_ref[...], w_bf[...])).astype(BF16)


def _gates(h, w_in_t, row_start, n):
    t, d = h.shape
    tm, tn = ROW_TILE_MM, 512
    shift = row_start % LANES
    aligned = row_start - shift
    assert 0 < shift and shift % BF16_SUBLANES == 0 and aligned % tn == 0
    return pl.pallas_call(
        functools.partial(_gates_kernel, shift=shift),
        grid=(n // tn, t // tm),
        in_specs=[pl.BlockSpec((tm, d), lambda j, i: (_serpentine(j, i, t // tm), 0)),
                  pl.BlockSpec((tn, d), lambda j, i: (aligned // tn + j, 0)),
                  pl.BlockSpec((LANES, d), lambda j, i: ((aligned + (j + 1) * tn) // LANES, 0))],
        out_specs=pl.BlockSpec((tm, tn), lambda j, i: (_serpentine(j, i, t // tm), j)),
        out_shape=jax.ShapeDtypeStruct((t, n), BF16),
        scratch_shapes=[pltpu.VMEM((tn, d), BF16)],
        compiler_params=_params(("parallel", "arbitrary"), 48),
        name="gates",
    )(h, w_in_t, w_in_t)


def _matmul_kernel(a_ref, w_ref, o_ref):
    o_ref[...] = _dot(a_ref[...], w_ref[...]).astype(BF16)


def _matmul_f32_weights_kernel(a_ref, w_ref, o_ref, w_bf):
    _cast_weights_at_sweep_start([(w_ref, w_bf)])
    o_ref[...] = _dot(a_ref[...], w_bf[...]).astype(BF16)


def _matmul(a, w, tm, tn, name):
    t, k = a.shape
    n = w.shape[1]
    cast_in_kernel = w.dtype == F32
    return pl.pallas_call(
        _matmul_f32_weights_kernel if cast_in_kernel else _matmul_kernel,
        grid=(n // tn, t // tm),
        in_specs=[pl.BlockSpec((tm, k), lambda j, i: (_serpentine(j, i, t // tm), 0)),
                  pl.BlockSpec((k, tn), lambda j, i: (0, j))],
        out_specs=pl.BlockSpec((tm, tn), lambda j, i: (_serpentine(j, i, t // tm), j)),
        out_shape=jax.ShapeDtypeStruct((t, n), BF16),
        scratch_shapes=[pltpu.VMEM((k, tn), BF16)] if cast_in_kernel else [],
        compiler_params=_params(("parallel", "arbitrary"), 52),
        name=name,
    )(a, w)


def _merge_kernel(a_ref, o_ref, wa_ref, wb_ref, ga_ref, gb_ref, out_ref):
    y_a = _dot(a_ref[...], wa_ref[...])
    y_b = _dot(o_ref[...], wb_ref[...])
    out_ref[...] = (ga_ref[...] * y_a + gb_ref[...] * y_b).astype(BF16)


def _merge(a, o, w_a, w_b, gates):
    t, d = a.shape
    n = w_a.shape[1]
    tm, tn = ROW_TILE_MM, 512
    nb = n // tn
    return pl.pallas_call(
        _merge_kernel,
        grid=(nb, t // tm),
        in_specs=[
            pl.BlockSpec((tm, d), lambda j, i: (_serpentine(j, i, t // tm), 0)),
            pl.BlockSpec((tm, d), lambda j, i: (_serpentine(j, i, t // tm), 0)),
            pl.BlockSpec((d, tn), lambda j, i: (0, j)),
            pl.BlockSpec((d, tn), lambda j, i: (0, j)),
            pl.BlockSpec((tm, tn), lambda j, i: (_serpentine(j, i, t // tm), j)),
            pl.BlockSpec((tm, tn), lambda j, i: (_serpentine(j, i, t // tm), j + nb)),
        ],
        out_specs=pl.BlockSpec((tm, tn), lambda j, i: (_serpentine(j, i, t // tm), j)),
        out_shape=jax.ShapeDtypeStruct((t, n), BF16),
        compiler_params=_params(("parallel", "arbitrary"), 60),
        name="merge",
    )(a, o, w_a, w_b, gates, gates)


def _ffn_up_kernel(h_ref, wg_ref, wv_ref, cwg_ref, cwv_ref, bg_ref, bv_ref, o_ref,
                   carry_g_ref, carry_v_ref, wg_bf, wv_bf, *, tiles_per_seq):
    _cast_weights_at_sweep_start([(wg_ref, wg_bf), (wv_ref, wv_bf)])
    _reset_carry_at_sequence_start([carry_g_ref, carry_v_ref], tiles_per_seq)
    h = h_ref[...]
    u_gate = _causal_conv3(_dot(h, wg_bf[...]), carry_g_ref, cwg_ref) + bg_ref[...]
    u_val = _causal_conv3(_dot(h, wv_bf[...]), carry_v_ref, cwv_ref) + bv_ref[...]
    half_gate = 0.5 * u_gate
    t = jnp.tanh(u_gate * (GELU_C0 + GELU_C1 * (u_gate * u_gate)))
    o_ref[...] = ((half_gate + half_gate * t) * u_val).astype(BF16)


def _ffn_up(h, w_up, conv_w, conv_b, seq):
    t, d = h.shape
    d_ff = w_up.shape[1] // 2
    tm, tn = ROW_TILE_MM, 256
    nb = d_ff // tn
    conv_b = conv_b.reshape(1, 2 * d_ff)
    return pl.pallas_call(
        functools.partial(_ffn_up_kernel, tiles_per_seq=seq // tm),
        grid=(nb, t // tm),
        in_specs=[
            pl.BlockSpec((tm, d), lambda j, i: (i, 0)),
            pl.BlockSpec((d, tn), lambda j, i: (0, j)),
            pl.BlockSpec((d, tn), lambda j, i: (0, j + nb)),
            pl.BlockSpec((SHORT_K, tn), lambda j, i: (0, j)),
            pl.BlockSpec((SHORT_K, tn), lambda j, i: (0, j + nb)),
            pl.BlockSpec((1, tn), lambda j, i: (0, j)),
            pl.BlockSpec((1, tn), lambda j, i: (0, j + nb)),
        ],
        out_specs=pl.BlockSpec((tm, tn), lambda j, i: (i, j)),
        out_shape=jax.ShapeDtypeStruct((t, d_ff), BF16),
        scratch_shapes=[pltpu.VMEM((CARRY_ROWS, tn), F32)] * 2 + [pltpu.VMEM((d, tn), BF16)] * 2,
        compiler_params=_params(("parallel", "arbitrary"), 48),
        name="ffn_up",
    )(h, w_up, w_up, conv_w, conv_w, conv_b, conv_b)


def _ple_kernel(h_ref, p_ref, wg_ref, wp_ref, o_ref, wg_bf, wp_bf):
    _cast_weights_at_sweep_start([(wg_ref, wg_bf), (wp_ref, wp_bf)])
    gate = _sigmoid(_dot(h_ref[...], wg_bf[...]))
    o_ref[...] = (gate * _dot(p_ref[...].astype(BF16), wp_bf[...])).astype(BF16)


def _ple(h, p, w_gate, w_proj):
    t, d = h.shape
    n = w_gate.shape[1]
    ple = p.shape[1]
    tm, tn = ROW_TILE_MM, 512
    return pl.pallas_call(
        _ple_kernel,
        grid=(n // tn, t // tm),
        in_specs=[
            pl.BlockSpec((tm, d), lambda j, i: (_serpentine(j, i, t // tm), 0)),
            pl.BlockSpec((tm, ple), lambda j, i: (_serpentine(j, i, t // tm), 0)),
            pl.BlockSpec((d, tn), lambda j, i: (0, j)),
            pl.BlockSpec((ple, tn), lambda j, i: (0, j)),
        ],
        out_specs=pl.BlockSpec((tm, tn), lambda j, i: (_serpentine(j, i, t // tm), j)),
        out_shape=jax.ShapeDtypeStruct((t, n), BF16),
        scratch_shapes=[pltpu.VMEM((d, tn), BF16), pltpu.VMEM((ple, tn), BF16)],
        compiler_params=_params(("parallel", "arbitrary"), 48),
        name="ple",
    )(h, p, w_gate, w_proj)


def _split_bf16x3(x):
    hi = x.astype(BF16)
    rest = x - hi.astype(F32)
    mid = rest.astype(BF16)
    lo = (rest - mid.astype(F32)).astype(BF16)
    return hi, mid, lo


def _forget_cumsum_kernel(h_ref, wf_ref, b_ref, spread_ref, ones_ref, c_ref, augk_ref, carry_ref,
                          wf_bf, *, tiles_per_seq):
    @pl.when(pl.program_id(0) == 0)
    def _():
        wf_bf[...] = wf_ref[...].astype(BF16)

    @pl.when(pl.program_id(0) % tiles_per_seq == 0)
    def _():
        carry_ref[...] = jnp.zeros_like(carry_ref)

    ts = h_ref.shape[0]
    nh = b_ref.shape[1]
    log_f = jax.nn.log_sigmoid(_dot_nt(h_ref[...], wf_bf[...])[:, 0:nh] + b_ref[...])
    row = lax.broadcasted_iota(jnp.int32, (ts, ts), 0)
    col = lax.broadcasted_iota(jnp.int32, (ts, ts), 1)
    lower = (col <= row).astype(F32)
    csum = jnp.dot(lower, log_f, preferred_element_type=F32,
                   precision=lax.Precision.HIGHEST) + carry_ref[...]
    c_ref[...] = csum
    carry_ref[...] = c_ref[ts - 1:ts, :]

    hi, mid, lo = _split_bf16x3(csum * LOG2_E)
    aug = (_dot(hi, spread_ref[0]) + _dot(mid, spread_ref[1]) + _dot(lo, spread_ref[2])
           + ones_ref[...])
    augk_ref[...] = aug.astype(BF16)


def _forget_cumsum(h, w_in_t, row_start, bias, seq):
    t, d = h.shape
    nh = bias.shape[0]
    ts = CUMSUM_TILE
    assert row_start % LANES == 0 and nh <= LANES
    width = nh * HEAD_DIM
    lane = jnp.arange(width)[None, :]
    head = jnp.arange(nh)[:, None]
    spread = jnp.stack([jnp.where(lane == head * HEAD_DIM + j, -1.0, 0.0) for j in range(3)])
    ones = jnp.where((lane % HEAD_DIM >= 3) & (lane % HEAD_DIM < 6), 1.0, 0.0).astype(F32)
    return pl.pallas_call(
        functools.partial(_forget_cumsum_kernel, tiles_per_seq=seq // ts),
        grid=(t // ts,),
        in_specs=[pl.BlockSpec((ts, d), lambda i: (i, 0)),
                  pl.BlockSpec((LANES, d), lambda i: (row_start // LANES, 0)),
                  pl.BlockSpec((1, nh), lambda i: (0, 0)),
                  pl.BlockSpec((3, nh, width), lambda i: (0, 0, 0)),
                  pl.BlockSpec((1, width), lambda i: (0, 0))],
        out_specs=[pl.BlockSpec((ts, nh), lambda i: (i, 0)),
                   pl.BlockSpec((ts, width), lambda i: (i, 0))],
        out_shape=[jax.ShapeDtypeStruct((t, nh), F32), jax.ShapeDtypeStruct((t, width), BF16)],
        scratch_shapes=[pltpu.VMEM((1, nh), F32), pltpu.VMEM((LANES, d), BF16)],
        compiler_params=_params(("arbitrary",), 48),
        name="forget_cumsum",
    )(h, w_in_t, bias.reshape(1, nh), spread.astype(BF16), ones)


AUG_ROWS = 16
SUM_ROWS = 16
HEADS_PER_STEP = 2


def _attention_kernel(q_ref, k_ref, v_ref, augk_ref, c_ref, o_ref,
                      kx_ref, vt_ref, qxt_ref, m_ref, *buffers, tile):
    n = HEADS_PER_STEP
    buffers = [buf.at[:, 0:tile] for buf in buffers]
    s_ref = [buffers[2 * g:2 * g + 2] for g in range(n)]
    p_ref = [buffers[2 * n + 2 * g:2 * n + 2 * g + 2] for g in range(n)]
    acc_ref = buffers[4 * n:]
    kt = tile // 2
    qi = pl.program_id(2)
    n_key_tiles = kx_ref.shape[1]
    heads = range(HEADS_PER_STEP)

    def lanes(g):
        return slice(g * HEAD_DIM, (g + 1) * HEAD_DIM)

    @pl.when(qi == 0)
    def _():
        def stage(j, _):
            rows = pl.ds(pl.multiple_of(j * kt, kt), kt)
            for g in heads:
                kx_ref[g, j, :, 0:HEAD_DIM] = k_ref[rows, lanes(g)]
                kx_ref[g, j, :, HEAD_DIM:] = augk_ref[rows, lanes(g)]
                vt_ref[g, j, 0:HEAD_DIM, :] = v_ref[rows, lanes(g)].astype(F32).T.astype(BF16)
                vt_ref[g, j, HEAD_DIM:, :] = jnp.ones((SUM_ROWS, kt), BF16)
            return 0
        lax.fori_loop(0, n_key_tiles, stage, 0)
        for g in heads:
            qxt_ref[g, HEAD_DIM + AUG_ROWS:, :] = jnp.zeros((HEAD_DIM - AUG_ROWS, tile), BF16)

    r = lax.broadcasted_iota(jnp.int32, (AUG_ROWS, tile), 0)
    for g in heads:
        qxt_ref[g, 0:HEAD_DIM, :] = q_ref[:, lanes(g)].astype(F32).T.astype(BF16)
        hi, mid, lo = _split_bf16x3(c_ref[g, qi] * LOG2_E)
        aug = jnp.where(r < 3, 1.0,
                        jnp.where(r == 3, hi.astype(F32),
                                  jnp.where(r == 4, mid.astype(F32),
                                            jnp.where(r == 5, lo.astype(F32), 0.0))))
        qxt_ref[g, HEAD_DIM:HEAD_DIM + AUG_ROWS, :] = aug.astype(BF16)

    def scores_into(g, slot, ki):
        s = _dot(kx_ref[g, ki], qxt_ref[g])
        s_ref[g][slot][...] = s
        return jnp.max(s, axis=0, keepdims=True)

    def softmax_into(g, slot, m, tile_max):
        m_new = jnp.maximum(m, tile_max)
        p_ref[g][slot][...] = jnp.exp2(s_ref[g][slot][...] - m_new).astype(BF16)
        return m_new, jnp.exp2(m - m_new)

    def accumulate(g, slot, alpha, ki):
        acc_ref[g][...] = alpha * acc_ref[g][...] + _dot(vt_ref[g, ki], p_ref[g][slot][...])

    for g in heads:
        p_ref[g][1][...] = jnp.zeros((kt, tile), BF16)
        acc_ref[g][...] = jnp.zeros(acc_ref[g].shape, F32)
    max_0 = tuple(scores_into(g, 0, 0) for g in heads)

    def pair(jj, carry):
        m, alpha_1, max_0 = carry
        k0 = 2 * jj
        max_1 = [scores_into(g, 1, k0 + 1) for g in heads]
        m, alpha_0 = zip(*[softmax_into(g, 0, m[g], max_0[g]) for g in heads])
        for g in heads:
            accumulate(g, 1, alpha_1[g], jnp.maximum(k0 - 1, 0))
        max_0 = tuple(scores_into(g, 0, k0 + 2) for g in heads)
        m, alpha_1 = zip(*[softmax_into(g, 1, m[g], max_1[g]) for g in heads])
        for g in heads:
            accumulate(g, 0, alpha_0[g], k0)
        return m, alpha_1, max_0

    init = (tuple(jnp.full((1, tile), NEG_INF, F32) for _ in heads),
            tuple(jnp.ones((1, tile), F32) for _ in heads), max_0)
    m, alpha_1, _ = lax.fori_loop(0, qi, pair, init)

    k0 = 2 * qi
    key = lax.broadcasted_iota(jnp.int32, (kt, tile), 0)
    query = lax.broadcasted_iota(jnp.int32, (kt, tile), 1)
    upper = slice(kt, tile)

    def upper_scores_into(g):
        s = _dot(kx_ref[g, k0 + 1], qxt_ref[g, :, upper])
        s = jnp.where(lax.broadcasted_iota(jnp.int32, (kt, kt), 0)
                      <= lax.broadcasted_iota(jnp.int32, (kt, kt), 1), s, NEG_INF)
        s_ref[g][1][:, 0:kt] = s
        return jnp.max(s, axis=0, keepdims=True)

    max_1 = [upper_scores_into(g) for g in heads]
    for g in heads:
        s_ref[g][0][...] = jnp.where(key <= query, s_ref[g][0][...], NEG_INF)
    max_0 = [jnp.max(s_ref[g][0][...], axis=0, keepdims=True) for g in heads]
    m, alpha_0 = zip(*[softmax_into(g, 0, m[g], max_0[g]) for g in heads])
    for g in heads:
        accumulate(g, 1, alpha_1[g], jnp.maximum(k0 - 1, 0))
    alpha_upper = []
    for g in heads:
        m_ref[g, 0:1, :] = m[g]
        m_old = m_ref[g, 0:1, upper]
        m_upper = jnp.maximum(m_old, max_1[g])
        alpha_upper.append(jnp.exp2(m_old - m_upper))
        p_ref[g][1][:, 0:kt] = jnp.exp2(s_ref[g][1][:, 0:kt] - m_upper).astype(BF16)
    for g in heads:
        accumulate(g, 0, alpha_0[g], k0)
    for g in heads:
        acc_ref[g][:, upper] = (alpha_upper[g] * acc_ref[g][:, upper]
                                + _dot(vt_ref[g, k0 + 1], p_ref[g][1][:, 0:kt]))
    for g in heads:
        o_t = acc_ref[g][0:HEAD_DIM, :] / acc_ref[g][HEAD_DIM:HEAD_DIM + 1, :]
        o_ref[:, lanes(g)] = o_t.T.astype(BF16)


def _attention(qkv, augk, c, batch, seq):
    t = qkv.shape[0]
    tile = ATTN_TILE
    kt = tile // 2
    nq = seq // tile
    g = HEADS_PER_STEP
    width = g * HEAD_DIM
    groups = N_HEADS // g
    c_rows = jnp.transpose(c.reshape(batch, seq, N_HEADS), (0, 2, 1)).reshape(
        batch * N_HEADS, nq, 1, tile)
    return pl.pallas_call(
        functools.partial(_attention_kernel, tile=tile),
        grid=(batch, groups, nq),
        in_specs=[
            pl.BlockSpec((tile, width), lambda b, h, i: (b * nq + i, h)),
            pl.BlockSpec((seq, width), lambda b, h, i: (b, groups + h)),
            pl.BlockSpec((seq, width), lambda b, h, i: (b, 2 * groups + h)),
            pl.BlockSpec((seq, width), lambda b, h, i: (b, h)),
            pl.BlockSpec((g, nq, 1, tile), lambda b, h, i: (b * groups + h, 0, 0, 0)),
        ],
        out_specs=pl.BlockSpec((tile, width), lambda b, h, i: (b * nq + i, h)),
        out_shape=jax.ShapeDtypeStruct((t, N_HEADS * HEAD_DIM), BF16),
        scratch_shapes=[pltpu.VMEM((g, seq // kt, kt, 2 * HEAD_DIM), BF16),
                        pltpu.VMEM((g, seq // kt, HEAD_DIM + SUM_ROWS, kt), BF16),
                        pltpu.VMEM((g, 2 * HEAD_DIM, tile), BF16),
                        pltpu.VMEM((g, 8, tile), F32),
                        *[pltpu.VMEM((kt, tile + LANES), F32)] * (2 * g),
                        *[pltpu.VMEM((kt, tile + LANES), BF16)] * (2 * g),
                        *[pltpu.VMEM((HEAD_DIM + SUM_ROWS, tile + LANES), F32)] * g],
        compiler_params=_params(("parallel", "parallel", "arbitrary"), 60),
        name="fox_attention",
    )(qkv, qkv, qkv, augk, c_rows)


def kernel(x, p, norm_mix_pre, w_in, forget_bias, conv_mix_w, w_branch_conv, w_branch_attn, w_out,
           norm_mix_post, norm_ffn_pre, w_up, ffn_conv_w, ffn_conv_b, w_down, norm_ffn_post,
           w_ple_proj, norm_ple_gate, w_ple_gate, norm_ple_post):
    batch, seq, d = x.shape
    t = batch * seq
    depth = w_in.shape[0]
    conv_width = conv_mix_w.shape[-1]
    attn_width = N_HEADS * HEAD_DIM
    qkv_start = 3 * conv_width
    forget_start = qkv_start + 3 * attn_width
    gate_start = forget_start + N_HEADS

    xs = x.reshape(t, d)
    for i in range(depth):
        w_in_t = jnp.transpose(w_in[i])

        h1 = _norm(xs, norm_mix_pre[i])
        a = _conv_branch(h1, w_in_t, conv_mix_w[i], seq)
        qkv = _qkv(h1, w_in_t, qkv_start, 3 * attn_width)
        gates = _gates(h1, w_in_t, gate_start, 2 * d)
        c, augk = _forget_cumsum(h1, w_in_t, forget_start, forget_bias[i], seq)
        o = _attention(qkv, augk, c, batch, seq)
        merged = _merge(a, o, w_branch_conv[i].astype(BF16), w_branch_attn[i].astype(BF16), gates)
        y1 = _matmul(merged, w_out[i], ROW_TILE_MM, 512, "out_proj")
        x1, h2 = _residual_norm(xs, y1, norm_mix_post[i], norm_ffn_pre[i])

        ffn = _ffn_up(h2, w_up[i], ffn_conv_w[i], ffn_conv_b[i], seq)
        y2 = _matmul(ffn, w_down[i].astype(BF16), ROW_TILE_WIDE_K, 512, "ffn_down")
        x2, h3 = _residual_norm(x1, y2, norm_ffn_post[i], norm_ple_gate[i])

        ge = _ple(h3, p[i].reshape(t, -1), w_ple_gate[i], w_ple_proj[i])
        xs = _residual(x2, ge, norm_ple_post[i])
    return xs.reshape(batch, seq, d)
```

```python
import functools
import math

import jax
import jax.numpy as jnp
from jax import lax
from jax.experimental import pallas as pl
from jax.experimental.pallas import tpu as pltpu

BF16 = jnp.bfloat16
F32 = jnp.float32

LANES = 128
BF16_SUBLANES = 16
N_HEADS = 32
HEAD_DIM = 128
SHORT_K = 3
EPS = 1e-6
NEG_INF = -1e30
LOG2_E = math.log2(math.e)
GELU_C0 = math.sqrt(2.0 / math.pi)
GELU_C1 = GELU_C0 * 0.044715
MIB = 1024 * 1024

ROW_TILE_NORM = 256
ROW_TILE_MM = 1024
ROW_TILE_WIDE_K = 512
ATTN_TILE = 1024
CUMSUM_TILE = 512
CARRY_ROWS = 8


def _params(semantics, vmem_mib):
    return pltpu.CompilerParams(dimension_semantics=semantics, vmem_limit_bytes=vmem_mib * MIB)


def _dot(a, b):
    return jnp.dot(a, b, preferred_element_type=F32)


def _dot_nt(a, b):
    return lax.dot_general(a, b, (((1,), (1,)), ((), ())), preferred_element_type=F32)


def _sigmoid(x):
    return 0.5 * jnp.tanh(0.5 * x) + 0.5


def _rms_scale(x, g):
    inv = lax.rsqrt(jnp.mean(x * x, axis=-1, keepdims=True) + EPS)
    return x * inv * g


def _norm_kernel(x_ref, g_ref, h_ref):
    h_ref[...] = _rms_scale(x_ref[...], g_ref[...]).astype(BF16)


def _residual_norm_kernel(x_ref, y_ref, g_post_ref, g_pre_ref, x_out_ref, h_ref):
    x_new = x_ref[...] + _rms_scale(y_ref[...].astype(F32), g_post_ref[...])
    x_out_ref[...] = x_new
    h_ref[...] = _rms_scale(x_new, g_pre_ref[...]).astype(BF16)


def _residual_kernel(x_ref, y_ref, g_post_ref, x_out_ref):
    x_out_ref[...] = x_ref[...] + _rms_scale(y_ref[...].astype(F32), g_post_ref[...])


def _row_spec(tr, d):
    return pl.BlockSpec((tr, d), lambda i: (i, 0))


def _gain_spec(d):
    return pl.BlockSpec((1, d), lambda i: (0, 0))


def _norm(x, g):
    t, d = x.shape
    tr = ROW_TILE_NORM
    return pl.pallas_call(
        _norm_kernel,
        grid=(t // tr,),
        in_specs=[_row_spec(tr, d), _gain_spec(d)],
        out_specs=_row_spec(tr, d),
        out_shape=jax.ShapeDtypeStruct((t, d), BF16),
        compiler_params=_params(("parallel",), 40),
        name="norm",
    )(x, g.reshape(1, d))


def _residual_norm(x, y, g_post, g_pre):
    t, d = x.shape
    tr = ROW_TILE_NORM
    return pl.pallas_call(
        _residual_norm_kernel,
        grid=(t // tr,),
        in_specs=[_row_spec(tr, d), _row_spec(tr, d), _gain_spec(d), _gain_spec(d)],
        out_specs=[_row_spec(tr, d), _row_spec(tr, d)],
        out_shape=[jax.ShapeDtypeStruct((t, d), F32), jax.ShapeDtypeStruct((t, d), BF16)],
        compiler_params=_params(("parallel",), 40),
        name="residual_norm",
    )(x, y, g_post.reshape(1, d), g_pre.reshape(1, d))


def _residual(x, y, g_post):
    t, d = x.shape
    tr = ROW_TILE_NORM
    return pl.pallas_call(
        _residual_kernel,
        grid=(t // tr,),
        in_specs=[_row_spec(tr, d), _row_spec(tr, d), _gain_spec(d)],
        out_specs=_row_spec(tr, d),
        out_shape=jax.ShapeDtypeStruct((t, d), F32),
        compiler_params=_params(("parallel",), 40),
        name="residual",
    )(x, y, g_post.reshape(1, d))


def _causal_conv3(u, carry_ref, w_ref):
    tm = u.shape[0]
    w0, w1, w2 = w_ref[0:1, :], w_ref[1:2, :], w_ref[2:3, :]
    y = w0 * pltpu.roll(u, 2, 0) + w1 * pltpu.roll(u, 1, 0) + w2 * u
    head = u[0:CARRY_ROWS, :]
    prev = carry_ref[...]
    r = lax.broadcasted_iota(jnp.int32, head.shape, 0)
    back1 = jnp.where(r < 1, pltpu.roll(prev, 1, 0), pltpu.roll(head, 1, 0))
    back2 = jnp.where(r < 2, pltpu.roll(prev, 2, 0), pltpu.roll(head, 2, 0))
    y_head = w0 * back2 + w1 * back1 + w2 * head
    carry_ref[...] = u[tm - CARRY_ROWS:, :]
    return jnp.concatenate([y_head, y[CARRY_ROWS:, :]], axis=0)


def _reset_carry_at_sequence_start(carry_refs, tiles_per_seq):
    @pl.when(pl.program_id(1) % tiles_per_seq == 0)
    def _():
        for ref in carry_refs:
            ref[...] = jnp.zeros_like(ref)


def _serpentine(j, i, n_tiles):
    return jnp.where(j % 2 == 0, i, n_tiles - 1 - i)


def _cast_weights_at_sweep_start(pairs):
    @pl.when(pl.program_id(1) == 0)
    def _():
        for w_ref, w_bf16_ref in pairs:
            w_bf16_ref[...] = w_ref[...].astype(BF16)


def _conv_branch_kernel(h_ref, wb_ref, wc_ref, wv_ref, cw_ref, o_ref, carry_ref,
                        wb_bf, wc_bf, wv_bf, *, tiles_per_seq):
    _cast_weights_at_sweep_start([(wb_ref, wb_bf), (wc_ref, wc_bf), (wv_ref, wv_bf)])
    _reset_carry_at_sequence_start([carry_ref], tiles_per_seq)
    h = h_ref[...]
    u = _dot_nt(h, wc_bf[...]) * _dot_nt(h, wv_bf[...])
    y = _causal_conv3(u, carry_ref, cw_ref)
    o_ref[...] = (_dot_nt(h, wb_bf[...]) * y).astype(BF16)


def _conv_branch(h, w_in_t, conv_w, seq):
    t, d = h.shape
    width = conv_w.shape[1]
    tm, tn = ROW_TILE_MM, 256
    nb = width // tn
    return pl.pallas_call(
        functools.partial(_conv_branch_kernel, tiles_per_seq=seq // tm),
        grid=(nb, t // tm),
        in_specs=[
            pl.BlockSpec((tm, d), lambda j, i: (i, 0)),
            pl.BlockSpec((tn, d), lambda j, i: (j, 0)),
            pl.BlockSpec((tn, d), lambda j, i: (j + nb, 0)),
            pl.BlockSpec((tn, d), lambda j, i: (j + 2 * nb, 0)),
            pl.BlockSpec((SHORT_K, tn), lambda j, i: (0, j)),
        ],
        out_specs=pl.BlockSpec((tm, tn), lambda j, i: (i, j)),
        out_shape=jax.ShapeDtypeStruct((t, width), BF16),
        scratch_shapes=[pltpu.VMEM((CARRY_ROWS, tn), F32)] + [pltpu.VMEM((tn, d), BF16)] * 3,
        compiler_params=_params(("parallel", "arbitrary"), 60),
        name="conv_branch",
    )(h, w_in_t, w_in_t, w_in_t, conv_w)


def _qkv_kernel(h_ref, w_ref, o_ref, w_bf, *, q_tiles, scale):
    _cast_weights_at_sweep_start([(w_ref, w_bf)])
    acc = _dot_nt(h_ref[...], w_bf[...])
    factor = jnp.where(pl.program_id(0) < q_tiles, scale, 1.0).astype(F32)
    o_ref[...] = (acc * factor).astype(BF16)


def _qkv(h, w_in_t, row_start, n):
    t, d = h.shape
    tm, tn = ROW_TILE_MM, 512
    first = row_start // tn
    kern = functools.partial(_qkv_kernel, q_tiles=(n // 3) // tn, scale=LOG2_E / math.sqrt(HEAD_DIM))
    return pl.pallas_call(
        kern,
        grid=(n // tn, t // tm),
        in_specs=[pl.BlockSpec((tm, d), lambda j, i: (_serpentine(j, i, t // tm), 0)),
                  pl.BlockSpec((tn, d), lambda j, i: (first + j, 0))],
        out_specs=pl.BlockSpec((tm, tn), lambda j, i: (_serpentine(j, i, t // tm), j)),
        out_shape=jax.ShapeDtypeStruct((t, n), BF16),
        scratch_shapes=[pltpu.VMEM((tn, d), BF16)],
        compiler_params=_params(("parallel", "arbitrary"), 48),
        name="qkv",
    )(h, w_in_t)


def _gates_kernel(h_ref, w_ref, w_next_ref, o_ref, w_bf, *, shift):
    @pl.when(pl.program_id(1) == 0)
    def _():
        tn = w_bf.shape[0]
        w_bf[0:tn - shift, :] = w_ref[shift:tn, :].astype(BF16)
        w_bf[tn - shift:tn, :] = w_next_ref[0:shift, :].astype(BF16)

    o_ref[...] = _sigmoid(_dot_nt(h_ref[...], w_bf[...])).astype(BF16)


def _gates(h, w_in_t, row_start, n):
    t, d = h.shape
    tm, tn = ROW_TILE_MM, 512
    shift = row_start % LANES
    aligned = row_start - shift
    assert 0 < shift and shift % BF16_SUBLANES == 0 and aligned % tn == 0
    return pl.pallas_call(
        functools.partial(_gates_kernel, shift=shift),
        grid=(n // tn, t // tm),
        in_specs=[pl.BlockSpec((tm, d), lambda j, i: (_serpentine(j, i, t // tm), 0)),
                  pl.BlockSpec((tn, d), lambda j, i: (aligned // tn + j, 0)),
                  pl.BlockSpec((LANES, d), lambda j, i: ((aligned + (j + 1) * tn) // LANES, 0))],
        out_specs=pl.BlockSpec((tm, tn), lambda j, i: (_serpentine(j, i, t // tm), j)),
        out_shape=jax.ShapeDtypeStruct((t, n), BF16),
        scratch_shapes=[pltpu.VMEM((tn, d), BF16)],
        compiler_params=_params(("parallel", "arbitrary"), 48),
        name="gates",
    )(h, w_in_t, w_in_t)


def _matmul_kernel(a_ref, w_ref, o_ref):
    o_ref[...] = _dot(a_ref[...], w_ref[...]).astype(BF16)


def _matmul_f32_weights_kernel(a_ref, w_ref, o_ref, w_bf):
    _cast_weights_at_sweep_start([(w_ref, w_bf)])
    o_ref[...] = _dot(a_ref[...], w_bf[...]).astype(BF16)


def _matmul(a, w, tm, tn, name):
    t, k = a.shape
    n = w.shape[1]
    cast_in_kernel = w.dtype == F32
    return pl.pallas_call(
        _matmul_f32_weights_kernel if cast_in_kernel else _matmul_kernel,
        grid=(n // tn, t // tm),
        in_specs=[pl.BlockSpec((tm, k), lambda j, i: (_serpentine(j, i, t // tm), 0)),
                  pl.BlockSpec((k, tn), lambda j, i: (0, j))],
        out_specs=pl.BlockSpec((tm, tn), lambda j, i: (_serpentine(j, i, t // tm), j)),
        out_shape=jax.ShapeDtypeStruct((t, n), BF16),
        scratch_shapes=[pltpu.VMEM((k, tn), BF16)] if cast_in_kernel else [],
        compiler_params=_params(("parallel", "arbitrary"), 52),
        name=name,
    )(a, w)


def _merge_kernel(a_ref, o_ref, wa_ref, wb_ref, ga_ref, gb_ref, out_ref):
    y_a = _dot(a_ref[...], wa_ref[...])
    y_b = _dot(o_ref[...], wb_ref[...])
    out_ref[...] = (ga_ref[...] * y_a + gb_ref[...] * y_b).astype(BF16)


def _merge(a, o, w_a, w_b, gates):
    t, d = a.shape
    n = w_a.shape[1]
    tm, tn = ROW_TILE_MM, 512
    nb = n // tn
    return pl.pallas_call(
        _merge_kernel,
        grid=(nb, t // tm),
        in_specs=[
            pl.BlockSpec((tm, d), lambda j, i: (_serpentine(j, i, t // tm), 0)),
            pl.BlockSpec((tm, d), lambda j, i: (_serpentine(j, i, t // tm), 0)),
            pl.BlockSpec((d, tn), lambda j, i: (0, j)),
            pl.BlockSpec((d, tn), lambda j, i: (0, j)),
            pl.BlockSpec((tm, tn), lambda j, i: (_serpentine(j, i, t // tm), j)),
            pl.BlockSpec((tm, tn), lambda j, i: (_serpentine(j, i, t // tm), j + nb)),
        ],
        out_specs=pl.BlockSpec((tm, tn), lambda j, i: (_serpentine(j, i, t // tm), j)),
        out_shape=jax.ShapeDtypeStruct((t, n), BF16),
        compiler_params=_params(("parallel", "arbitrary"), 60),
        name="merge",
    )(a, o, w_a, w_b, gates, gates)


def _ffn_up_kernel(h_ref, wg_ref, wv_ref, cwg_ref, cwv_ref, bg_ref, bv_ref, o_ref,
                   carry_g_ref, carry_v_ref, wg_bf, wv_bf, *, tiles_per_seq):
    _cast_weights_at_sweep_start([(wg_ref, wg_bf), (wv_ref, wv_bf)])
    _reset_carry_at_sequence_start([carry_g_ref, carry_v_ref], tiles_per_seq)
    h = h_ref[...]
    u_gate = _causal_conv3(_dot(h, wg_bf[...]), carry_g_ref, cwg_ref) + bg_ref[...]
    u_val = _causal_conv3(_dot(h, wv_bf[...]), carry_v_ref, cwv_ref) + bv_ref[...]
    half_gate = 0.5 * u_gate
    t = jnp.tanh(u_gate * (GELU_C0 + GELU_C1 * (u_gate * u_gate)))
    o_ref[...] = ((half_gate + half_gate * t) * u_val).astype(BF16)


def _ffn_up(h, w_up, conv_w, conv_b, seq):
    t, d = h.shape
    d_ff = w_up.shape[1] // 2
    tm, tn = ROW_TILE_MM, 256
    nb = d_ff // tn
    conv_b = conv_b.reshape(1, 2 * d_ff)
    return pl.pallas_call(
        functools.partial(_ffn_up_kernel, tiles_per_seq=seq // tm),
        grid=(nb, t // tm),
        in_specs=[
            pl.BlockSpec((tm, d), lambda j, i: (i, 0)),
            pl.BlockSpec((d, tn), lambda j, i: (0, j)),
            pl.BlockSpec((d, tn), lambda j, i: (0, j + nb)),
            pl.BlockSpec((SHORT_K, tn), lambda j, i: (0, j)),
            pl.BlockSpec((SHORT_K, tn), lambda j, i: (0, j + nb)),
            pl.BlockSpec((1, tn), lambda j, i: (0, j)),
            pl.BlockSpec((1, tn), lambda j, i: (0, j + nb)),
        ],
        out_specs=pl.BlockSpec((tm, tn), lambda j, i: (i, j)),
        out_shape=jax.ShapeDtypeStruct((t, d_ff), BF16),
        scratch_shapes=[pltpu.VMEM((CARRY_ROWS, tn), F32)] * 2 + [pltpu.VMEM((d, tn), BF16)] * 2,
        compiler_params=_params(("parallel", "arbitrary"), 48),
        name="ffn_up",
    )(h, w_up, w_up, conv_w, conv_w, conv_b, conv_b)


def _ple_kernel(h_ref, p_ref, wg_ref, wp_ref, o_ref, wg_bf, wp_bf):
    _cast_weights_at_sweep_start([(wg_ref, wg_bf), (wp_ref, wp_bf)])
    gate = _sigmoid(_dot(h_ref[...], wg_bf[...]))
    o_ref[...] = (gate * _dot(p_ref[...].astype(BF16), wp_bf[...])).astype(BF16)


def _ple(h, p, w_gate, w_proj):
    t, d = h.shape
    n = w_gate.shape[1]
    ple = p.shape[1]
    tm, tn = ROW_TILE_MM, 512
    return pl.pallas_call(
        _ple_kernel,
        grid=(n // tn, t // tm),
        in_specs=[
            pl.BlockSpec((tm, d), lambda j, i: (_serpentine(j, i, t // tm), 0)),
            pl.BlockSpec((tm, ple), lambda j, i: (_serpentine(j, i, t // tm), 0)),
            pl.BlockSpec((d, tn), lambda j, i: (0, j)),
            pl.BlockSpec((ple, tn), lambda j, i: (0, j)),
        ],
        out_specs=pl.BlockSpec((tm, tn), lambda j, i: (_serpentine(j, i, t // tm), j)),
        out_shape=jax.ShapeDtypeStruct((t, n), BF16),
        scratch_shapes=[pltpu.VMEM((d, tn), BF16), pltpu.VMEM((ple, tn), BF16)],
        compiler_params=_params(("parallel", "arbitrary"), 48),
        name="ple",
    )(h, p, w_gate, w_proj)


def _split_bf16x3(x):
    hi = x.astype(BF16)
    rest = x - hi.astype(F32)
    mid = rest.astype(BF16)
    lo = (rest - mid.astype(F32)).astype(BF16)
    return hi, mid, lo


def _forget_cumsum_kernel(h_ref, wf_ref, b_ref, spread_ref, ones_ref, c_ref, augk_ref, carry_ref,
                          wf_bf, *, tiles_per_seq):
    @pl.when(pl.program_id(0) == 0)
    def _():
        wf_bf[...] = wf_ref[...].astype(BF16)

    @pl.when(pl.program_id(0) % tiles_per_seq == 0)
    def _():
        carry_ref[...] = jnp.zeros_like(carry_ref)

    ts = h_ref.shape[0]
    nh = c_ref.shape[1]
    lane = lax.broadcasted_iota(jnp.int32, (ts, LANES), 1)
    log_f = jnp.where(lane < nh, jax.nn.log_sigmoid(_dot_nt(h_ref[...], wf_bf[...]) + b_ref[...]), 0.0)
    row = lax.broadcasted_iota(jnp.int32, (ts, ts), 0)
    col = lax.broadcasted_iota(jnp.int32, (ts, ts), 1)
    lower = (col <= row).astype(F32)
    csum = jnp.dot(lower, log_f, preferred_element_type=F32,
                   precision=lax.Precision.HIGHEST) + carry_ref[...]
    c_ref[...] = csum[:, 0:nh]
    carry_ref[...] = csum[ts - 1:ts, :]

    c3 = jnp.where(lane < nh, csum,
                   jnp.where(lane < 2 * nh, pltpu.roll(csum, nh, 1), pltpu.roll(csum, 2 * nh, 1)))
    hi, mid, lo = _split_bf16x3(c3 * LOG2_E)
    pieces = jnp.where(lane < nh, hi.astype(F32),
                       jnp.where(lane < 2 * nh, mid.astype(F32), lo.astype(F32))).astype(BF16)
    augk_ref[...] = (_dot(pieces, spread_ref[...]) + ones_ref[...]).astype(BF16)


def _forget_cumsum(h, w_in_t, row_start, bias, seq):
    t, d = h.shape
    nh = bias.shape[0]
    ts = CUMSUM_TILE
    assert row_start % LANES == 0 and 3 * nh <= LANES
    width = nh * HEAD_DIM
    lane = jnp.arange(width)[None, :]
    piece_row = jnp.arange(LANES)[:, None]
    spread = jnp.where((piece_row < 3 * nh)
                       & (lane == (piece_row % nh) * HEAD_DIM + piece_row // nh), -1.0, 0.0)
    ones = jnp.where((lane % HEAD_DIM >= 3) & (lane % HEAD_DIM < 6), 1.0, 0.0).astype(F32)
    bias_tile = jnp.zeros((1, LANES), F32).at[0, 0:nh].set(bias)
    return pl.pallas_call(
        functools.partial(_forget_cumsum_kernel, tiles_per_seq=seq // ts),
        grid=(t // ts,),
        in_specs=[pl.BlockSpec((ts, d), lambda i: (i, 0)),
                  pl.BlockSpec((LANES, d), lambda i: (row_start // LANES, 0)),
                  pl.BlockSpec((1, LANES), lambda i: (0, 0)),
                  pl.BlockSpec((LANES, width), lambda i: (0, 0)),
                  pl.BlockSpec((1, width), lambda i: (0, 0))],
        out_specs=[pl.BlockSpec((ts, nh), lambda i: (i, 0)),
                   pl.BlockSpec((ts, width), lambda i: (i, 0))],
        out_shape=[jax.ShapeDtypeStruct((t, nh), F32), jax.ShapeDtypeStruct((t, width), BF16)],
        scratch_shapes=[pltpu.VMEM((1, LANES), F32), pltpu.VMEM((LANES, d), BF16)],
        compiler_params=_params(("arbitrary",), 48),
        name="forget_cumsum",
    )(h, w_in_t, bias_tile, spread.astype(BF16), ones)


AUG_ROWS = 16
SUM_ROWS = 16
HEADS_PER_STEP = 2


def _attention_kernel(q_ref, k_ref, v_ref, augk_ref, c_ref, o_ref,
                      kx_ref, vt_ref, qxt_ref, m_ref, *buffers, tile):
    n = HEADS_PER_STEP
    buffers = [buf.at[:, 0:tile] for buf in buffers]
    s_ref = [buffers[2 * g:2 * g + 2] for g in range(n)]
    p_ref = [buffers[2 * n + 2 * g:2 * n + 2 * g + 2] for g in range(n)]
    acc_ref = buffers[4 * n:]
    kt = tile // 2
    qi = pl.program_id(2)
    n_key_tiles = kx_ref.shape[1]
    heads = range(HEADS_PER_STEP)

    def lanes(g):
        return slice(g * HEAD_DIM, (g + 1) * HEAD_DIM)

    @pl.when(qi == 0)
    def _():
        def stage(j, _):
            rows = pl.ds(pl.multiple_of(j * kt, kt), kt)
            for g in heads:
                kx_ref[g, j, :, 0:HEAD_DIM] = k_ref[rows, lanes(g)]
                kx_ref[g, j, :, HEAD_DIM:] = augk_ref[rows, lanes(g)]
                vt_ref[g, j, 0:HEAD_DIM, :] = v_ref[rows, lanes(g)].astype(F32).T.astype(BF16)
                vt_ref[g, j, HEAD_DIM:, :] = jnp.ones((SUM_ROWS, kt), BF16)
            return 0
        lax.fori_loop(0, n_key_tiles, stage, 0)
        for g in heads:
            qxt_ref[g, HEAD_DIM + AUG_ROWS:, :] = jnp.zeros((HEAD_DIM - AUG_ROWS, tile), BF16)

    r = lax.broadcasted_iota(jnp.int32, (AUG_ROWS, tile), 0)
    for g in heads:
        qxt_ref[g, 0:HEAD_DIM, :] = q_ref[:, lanes(g)].astype(F32).T.astype(BF16)
        hi, mid, lo = _split_bf16x3(c_ref[g, qi] * LOG2_E)
        aug = jnp.where(r < 3, 1.0,
                        jnp.where(r == 3, hi.astype(F32),
                                  jnp.where(r == 4, mid.astype(F32),
                                            jnp.where(r == 5, lo.astype(F32), 0.0))))
        qxt_ref[g, HEAD_DIM:HEAD_DIM + AUG_ROWS, :] = aug.astype(BF16)

    def scores_into(g, slot, ki):
        s = _dot(kx_ref[g, ki], qxt_ref[g])
        s_ref[g][slot][...] = s
        return jnp.max(s, axis=0, keepdims=True)

    def softmax_into(g, slot, m, tile_max):
        m_new = jnp.maximum(m, tile_max)
        p_ref[g][slot][...] = jnp.exp2(s_ref[g][slot][...] - m_new).astype(BF16)
        return m_new, jnp.exp2(m - m_new)

    def accumulate(g, slot, alpha, ki):
        acc_ref[g][...] = alpha * acc_ref[g][...] + _dot(vt_ref[g, ki], p_ref[g][slot][...])

    for g in heads:
        p_ref[g][1][...] = jnp.zeros((kt, tile), BF16)
        acc_ref[g][...] = jnp.zeros(acc_ref[g].shape, F32)
    max_0 = tuple(scores_into(g, 0, 0) for g in heads)

    def pair(jj, carry):
        m, alpha_1, max_0 = carry
        k0 = 2 * jj
        max_1 = [scores_into(g, 1, k0 + 1) for g in heads]
        m, alpha_0 = zip(*[softmax_into(g, 0, m[g], max_0[g]) for g in heads])
        for g in heads:
            accumulate(g, 1, alpha_1[g], jnp.maximum(k0 - 1, 0))
        max_0 = tuple(scores_into(g, 0, k0 + 2) for g in heads)
        m, alpha_1 = zip(*[softmax_into(g, 1, m[g], max_1[g]) for g in heads])
        for g in heads:
            accumulate(g, 0, alpha_0[g], k0)
        return m, alpha_1, max_0

    init = (tuple(jnp.full((1, tile), NEG_INF, F32) for _ in heads),
            tuple(jnp.ones((1, tile), F32) for _ in heads), max_0)
    m, alpha_1, _ = lax.fori_loop(0, qi, pair, init)

    k0 = 2 * qi
    key = lax.broadcasted_iota(jnp.int32, (kt, tile), 0)
    query = lax.broadcasted_iota(jnp.int32, (kt, tile), 1)
    upper = slice(kt, tile)

    def upper_scores_into(g):
        s = _dot(kx_ref[g, k0 + 1], qxt_ref[g, :, upper])
        s = jnp.where(lax.broadcasted_iota(jnp.int32, (kt, kt), 0)
                      <= lax.broadcasted_iota(jnp.int32, (kt, kt), 1), s, NEG_INF)
        s_ref[g][1][:, 0:kt] = s
        return jnp.max(s, axis=0, keepdims=True)

    max_1 = [upper_scores_into(g) for g in heads]
    for g in heads:
        s_ref[g][0][...] = jnp.where(key <= query, s_ref[g][0][...], NEG_INF)
    max_0 = [jnp.max(s_ref[g][0][...], axis=0, keepdims=True) for g in heads]
    m, alpha_0 = zip(*[softmax_into(g, 0, m[g], max_0[g]) for g in heads])
    for g in heads:
        accumulate(g, 1, alpha_1[g], jnp.maximum(k0 - 1, 0))
    alpha_upper = []
    for g in heads:
        m_ref[g, 0:1, :] = m[g]
        m_old = m_ref[g, 0:1, upper]
        m_upper = jnp.maximum(m_old, max_1[g])
        alpha_upper.append(jnp.exp2(m_old - m_upper))
        p_ref[g][1][:, 0:kt] = jnp.exp2(s_ref[g][1][:, 0:kt] - m_upper).astype(BF16)
    for g in heads:
        accumulate(g, 0, alpha_0[g], k0)
    for g in heads:
        acc_ref[g][:, upper] = (alpha_upper[g] * acc_ref[g][:, upper]
                                + _dot(vt_ref[g, k0 + 1], p_ref[g][1][:, 0:kt]))
    for g in heads:
        o_t = acc_ref[g][0:HEAD_DIM, :] / acc_ref[g][HEAD_DIM:HEAD_DIM + 1, :]
        o_ref[:, lanes(g)] = o_t.T.astype(BF16)


def _attention(qkv, augk, c, batch, seq):
    t = qkv.shape[0]
    tile = ATTN_TILE
    kt = tile // 2
    nq = seq // tile
    g = HEADS_PER_STEP
    width = g * HEAD_DIM
    groups = N_HEADS // g
    c_rows = jnp.transpose(c.reshape(batch, seq, N_HEADS), (0, 2, 1)).reshape(
        batch * N_HEADS, nq, 1, tile)
    return pl.pallas_call(
        functools.partial(_attention_kernel, tile=tile),
        grid=(batch, groups, nq),
        in_specs=[
            pl.BlockSpec((tile, width), lambda b, h, i: (b * nq + i, h)),
            pl.BlockSpec((seq, width), lambda b, h, i: (b, groups + h)),
            pl.BlockSpec((seq, width), lambda b, h, i: (b, 2 * groups + h)),
            pl.BlockSpec((seq, width), lambda b, h, i: (b, h)),
            pl.BlockSpec((g, nq, 1, tile), lambda b, h, i: (b * groups + h, 0, 0, 0)),
        ],
        out_specs=pl.BlockSpec((tile, width), lambda b, h, i: (b * nq + i, h)),
        out_shape=jax.ShapeDtypeStruct((t, N_HEADS * HEAD_DIM), BF16),
        scratch_shapes=[pltpu.VMEM((g, seq // kt, kt, 2 * HEAD_DIM), BF16),
                        pltpu.VMEM((g, seq // kt, HEAD_DIM + SUM_ROWS, kt), BF16),
                        pltpu.VMEM((g, 2 * HEAD_DIM, tile), BF16),
                        pltpu.VMEM((g, 8, tile), F32),
                        *[pltpu.VMEM((kt, tile + LANES), F32)] * (2 * g),
                        *[pltpu.VMEM((kt, tile + LANES), BF16)] * (2 * g),
                        *[pltpu.VMEM((HEAD_DIM + SUM_ROWS, tile + LANES), F32)] * g],
        compiler_params=_params(("parallel", "parallel", "arbitrary"), 60),
        name="fox_attention",
    )(qkv, qkv, qkv, augk, c_rows)


def kernel(x, p, norm_mix_pre, w_in, forget_bias, conv_mix_w, w_branch_conv, w_branch_attn, w_out,
           norm_mix_post, norm_ffn_pre, w_up, ffn_conv_w, ffn_conv_b, w_down, norm_ffn_post,
           w_ple_proj, norm_ple_gate, w_ple_gate, norm_ple_post):
    batch, seq, d = x.shape
    t = batch * seq
    depth = w_in.shape[0]
    conv_width = conv_mix_w.shape[-1]
    attn_width = N_HEADS * HEAD_DIM
    qkv_start = 3 * conv_width
    forget_start = qkv_start + 3 * attn_width
    gate_start = forget_start + N_HEADS

    xs = x.reshape(t, d)
    for i in range(depth):
        w_in_t = jnp.transpose(w_in[i])

        h1 = _norm(xs, norm_mix_pre[i])
        a = _conv_branch(h1, w_in_t, conv_mix_w[i], seq)
        qkv = _qkv(h1, w_in_t, qkv_start, 3 * attn_width)
        gates = _gates(h1, w_in_t, gate_start, 2 * d)
        c, augk = _forget_cumsum(h1, w_in_t, forget_start, forget_bias[i], seq)
        o = _attention(qkv, augk, c, batch, seq)
        merged = _merge(a, o, w_branch_conv[i].astype(BF16), w_branch_attn[i].astype(BF16), gates)
        y1 = _matmul(merged, w_out[i], ROW_TILE_MM, 512, "out_proj")
        x1, h2 = _residual_norm(xs, y1, norm_mix_post[i], norm_ffn_pre[i])

        ffn = _ffn_up(h2, w_up[i], ffn_conv_w[i], ffn_conv_b[i], seq)
        y2 = _matmul(ffn, w_down[i].astype(BF16), ROW_TILE_WIDE_K, 512, "ffn_down")
        x2, h3 = _residual_norm(x1, y2, norm_ffn_post[i], norm_ple_gate[i])

        ge = _ple(h3, p[i].reshape(t, -1), w_ple_gate[i], w_ple_proj[i])
        xs = _residual(x2, ge, norm_ple_post[i])
    return xs.reshape(batch, seq, d)
```

```python
import functools
import math

import jax
import jax.numpy as jnp
from jax import lax
from jax.experimental import pallas as pl
from jax.experimental.pallas import tpu as pltpu

BF16 = jnp.bfloat16
F32 = jnp.float32

LANES = 128
BF16_SUBLANES = 16
N_HEADS = 32
HEAD_DIM = 128
SHORT_K = 3
EPS = 1e-6
NEG_INF = -1e30
LOG2_E = math.log2(math.e)
GELU_C0 = math.sqrt(2.0 / math.pi)
GELU_C1 = GELU_C0 * 0.044715
MIB = 1024 * 1024

ROW_TILE_NORM = 256
ROW_TILE_MM = 1024
ROW_TILE_WIDE_K = 512
ATTN_TILE = 1024
CUMSUM_TILE = 512
CARRY_ROWS = 8
FFN_HEAD_FRACTION_NUM, FFN_HEAD_FRACTION_DEN = 3, 4


def _params(semantics, vmem_mib):
    return pltpu.CompilerParams(dimension_semantics=semantics, vmem_limit_bytes=vmem_mib * MIB)


def _dot(a, b):
    return jnp.dot(a, b, preferred_element_type=F32)


def _dot_nt(a, b):
    return lax.dot_general(a, b, (((1,), (1,)), ((), ())), preferred_element_type=F32)


def _sigmoid(x):
    return 0.5 * jnp.tanh(0.5 * x) + 0.5


def _rms_scale(x, g):
    inv = lax.rsqrt(jnp.mean(x * x, axis=-1, keepdims=True) + EPS)
    return x * inv * g


def _norm_kernel(x_ref, g_ref, h_ref):
    h_ref[...] = _rms_scale(x_ref[...], g_ref[...]).astype(BF16)


def _residual_norm_kernel(x_ref, y_ref, g_post_ref, g_pre_ref, x_out_ref, h_ref):
    x_new = x_ref[...] + _rms_scale(y_ref[...].astype(F32), g_post_ref[...])
    x_out_ref[...] = x_new
    h_ref[...] = _rms_scale(x_new, g_pre_ref[...]).astype(BF16)


def _residual_kernel(x_ref, y_ref, g_post_ref, x_out_ref):
    x_out_ref[...] = x_ref[...] + _rms_scale(y_ref[...].astype(F32), g_post_ref[...])


def _row_spec(tr, d):
    return pl.BlockSpec((tr, d), lambda i: (i, 0))


def _gain_spec(d):
    return pl.BlockSpec((1, d), lambda i: (0, 0))


def _norm(x, g):
    t, d = x.shape
    tr = ROW_TILE_NORM
    return pl.pallas_call(
        _norm_kernel,
        grid=(t // tr,),
        in_specs=[_row_spec(tr, d), _gain_spec(d)],
        out_specs=_row_spec(tr, d),
        out_shape=jax.ShapeDtypeStruct((t, d), BF16),
        compiler_params=_params(("parallel",), 40),
        name="norm",
    )(x, g.reshape(1, d))


def _residual_norm(x, y, g_post, g_pre):
    t, d = x.shape
    tr = ROW_TILE_NORM
    return pl.pallas_call(
        _residual_norm_kernel,
        grid=(t // tr,),
        in_specs=[_row_spec(tr, d), _row_spec(tr, d), _gain_spec(d), _gain_spec(d)],
        out_specs=[_row_spec(tr, d), _row_spec(tr, d)],
        out_shape=[jax.ShapeDtypeStruct((t, d), F32), jax.ShapeDtypeStruct((t, d), BF16)],
        compiler_params=_params(("parallel",), 40),
        name="residual_norm",
    )(x, y, g_post.reshape(1, d), g_pre.reshape(1, d))


def _residual(x, y, g_post):
    t, d = x.shape
    tr = ROW_TILE_NORM
    return pl.pallas_call(
        _residual_kernel,
        grid=(t // tr,),
        in_specs=[_row_spec(tr, d), _row_spec(tr, d), _gain_spec(d)],
        out_specs=_row_spec(tr, d),
        out_shape=jax.ShapeDtypeStruct((t, d), F32),
        compiler_params=_params(("parallel",), 40),
        name="residual",
    )(x, y, g_post.reshape(1, d))


def _causal_conv3(u, carry_ref, w_ref):
    tm = u.shape[0]
    w0, w1, w2 = w_ref[0:1, :], w_ref[1:2, :], w_ref[2:3, :]
    y = w0 * pltpu.roll(u, 2, 0) + w1 * pltpu.roll(u, 1, 0) + w2 * u
    head = u[0:CARRY_ROWS, :]
    prev = carry_ref[...]
    r = lax.broadcasted_iota(jnp.int32, head.shape, 0)
    back1 = jnp.where(r < 1, pltpu.roll(prev, 1, 0), pltpu.roll(head, 1, 0))
    back2 = jnp.where(r < 2, pltpu.roll(prev, 2, 0), pltpu.roll(head, 2, 0))
    y_head = w0 * back2 + w1 * back1 + w2 * head
    carry_ref[...] = u[tm - CARRY_ROWS:, :]
    return jnp.concatenate([y_head, y[CARRY_ROWS:, :]], axis=0)


def _reset_carry_at_sequence_start(carry_refs, tiles_per_seq):
    @pl.when(pl.program_id(1) % tiles_per_seq == 0)
    def _():
        for ref in carry_refs:
            ref[...] = jnp.zeros_like(ref)


def _serpentine(j, i, n_tiles):
    return jnp.where(j % 2 == 0, i, n_tiles - 1 - i)


def _cast_weights_at_sweep_start(pairs):
    @pl.when(pl.program_id(1) == 0)
    def _():
        for w_ref, w_bf16_ref in pairs:
            w_bf16_ref[...] = w_ref[...].astype(BF16)


def _conv_branch_kernel(h_ref, wb_ref, wc_ref, wv_ref, cw_ref, o_ref, carry_ref,
                        wb_bf, wc_bf, wv_bf, *, tiles_per_seq):
    _cast_weights_at_sweep_start([(wb_ref, wb_bf), (wc_ref, wc_bf), (wv_ref, wv_bf)])
    _reset_carry_at_sequence_start([carry_ref], tiles_per_seq)
    h = h_ref[...]
    u = _dot_nt(h, wc_bf[...]) * _dot_nt(h, wv_bf[...])
    y = _causal_conv3(u, carry_ref, cw_ref)
    o_ref[...] = (_dot_nt(h, wb_bf[...]) * y).astype(BF16)


def _conv_branch(h, w_in_t, conv_w, seq):
    t, d = h.shape
    width = conv_w.shape[1]
    tm, tn = ROW_TILE_MM, 256
    nb = width // tn
    return pl.pallas_call(
        functools.partial(_conv_branch_kernel, tiles_per_seq=seq // tm),
        grid=(nb, t // tm),
        in_specs=[
            pl.BlockSpec((tm, d), lambda j, i: (i, 0)),
            pl.BlockSpec((tn, d), lambda j, i: (j, 0)),
            pl.BlockSpec((tn, d), lambda j, i: (j + nb, 0)),
            pl.BlockSpec((tn, d), lambda j, i: (j + 2 * nb, 0)),
            pl.BlockSpec((SHORT_K, tn), lambda j, i: (0, j)),
        ],
        out_specs=pl.BlockSpec((tm, tn), lambda j, i: (i, j)),
        out_shape=jax.ShapeDtypeStruct((t, width), BF16),
        scratch_shapes=[pltpu.VMEM((CARRY_ROWS, tn), F32)] + [pltpu.VMEM((tn, d), BF16)] * 3,
        compiler_params=_params(("parallel", "arbitrary"), 60),
        name="conv_branch",
    )(h, w_in_t, w_in_t, w_in_t, conv_w)


def _qkv_kernel(h_ref, w_ref, o_ref, w_bf, *, q_tiles, scale):
    _cast_weights_at_sweep_start([(w_ref, w_bf)])
    acc = _dot_nt(h_ref[...], w_bf[...])
    factor = jnp.where(pl.program_id(0) < q_tiles, scale, 1.0).astype(F32)
    o_ref[...] = (acc * factor).astype(BF16)


def _qkv(h, w_in_t, row_start, n):
    t, d = h.shape
    tm, tn = ROW_TILE_MM, 512
    first = row_start // tn
    kern = functools.partial(_qkv_kernel, q_tiles=(n // 3) // tn, scale=LOG2_E / math.sqrt(HEAD_DIM))
    return pl.pallas_call(
        kern,
        grid=(n // tn, t // tm),
        in_specs=[pl.BlockSpec((tm, d), lambda j, i: (_serpentine(j, i, t // tm), 0)),
                  pl.BlockSpec((tn, d), lambda j, i: (first + j, 0))],
        out_specs=pl.BlockSpec((tm, tn), lambda j, i: (_serpentine(j, i, t // tm), j)),
        out_shape=jax.ShapeDtypeStruct((t, n), BF16),
        scratch_shapes=[pltpu.VMEM((tn, d), BF16)],
        compiler_params=_params(("parallel", "arbitrary"), 48),
        name="qkv",
    )(h, w_in_t)


def _gates_kernel(h_ref, w_ref, w_next_ref, o_ref, w_bf, *, shift):
    @pl.when(pl.program_id(1) == 0)
    def _():
        tn = w_bf.shape[0]
        w_bf[0:tn - shift, :] = w_ref[shift:tn, :].astype(BF16)
        w_bf[tn - shift:tn, :] = w_next_ref[0:shift, :].astype(BF16)

    o_ref[...] = _sigmoid(_dot_nt(h_ref[...], w_bf[...])).astype(BF16)


def _gates(h, w_in_t, row_start, n):
    t, d = h.shape
    tm, tn = ROW_TILE_MM, 512
    shift = row_start % LANES
    aligned = row_start - shift
    assert 0 < shift and shift % BF16_SUBLANES == 0 and aligned % tn == 0
    return pl.pallas_call(
        functools.partial(_gates_kernel, shift=shift),
        grid=(n // tn, t // tm),
        in_specs=[pl.BlockSpec((tm, d), lambda j, i: (_serpentine(j, i, t // tm), 0)),
                  pl.BlockSpec((tn, d), lambda j, i: (aligned // tn + j, 0)),
                  pl.BlockSpec((LANES, d), lambda j, i: ((aligned + (j + 1) * tn) // LANES, 0))],
        out_specs=pl.BlockSpec((tm, tn), lambda j, i: (_serpentine(j, i, t // tm), j)),
        out_shape=jax.ShapeDtypeStruct((t, n), BF16),
        scratch_shapes=[pltpu.VMEM((tn, d), BF16)],
        compiler_params=_params(("parallel", "arbitrary"), 48),
        name="gates",
    )(h, w_in_t, w_in_t)


def _matmul_kernel(a_ref, w_ref, o_ref):
    o_ref[...] = _dot(a_ref[...], w_ref[...]).astype(BF16)


def _matmul_f32_weights_kernel(a_ref, w_ref, o_ref, w_bf):
    _cast_weights_at_sweep_start([(w_ref, w_bf)])
    o_ref[...] = _dot(a_ref[...], w_bf[...]).astype(BF16)


def _matmul(a, w, tm, tn, name):
    t, k = a.shape
    n = w.shape[1]
    cast_in_kernel = w.dtype == F32
    return pl.pallas_call(
        _matmul_f32_weights_kernel if cast_in_kernel else _matmul_kernel,
        grid=(n // tn, t // tm),
        in_specs=[pl.BlockSpec((tm, k), lambda j, i: (_serpentine(j, i, t // tm), 0)),
                  pl.BlockSpec((k, tn), lambda j, i: (0, j))],
        out_specs=pl.BlockSpec((tm, tn), lambda j, i: (_serpentine(j, i, t // tm), j)),
        out_shape=jax.ShapeDtypeStruct((t, n), BF16),
        scratch_shapes=[pltpu.VMEM((k, tn), BF16)] if cast_in_kernel else [],
        compiler_params=_params(("parallel", "arbitrary"), 52),
        name=name,
    )(a, w)


def _merge_kernel(a_ref, o_ref, wa_ref, wb_ref, ga_ref, gb_ref, out_ref):
    y_a = _dot(a_ref[...], wa_ref[...])
    y_b = _dot(o_ref[...], wb_ref[...])
    out_ref[...] = (ga_ref[...] * y_a + gb_ref[...] * y_b).astype(BF16)


def _merge(a, o, w_a, w_b, gates):
    t, d = a.shape
    n = w_a.shape[1]
    tm, tn = ROW_TILE_MM, 512
    nb = n // tn
    return pl.pallas_call(
        _merge_kernel,
        grid=(nb, t // tm),
        in_specs=[
            pl.BlockSpec((tm, d), lambda j, i: (_serpentine(j, i, t // tm), 0)),
            pl.BlockSpec((tm, d), lambda j, i: (_serpentine(j, i, t // tm), 0)),
            pl.BlockSpec((d, tn), lambda j, i: (0, j)),
            pl.BlockSpec((d, tn), lambda j, i: (0, j)),
            pl.BlockSpec((tm, tn), lambda j, i: (_serpentine(j, i, t // tm), j)),
            pl.BlockSpec((tm, tn), lambda j, i: (_serpentine(j, i, t // tm), j + nb)),
        ],
        out_specs=pl.BlockSpec((tm, tn), lambda j, i: (_serpentine(j, i, t // tm), j)),
        out_shape=jax.ShapeDtypeStruct((t, n), BF16),
        compiler_params=_params(("parallel", "arbitrary"), 60),
        name="merge",
    )(a, o, w_a, w_b, gates, gates)


def _ffn_up_kernel(h_ref, wg_ref, wv_ref, cwg_ref, cwv_ref, bg_ref, bv_ref, o_ref,
                   carry_g_ref, carry_v_ref, wg_bf, wv_bf, raw_g_ref, raw_v_ref, eye_ref,
                   *, tiles_per_seq):
    _cast_weights_at_sweep_start([(wg_ref, wg_bf), (wv_ref, wv_bf)])
    _reset_carry_at_sequence_start([carry_g_ref, carry_v_ref], tiles_per_seq)

    @pl.when(pl.program_id(1) == 0)
    def _():
        tn = eye_ref.shape[0]
        eye_ref[...] = (lax.broadcasted_iota(jnp.int32, (tn, tn), 0)
                        == lax.broadcasted_iota(jnp.int32, (tn, tn), 1)).astype(BF16)

    tm = h_ref.shape[0]
    head = FFN_HEAD_FRACTION_NUM * tm // FFN_HEAD_FRACTION_DEN
    zero_offset = jnp.minimum(pl.program_id(1), 0)

    def activation(raw_gate, raw_val):
        u_gate = _causal_conv3(raw_gate, carry_g_ref, cwg_ref) + bg_ref[...]
        u_val = _causal_conv3(raw_val, carry_v_ref, cwv_ref) + bv_ref[...]
        half_gate = 0.5 * u_gate
        t = jnp.tanh(u_gate * (GELU_C0 + GELU_C1 * (u_gate * u_gate)))
        return ((half_gate + half_gate * t) * u_val).astype(BF16)

    h_head = h_ref[0:head, :]
    raw_g_ref[...] = _dot(h_head, wg_bf[...])
    raw_v_ref[...] = _dot(h_head, wv_bf[...])
    parked = pl.ds(pl.multiple_of(zero_offset, CARRY_ROWS), head)
    act_head = activation(raw_g_ref[parked, :], raw_v_ref[parked, :])
    h_tail = h_ref[head:, :]
    raw_g_tail = _dot(h_tail, wg_bf[...])
    raw_v_tail = _dot(h_tail, wv_bf[...])
    o_ref[0:head, :] = _dot(act_head, eye_ref[...]).astype(BF16)
    o_ref[head:, :] = activation(raw_g_tail, raw_v_tail)


def _ffn_up(h, w_up, conv_w, conv_b, seq):
    t, d = h.shape
    d_ff = w_up.shape[1] // 2
    tm, tn = ROW_TILE_MM, 256
    head = FFN_HEAD_FRACTION_NUM * tm // FFN_HEAD_FRACTION_DEN
    nb = d_ff // tn
    conv_b = conv_b.reshape(1, 2 * d_ff)
    return pl.pallas_call(
        functools.partial(_ffn_up_kernel, tiles_per_seq=seq // tm),
        grid=(nb, t // tm),
        in_specs=[
            pl.BlockSpec((tm, d), lambda j, i: (i, 0)),
            pl.BlockSpec((d, tn), lambda j, i: (0, j)),
            pl.BlockSpec((d, tn), lambda j, i: (0, j + nb)),
            pl.BlockSpec((SHORT_K, tn), lambda j, i: (0, j)),
            pl.BlockSpec((SHORT_K, tn), lambda j, i: (0, j + nb)),
            pl.BlockSpec((1, tn), lambda j, i: (0, j)),
            pl.BlockSpec((1, tn), lambda j, i: (0, j + nb)),
        ],
        out_specs=pl.BlockSpec((tm, tn), lambda j, i: (i, j)),
        out_shape=jax.ShapeDtypeStruct((t, d_ff), BF16),
        scratch_shapes=([pltpu.VMEM((CARRY_ROWS, tn), F32)] * 2 + [pltpu.VMEM((d, tn), BF16)] * 2
                        + [pltpu.VMEM((head, tn), F32)] * 2 + [pltpu.VMEM((tn, tn), BF16)]),
        compiler_params=_params(("parallel", "arbitrary"), 48),
        name="ffn_up",
    )(h, w_up, w_up, conv_w, conv_w, conv_b, conv_b)


def _ple_kernel(h_ref, p_ref, wg_ref, wp_ref, o_ref, wg_bf, wp_bf):
    _cast_weights_at_sweep_start([(wg_ref, wg_bf), (wp_ref, wp_bf)])
    gate = _sigmoid(_dot(h_ref[...], wg_bf[...]))
    o_ref[...] = (gate * _dot(p_ref[...].astype(BF16), wp_bf[...])).astype(BF16)


def _ple(h, p, w_gate, w_proj):
    t, d = h.shape
    n = w_gate.shape[1]
    ple = p.shape[1]
    tm, tn = ROW_TILE_MM, 512
    return pl.pallas_call(
        _ple_kernel,
        grid=(n // tn, t // tm),
        in_specs=[
            pl.BlockSpec((tm, d), lambda j, i: (_serpentine(j, i, t // tm), 0)),
            pl.BlockSpec((tm, ple), lambda j, i: (_serpentine(j, i, t // tm), 0)),
            pl.BlockSpec((d, tn), lambda j, i: (0, j)),
            pl.BlockSpec((ple, tn), lambda j, i: (0, j)),
        ],
        out_specs=pl.BlockSpec((tm, tn), lambda j, i: (_serpentine(j, i, t // tm), j)),
        out_shape=jax.ShapeDtypeStruct((t, n), BF16),
        scratch_shapes=[pltpu.VMEM((d, tn), BF16), pltpu.VMEM((ple, tn), BF16)],
        compiler_params=_params(("parallel", "arbitrary"), 48),
        name="ple",
    )(h, p, w_gate, w_proj)


def _split_bf16x3(x):
    hi = x.astype(BF16)
    rest = x - hi.astype(F32)
    mid = rest.astype(BF16)
    lo = (rest - mid.astype(F32)).astype(BF16)
    return hi, mid, lo


def _forget_cumsum_kernel(h_ref, wf_ref, b_ref, spread_ref, ones_ref, c_ref, augk_ref, carry_ref,
                          wf_bf, *, tiles_per_seq):
    @pl.when(pl.program_id(0) == 0)
    def _():
        wf_bf[...] = wf_ref[...].astype(BF16)

    @pl.when(pl.program_id(0) % tiles_per_seq == 0)
    def _():
        carry_ref[...] = jnp.zeros_like(carry_ref)

    ts = h_ref.shape[0]
    nh = c_ref.shape[1]
    lane = lax.broadcasted_iota(jnp.int32, (ts, LANES), 1)
    log_f = jnp.where(lane < nh, jax.nn.log_sigmoid(_dot_nt(h_ref[...], wf_bf[...]) + b_ref[...]), 0.0)
    row = lax.broadcasted_iota(jnp.int32, (ts, ts), 0)
    col = lax.broadcasted_iota(jnp.int32, (ts, ts), 1)
    lower = (col <= row).astype(F32)
    csum = jnp.dot(lower, log_f, preferred_element_type=F32,
                   precision=lax.Precision.HIGHEST) + carry_ref[...]
    c_ref[...] = csum[:, 0:nh]
    carry_ref[...] = csum[ts - 1:ts, :]

    c3 = jnp.where(lane < nh, csum,
                   jnp.where(lane < 2 * nh, pltpu.roll(csum, nh, 1), pltpu.roll(csum, 2 * nh, 1)))
    hi, mid, lo = _split_bf16x3(c3 * LOG2_E)
    pieces = jnp.where(lane < nh, hi.astype(F32),
                       jnp.where(lane < 2 * nh, mid.astype(F32), lo.astype(F32))).astype(BF16)
    augk_ref[...] = (_dot(pieces, spread_ref[...]) + ones_ref[...]).astype(BF16)


def _forget_cumsum(h, w_in_t, row_start, bias, seq):
    t, d = h.shape
    nh = bias.shape[0]
    ts = CUMSUM_TILE
    assert row_start % LANES == 0 and 3 * nh <= LANES
    width = nh * HEAD_DIM
    lane = jnp.arange(width)[None, :]
    piece_row = jnp.arange(LANES)[:, None]
    spread = jnp.where((piece_row < 3 * nh)
                       & (lane == (piece_row % nh) * HEAD_DIM + piece_row // nh), -1.0, 0.0)
    ones = jnp.where((lane % HEAD_DIM >= 3) & (lane % HEAD_DIM < 6), 1.0, 0.0).astype(F32)
    bias_tile = jnp.zeros((1, LANES), F32).at[0, 0:nh].set(bias)
    return pl.pallas_call(
        functools.partial(_forget_cumsum_kernel, tiles_per_seq=seq // ts),
        grid=(t // ts,),
        in_specs=[pl.BlockSpec((ts, d), lambda i: (i, 0)),
                  pl.BlockSpec((LANES, d), lambda i: (row_start // LANES, 0)),
                  pl.BlockSpec((1, LANES), lambda i: (0, 0)),
                  pl.BlockSpec((LANES, width), lambda i: (0, 0)),
                  pl.BlockSpec((1, width), lambda i: (0, 0))],
        out_specs=[pl.BlockSpec((ts, nh), lambda i: (i, 0)),
                   pl.BlockSpec((ts, width), lambda i: (i, 0))],
        out_shape=[jax.ShapeDtypeStruct((t, nh), F32), jax.ShapeDtypeStruct((t, width), BF16)],
        scratch_shapes=[pltpu.VMEM((1, LANES), F32), pltpu.VMEM((LANES, d), BF16)],
        compiler_params=_params(("arbitrary",), 48),
        name="forget_cumsum",
    )(h, w_in_t, bias_tile, spread.astype(BF16), ones)


AUG_ROWS = 16
SUM_ROWS = 16
HEADS_PER_STEP = 2


def _attention_kernel(q_ref, k_ref, v_ref, augk_ref, c_ref, o_ref,
                      kx_ref, vt_ref, qxt_ref, m_ref, *buffers, tile):
    n = HEADS_PER_STEP
    buffers = [buf.at[:, 0:tile] for buf in buffers]
    s_ref = [buffers[2 * g:2 * g + 2] for g in range(n)]
    p_ref = [buffers[2 * n + 2 * g:2 * n + 2 * g + 2] for g in range(n)]
    acc_ref = buffers[4 * n:]
    kt = tile // 2
    qi = pl.program_id(2)
    n_key_tiles = kx_ref.shape[1]
    heads = range(HEADS_PER_STEP)

    def lanes(g):
        return slice(g * HEAD_DIM, (g + 1) * HEAD_DIM)

    @pl.when(qi == 0)
    def _():
        def stage(j, _):
            rows = pl.ds(pl.multiple_of(j * kt, kt), kt)
            for g in heads:
                kx_ref[g, j, :, 0:HEAD_DIM] = k_ref[rows, lanes(g)]
                kx_ref[g, j, :, HEAD_DIM:] = augk_ref[rows, lanes(g)]
                vt_ref[g, j, 0:HEAD_DIM, :] = v_ref[rows, lanes(g)].astype(F32).T.astype(BF16)
                vt_ref[g, j, HEAD_DIM:, :] = jnp.ones((SUM_ROWS, kt), BF16)
            return 0
        lax.fori_loop(0, n_key_tiles, stage, 0)
        for g in heads:
            qxt_ref[g, HEAD_DIM + AUG_ROWS:, :] = jnp.zeros((HEAD_DIM - AUG_ROWS, tile), BF16)

    r = lax.broadcasted_iota(jnp.int32, (AUG_ROWS, tile), 0)
    for g in heads:
        qxt_ref[g, 0:HEAD_DIM, :] = q_ref[:, lanes(g)].astype(F32).T.astype(BF16)
        hi, mid, lo = _split_bf16x3(c_ref[g, qi] * LOG2_E)
        aug = jnp.where(r < 3, 1.0,
                        jnp.where(r == 3, hi.astype(F32),
                                  jnp.where(r == 4, mid.astype(F32),
                                            jnp.where(r == 5, lo.astype(F32), 0.0))))
        qxt_ref[g, HEAD_DIM:HEAD_DIM + AUG_ROWS, :] = aug.astype(BF16)

    def scores_into(g, slot, ki):
        s = _dot(kx_ref[g, ki], qxt_ref[g])
        s_ref[g][slot][...] = s
        return jnp.max(s, axis=0, keepdims=True)

    def softmax_into(g, slot, m, tile_max):
        m_new = jnp.maximum(m, tile_max)
        p_ref[g][slot][...] = jnp.exp2(s_ref[g][slot][...] - m_new).astype(BF16)
        return m_new, jnp.exp2(m - m_new)

    def accumulate(g, slot, alpha, ki):
        acc_ref[g][...] = alpha * acc_ref[g][...] + _dot(vt_ref[g, ki], p_ref[g][slot][...])

    for g in heads:
        p_ref[g][1][...] = jnp.zeros((kt, tile), BF16)
        acc_ref[g][...] = jnp.zeros(acc_ref[g].shape, F32)
    max_0 = tuple(scores_into(g, 0, 0) for g in heads)

    def pair(jj, carry):
        m, alpha_1, max_0 = carry
        k0 = 2 * jj
        max_1 = [scores_into(g, 1, k0 + 1) for g in heads]
        m, alpha_0 = zip(*[softmax_into(g, 0, m[g], max_0[g]) for g in heads])
        for g in heads:
            accumulate(g, 1, alpha_1[g], jnp.maximum(k0 - 1, 0))
        max_0 = tuple(scores_into(g, 0, k0 + 2) for g in heads)
        m, alpha_1 = zip(*[softmax_into(g, 1, m[g], max_1[g]) for g in heads])
        for g in heads:
            accumulate(g, 0, alpha_0[g], k0)
        return m, alpha_1, max_0

    init = (tuple(jnp.full((1, tile), NEG_INF, F32) for _ in heads),
            tuple(jnp.ones((1, tile), F32) for _ in heads), max_0)
    m, alpha_1, _ = lax.fori_loop(0, qi, pair, init)

    k0 = 2 * qi
    key = lax.broadcasted_iota(jnp.int32, (kt, tile), 0)
    query = lax.broadcasted_iota(jnp.int32, (kt, tile), 1)
    upper = slice(kt, tile)

    def upper_scores_into(g):
        s = _dot(kx_ref[g, k0 + 1], qxt_ref[g, :, upper])
        s = jnp.where(lax.broadcasted_iota(jnp.int32, (kt, kt), 0)
                      <= lax.broadcasted_iota(jnp.int32, (kt, kt), 1), s, NEG_INF)
        s_ref[g][1][:, 0:kt] = s
        return jnp.max(s, axis=0, keepdims=True)

    max_1 = [upper_scores_into(g) for g in heads]
    for g in heads:
        s_ref[g][0][...] = jnp.where(key <= query, s_ref[g][0][...], NEG_INF)
    max_0 = [jnp.max(s_ref[g][0][...], axis=0, keepdims=True) for g in heads]
    m, alpha_0 = zip(*[softmax_into(g, 0, m[g], max_0[g]) for g in heads])
    for g in heads:
        accumulate(g, 1, alpha_1[g], jnp.maximum(k0 - 1, 0))
    alpha_upper = []
    for g in heads:
        m_ref[g, 0:1, :] = m[g]
        m_old = m_ref[g, 0:1, upper]
        m_upper = jnp.maximum(m_old, max_1[g])
        alpha_upper.append(jnp.exp2(m_old - m_upper))
        p_ref[g][1][:, 0:kt] = jnp.exp2(s_ref[g][1][:, 0:kt] - m_upper).astype(BF16)
    for g in heads:
        accumulate(g, 0, alpha_0[g], k0)
    for g in heads:
        acc_ref[g][:, upper] = (alpha_upper[g] * acc_ref[g][:, upper]
                                + _dot(vt_ref[g, k0 + 1], p_ref[g][1][:, 0:kt]))
    for g in heads:
        o_t = acc_ref[g][0:HEAD_DIM, :] / acc_ref[g][HEAD_DIM:HEAD_DIM + 1, :]
        o_ref[:, lanes(g)] = o_t.T.astype(BF16)


def _attention(qkv, augk, c, batch, seq):
    t = qkv.shape[0]
    tile = ATTN_TILE
    kt = tile // 2
    nq = seq // tile
    g = HEADS_PER_STEP
    width = g * HEAD_DIM
    groups = N_HEADS // g
    c_rows = jnp.transpose(c.reshape(batch, seq, N_HEADS), (0, 2, 1)).reshape(
        batch * N_HEADS, nq, 1, tile)
    return pl.pallas_call(
        functools.partial(_attention_kernel, tile=tile),
        grid=(batch, groups, nq),
        in_specs=[
            pl.BlockSpec((tile, width), lambda b, h, i: (b * nq + i, h)),
            pl.BlockSpec((seq, width), lambda b, h, i: (b, groups + h)),
            pl.BlockSpec((seq, width), lambda b, h, i: (b, 2 * groups + h)),
            pl.BlockSpec((seq, width), lambda b, h, i: (b, h)),
            pl.BlockSpec((g, nq, 1, tile), lambda b, h, i: (b * groups + h, 0, 0, 0)),
        ],
        out_specs=pl.BlockSpec((tile, width), lambda b, h, i: (b * nq + i, h)),
        out_shape=jax.ShapeDtypeStruct((t, N_HEADS * HEAD_DIM), BF16),
        scratch_shapes=[pltpu.VMEM((g, seq // kt, kt, 2 * HEAD_DIM), BF16),
                        pltpu.VMEM((g, seq // kt, HEAD_DIM + SUM_ROWS, kt), BF16),
                        pltpu.VMEM((g, 2 * HEAD_DIM, tile), BF16),
                        pltpu.VMEM((g, 8, tile), F32),
                        *[pltpu.VMEM((kt, tile + LANES), F32)] * (2 * g),
                        *[pltpu.VMEM((kt, tile + LANES), BF16)] * (2 * g),
                        *[pltpu.VMEM((HEAD_DIM + SUM_ROWS, tile + LANES), F32)] * g],
        compiler_params=_params(("parallel", "parallel", "arbitrary"), 60),
        name="fox_attention",
    )(qkv, qkv, qkv, augk, c_rows)


def kernel(x, p, norm_mix_pre, w_in, forget_bias, conv_mix_w, w_branch_conv, w_branch_attn, w_out,
           norm_mix_post, norm_ffn_pre, w_up, ffn_conv_w, ffn_conv_b, w_down, norm_ffn_post,
           w_ple_proj, norm_ple_gate, w_ple_gate, norm_ple_post):
    batch, seq, d = x.shape
    t = batch * seq
    depth = w_in.shape[0]
    conv_width = conv_mix_w.shape[-1]
    attn_width = N_HEADS * HEAD_DIM
    qkv_start = 3 * conv_width
    forget_start = qkv_start + 3 * attn_width
    gate_start = forget_start + N_HEADS

    xs = x.reshape(t, d)
    for i in range(depth):
        w_in_t = jnp.transpose(w_in[i])

        h1 = _norm(xs, norm_mix_pre[i])
        a = _conv_branch(h1, w_in_t, conv_mix_w[i], seq)
        qkv = _qkv(h1, w_in_t, qkv_start, 3 * attn_width)
        gates = _gates(h1, w_in_t, gate_start, 2 * d)
        c, augk = _forget_cumsum(h1, w_in_t, forget_start, forget_bias[i], seq)
        o = _attention(qkv, augk, c, batch, seq)
        merged = _merge(a, o, w_branch_conv[i].astype(BF16), w_branch_attn[i].astype(BF16), gates)
        y1 = _matmul(merged, w_out[i], ROW_TILE_MM, 512, "out_proj")
        x1, h2 = _residual_norm(xs, y1, norm_mix_post[i], norm_ffn_pre[i])

        ffn = _ffn_up(h2, w_up[i], ffn_conv_w[i], ffn_conv_b[i], seq)
        y2 = _matmul(ffn, w_down[i].astype(BF16), ROW_TILE_WIDE_K, 512, "ffn_down")
        x2, h3 = _residual_norm(x1, y2, norm_ffn_post[i], norm_ple_gate[i])

        ge = _ple(h3, p[i].reshape(t, -1), w_ple_gate[i], w_ple_proj[i])
        xs = _residual(x2, ge, norm_ple_post[i])
    return xs.reshape(batch, seq, d)
```

```python
import functools
import math

import jax
import jax.numpy as jnp
from jax import lax
from jax.experimental import pallas as pl
from jax.experimental.pallas import tpu as pltpu

BF16 = jnp.bfloat16
F32 = jnp.float32

LANES = 128
BF16_SUBLANES = 16
N_HEADS = 32
HEAD_DIM = 128
SHORT_K = 3
EPS = 1e-6
NEG_INF = -1e30
LOG2_E = math.log2(math.e)
GELU_C0 = math.sqrt(2.0 / math.pi)
GELU_C1 = GELU_C0 * 0.044715
MIB = 1024 * 1024

ROW_TILE_NORM = 256
ROW_TILE_MM = 1024
ROW_TILE_WIDE_K = 512
ATTN_TILE = 1024
CUMSUM_TILE = 512
CARRY_ROWS = 8


def _params(semantics, vmem_mib):
    return pltpu.CompilerParams(dimension_semantics=semantics, vmem_limit_bytes=vmem_mib * MIB)


def _dot(a, b):
    return jnp.dot(a, b, preferred_element_type=F32)


def _dot_nt(a, b):
    return lax.dot_general(a, b, (((1,), (1,)), ((), ())), preferred_element_type=F32)


def _sigmoid(x):
    return 0.5 * jnp.tanh(0.5 * x) + 0.5


def _rms_scale(x, g):
    inv = lax.rsqrt(jnp.mean(x * x, axis=-1, keepdims=True) + EPS)
    return x * inv * g


def _norm_kernel(x_ref, g_ref, h_ref):
    h_ref[...] = _rms_scale(x_ref[...], g_ref[...]).astype(BF16)


def _residual_norm_kernel(x_ref, y_ref, g_post_ref, g_pre_ref, x_out_ref, h_ref):
    x_new = x_ref[...] + _rms_scale(y_ref[...].astype(F32), g_post_ref[...])
    x_out_ref[...] = x_new
    h_ref[...] = _rms_scale(x_new, g_pre_ref[...]).astype(BF16)


def _residual_kernel(x_ref, y_ref, g_post_ref, x_out_ref):
    x_out_ref[...] = x_ref[...] + _rms_scale(y_ref[...].astype(F32), g_post_ref[...])


def _row_spec(tr, d):
    return pl.BlockSpec((tr, d), lambda i: (i, 0))


def _gain_spec(d):
    return pl.BlockSpec((1, d), lambda i: (0, 0))


def _norm(x, g):
    t, d = x.shape
    tr = ROW_TILE_NORM
    return pl.pallas_call(
        _norm_kernel,
        grid=(t // tr,),
        in_specs=[_row_spec(tr, d), _gain_spec(d)],
        out_specs=_row_spec(tr, d),
        out_shape=jax.ShapeDtypeStruct((t, d), BF16),
        compiler_params=_params(("parallel",), 40),
        name="norm",
    )(x, g.reshape(1, d))


def _residual_norm(x, y, g_post, g_pre):
    t, d = x.shape
    tr = ROW_TILE_NORM
    return pl.pallas_call(
        _residual_norm_kernel,
        grid=(t // tr,),
        in_specs=[_row_spec(tr, d), _row_spec(tr, d), _gain_spec(d), _gain_spec(d)],
        out_specs=[_row_spec(tr, d), _row_spec(tr, d)],
        out_shape=[jax.ShapeDtypeStruct((t, d), F32), jax.ShapeDtypeStruct((t, d), BF16)],
        compiler_params=_params(("parallel",), 40),
        name="residual_norm",
    )(x, y, g_post.reshape(1, d), g_pre.reshape(1, d))


def _residual(x, y, g_post):
    t, d = x.shape
    tr = ROW_TILE_NORM
    return pl.pallas_call(
        _residual_kernel,
        grid=(t // tr,),
        in_specs=[_row_spec(tr, d), _row_spec(tr, d), _gain_spec(d)],
        out_specs=_row_spec(tr, d),
        out_shape=jax.ShapeDtypeStruct((t, d), F32),
        compiler_params=_params(("parallel",), 40),
        name="residual",
    )(x, y, g_post.reshape(1, d))


def _causal_conv3(u, carry_ref, w_ref):
    tm = u.shape[0]
    w0, w1, w2 = w_ref[0:1, :], w_ref[1:2, :], w_ref[2:3, :]
    y = w0 * pltpu.roll(u, 2, 0) + w1 * pltpu.roll(u, 1, 0) + w2 * u
    head = u[0:CARRY_ROWS, :]
    prev = carry_ref[...]
    r = lax.broadcasted_iota(jnp.int32, head.shape, 0)
    back1 = jnp.where(r < 1, pltpu.roll(prev, 1, 0), pltpu.roll(head, 1, 0))
    back2 = jnp.where(r < 2, pltpu.roll(prev, 2, 0), pltpu.roll(head, 2, 0))
    y_head = w0 * back2 + w1 * back1 + w2 * head
    carry_ref[...] = u[tm - CARRY_ROWS:, :]
    return jnp.concatenate([y_head, y[CARRY_ROWS:, :]], axis=0)


def _reset_carry_at_sequence_start(carry_refs, tiles_per_seq):
    @pl.when(pl.program_id(1) % tiles_per_seq == 0)
    def _():
        for ref in carry_refs:
            ref[...] = jnp.zeros_like(ref)


def _serpentine(j, i, n_tiles):
    return jnp.where(j % 2 == 0, i, n_tiles - 1 - i)


def _cast_weights_at_sweep_start(pairs):
    @pl.when(pl.program_id(1) == 0)
    def _():
        for w_ref, w_bf16_ref in pairs:
            w_bf16_ref[...] = w_ref[...].astype(BF16)


HEAD_NUM, HEAD_DEN = 3, 4


def _head_rows(tm):
    return HEAD_NUM * tm // HEAD_DEN


def _write_identity_at_sweep_start(eye_ref):
    @pl.when(pl.program_id(1) == 0)
    def _():
        n = eye_ref.shape[0]
        eye_ref[...] = (lax.broadcasted_iota(jnp.int32, (n, n), 0)
                        == lax.broadcasted_iota(jnp.int32, (n, n), 1)).astype(BF16)


def _parked(ref):
    rows = pl.ds(pl.multiple_of(jnp.minimum(pl.program_id(1), 0), CARRY_ROWS), ref.shape[0])
    return ref[rows, :]


def _through_identity(x, eye_ref):
    return _dot(x, eye_ref[...]).astype(x.dtype)


def _conv_branch_kernel(h_ref, wb_ref, wc_ref, wv_ref, cw_ref, o_ref, carry_ref,
                        wb_bf, wc_bf, wv_bf, *, tiles_per_seq):
    _cast_weights_at_sweep_start([(wb_ref, wb_bf), (wc_ref, wc_bf), (wv_ref, wv_bf)])
    _reset_carry_at_sequence_start([carry_ref], tiles_per_seq)
    h = h_ref[...]
    u = _dot_nt(h, wc_bf[...]) * _dot_nt(h, wv_bf[...])
    y = _causal_conv3(u, carry_ref, cw_ref)
    o_ref[...] = (_dot_nt(h, wb_bf[...]) * y).astype(BF16)


def _conv_branch(h, w_in_t, conv_w, seq):
    t, d = h.shape
    width = conv_w.shape[1]
    tm, tn = ROW_TILE_MM, 256
    nb = width // tn
    return pl.pallas_call(
        functools.partial(_conv_branch_kernel, tiles_per_seq=seq // tm),
        grid=(nb, t // tm),
        in_specs=[
            pl.BlockSpec((tm, d), lambda j, i: (i, 0)),
            pl.BlockSpec((tn, d), lambda j, i: (j, 0)),
            pl.BlockSpec((tn, d), lambda j, i: (j + nb, 0)),
            pl.BlockSpec((tn, d), lambda j, i: (j + 2 * nb, 0)),
            pl.BlockSpec((SHORT_K, tn), lambda j, i: (0, j)),
        ],
        out_specs=pl.BlockSpec((tm, tn), lambda j, i: (i, j)),
        out_shape=jax.ShapeDtypeStruct((t, width), BF16),
        scratch_shapes=[pltpu.VMEM((CARRY_ROWS, tn), F32)] + [pltpu.VMEM((tn, d), BF16)] * 3,
        compiler_params=_params(("parallel", "arbitrary"), 60),
        name="conv_branch",
    )(h, w_in_t, w_in_t, w_in_t, conv_w)


def _qkv_kernel(h_ref, w_ref, o_ref, w_bf, *, q_tiles, scale):
    _cast_weights_at_sweep_start([(w_ref, w_bf)])
    acc = _dot_nt(h_ref[...], w_bf[...])
    factor = jnp.where(pl.program_id(0) < q_tiles, scale, 1.0).astype(F32)
    o_ref[...] = (acc * factor).astype(BF16)


def _qkv(h, w_in_t, row_start, n):
    t, d = h.shape
    tm, tn = ROW_TILE_MM, 512
    first = row_start // tn
    kern = functools.partial(_qkv_kernel, q_tiles=(n // 3) // tn, scale=LOG2_E / math.sqrt(HEAD_DIM))
    return pl.pallas_call(
        kern,
        grid=(n // tn, t // tm),
        in_specs=[pl.BlockSpec((tm, d), lambda j, i: (_serpentine(j, i, t // tm), 0)),
                  pl.BlockSpec((tn, d), lambda j, i: (first + j, 0))],
        out_specs=pl.BlockSpec((tm, tn), lambda j, i: (_serpentine(j, i, t // tm), j)),
        out_shape=jax.ShapeDtypeStruct((t, n), BF16),
        scratch_shapes=[pltpu.VMEM((tn, d), BF16)],
        compiler_params=_params(("parallel", "arbitrary"), 48),
        name="qkv",
    )(h, w_in_t)


def _gates_kernel(h_ref, w_ref, w_next_ref, o_ref, w_bf, *, shift):
    @pl.when(pl.program_id(1) == 0)
    def _():
        tn = w_bf.shape[0]
        w_bf[0:tn - shift, :] = w_ref[shift:tn, :].astype(BF16)
        w_bf[tn - shift:tn, :] = w_next_ref[0:shift, :].astype(BF16)

    o_ref[...] = _sigmoid(_dot_nt(h_ref[...], w_bf[...])).astype(BF16)


def _gates(h, w_in_t, row_start, n):
    t, d = h.shape
    tm, tn = ROW_TILE_MM, 512
    shift = row_start % LANES
    aligned = row_start - shift
    assert 0 < shift and shift % BF16_SUBLANES == 0 and aligned % tn == 0
    return pl.pallas_call(
        functools.partial(_gates_kernel, shift=shift),
        grid=(n // tn, t // tm),
        in_specs=[pl.BlockSpec((tm, d), lambda j, i: (_serpentine(j, i, t // tm), 0)),
                  pl.BlockSpec((tn, d), lambda j, i: (aligned // tn + j, 0)),
                  pl.BlockSpec((LANES, d), lambda j, i: ((aligned + (j + 1) * tn) // LANES, 0))],
        out_specs=pl.BlockSpec((tm, tn), lambda j, i: (_serpentine(j, i, t // tm), j)),
        out_shape=jax.ShapeDtypeStruct((t, n), BF16),
        scratch_shapes=[pltpu.VMEM((tn, d), BF16)],
        compiler_params=_params(("parallel", "arbitrary"), 48),
        name="gates",
    )(h, w_in_t, w_in_t)


def _matmul_kernel(a_ref, w_ref, o_ref):
    o_ref[...] = _dot(a_ref[...], w_ref[...]).astype(BF16)


def _matmul_f32_weights_kernel(a_ref, w_ref, o_ref, w_bf):
    _cast_weights_at_sweep_start([(w_ref, w_bf)])
    o_ref[...] = _dot(a_ref[...], w_bf[...]).astype(BF16)


def _matmul(a, w, tm, tn, name):
    t, k = a.shape
    n = w.shape[1]
    cast_in_kernel = w.dtype == F32
    return pl.pallas_call(
        _matmul_f32_weights_kernel if cast_in_kernel else _matmul_kernel,
        grid=(n // tn, t // tm),
        in_specs=[pl.BlockSpec((tm, k), lambda j, i: (_serpentine(j, i, t // tm), 0)),
                  pl.BlockSpec((k, tn), lambda j, i: (0, j))],
        out_specs=pl.BlockSpec((tm, tn), lambda j, i: (_serpentine(j, i, t // tm), j)),
        out_shape=jax.ShapeDtypeStruct((t, n), BF16),
        scratch_shapes=[pltpu.VMEM((k, tn), BF16)] if cast_in_kernel else [],
        compiler_params=_params(("parallel", "arbitrary"), 52),
        name=name,
    )(a, w)


def _merge_kernel(a_ref, o_ref, wa_ref, wb_ref, ga_ref, gb_ref, out_ref):
    y_a = _dot(a_ref[...], wa_ref[...])
    y_b = _dot(o_ref[...], wb_ref[...])
    out_ref[...] = (ga_ref[...] * y_a + gb_ref[...] * y_b).astype(BF16)


def _merge(a, o, w_a, w_b, gates):
    t, d = a.shape
    n = w_a.shape[1]
    tm, tn = ROW_TILE_MM, 512
    nb = n // tn
    return pl.pallas_call(
        _merge_kernel,
        grid=(nb, t // tm),
        in_specs=[
            pl.BlockSpec((tm, d), lambda j, i: (_serpentine(j, i, t // tm), 0)),
            pl.BlockSpec((tm, d), lambda j, i: (_serpentine(j, i, t // tm), 0)),
            pl.BlockSpec((d, tn), lambda j, i: (0, j)),
            pl.BlockSpec((d, tn), lambda j, i: (0, j)),
            pl.BlockSpec((tm, tn), lambda j, i: (_serpentine(j, i, t // tm), j)),
            pl.BlockSpec((tm, tn), lambda j, i: (_serpentine(j, i, t // tm), j + nb)),
        ],
        out_specs=pl.BlockSpec((tm, tn), lambda j, i: (_serpentine(j, i, t // tm), j)),
        out_shape=jax.ShapeDtypeStruct((t, n), BF16),
        compiler_params=_params(("parallel", "arbitrary"), 60),
        name="merge",
    )(a, o, w_a, w_b, gates, gates)


def _ffn_up_kernel(h_ref, wg_ref, wv_ref, cwg_ref, cwv_ref, bg_ref, bv_ref, o_ref,
                   carry_g_ref, carry_v_ref, wg_bf, wv_bf, raw_g_ref, raw_v_ref, eye_ref,
                   *, tiles_per_seq):
    _cast_weights_at_sweep_start([(wg_ref, wg_bf), (wv_ref, wv_bf)])
    _reset_carry_at_sequence_start([carry_g_ref, carry_v_ref], tiles_per_seq)

    _write_identity_at_sweep_start(eye_ref)
    head = raw_g_ref.shape[0]

    def activation(raw_gate, raw_val):
        u_gate = _causal_conv3(raw_gate, carry_g_ref, cwg_ref) + bg_ref[...]
        u_val = _causal_conv3(raw_val, carry_v_ref, cwv_ref) + bv_ref[...]
        half_gate = 0.5 * u_gate
        t = jnp.tanh(u_gate * (GELU_C0 + GELU_C1 * (u_gate * u_gate)))
        return ((half_gate + half_gate * t) * u_val).astype(BF16)

    h_head = h_ref[0:head, :]
    raw_g_ref[...] = _dot(h_head, wg_bf[...])
    raw_v_ref[...] = _dot(h_head, wv_bf[...])
    act_head = activation(_parked(raw_g_ref), _parked(raw_v_ref))
    h_tail = h_ref[head:, :]
    raw_g_tail = _dot(h_tail, wg_bf[...])
    raw_v_tail = _dot(h_tail, wv_bf[...])
    o_ref[0:head, :] = _through_identity(act_head, eye_ref)
    o_ref[head:, :] = activation(raw_g_tail, raw_v_tail)


def _ffn_up(h, w_up, conv_w, conv_b, seq):
    t, d = h.shape
    d_ff = w_up.shape[1] // 2
    tm, tn = ROW_TILE_MM, 256
    head = _head_rows(tm)
    nb = d_ff // tn
    conv_b = conv_b.reshape(1, 2 * d_ff)
    return pl.pallas_call(
        functools.partial(_ffn_up_kernel, tiles_per_seq=seq // tm),
        grid=(nb, t // tm),
        in_specs=[
            pl.BlockSpec((tm, d), lambda j, i: (i, 0)),
            pl.BlockSpec((d, tn), lambda j, i: (0, j)),
            pl.BlockSpec((d, tn), lambda j, i: (0, j + nb)),
            pl.BlockSpec((SHORT_K, tn), lambda j, i: (0, j)),
            pl.BlockSpec((SHORT_K, tn), lambda j, i: (0, j + nb)),
            pl.BlockSpec((1, tn), lambda j, i: (0, j)),
            pl.BlockSpec((1, tn), lambda j, i: (0, j + nb)),
        ],
        out_specs=pl.BlockSpec((tm, tn), lambda j, i: (i, j)),
        out_shape=jax.ShapeDtypeStruct((t, d_ff), BF16),
        scratch_shapes=([pltpu.VMEM((CARRY_ROWS, tn), F32)] * 2 + [pltpu.VMEM((d, tn), BF16)] * 2
                        + [pltpu.VMEM((head, tn), F32)] * 2 + [pltpu.VMEM((tn, tn), BF16)]),
        compiler_params=_params(("parallel", "arbitrary"), 48),
        name="ffn_up",
    )(h, w_up, w_up, conv_w, conv_w, conv_b, conv_b)


def _ple_kernel(h_ref, p_ref, wg_ref, wp_ref, o_ref, wg_bf, wp_bf):
    _cast_weights_at_sweep_start([(wg_ref, wg_bf), (wp_ref, wp_bf)])
    gate = _sigmoid(_dot(h_ref[...], wg_bf[...]))
    o_ref[...] = (gate * _dot(p_ref[...].astype(BF16), wp_bf[...])).astype(BF16)


def _ple(h, p, w_gate, w_proj):
    t, d = h.shape
    n = w_gate.shape[1]
    ple = p.shape[1]
    tm, tn = ROW_TILE_MM, 512
    return pl.pallas_call(
        _ple_kernel,
        grid=(n // tn, t // tm),
        in_specs=[
            pl.BlockSpec((tm, d), lambda j, i: (_serpentine(j, i, t // tm), 0)),
            pl.BlockSpec((tm, ple), lambda j, i: (_serpentine(j, i, t // tm), 0)),
            pl.BlockSpec((d, tn), lambda j, i: (0, j)),
            pl.BlockSpec((ple, tn), lambda j, i: (0, j)),
        ],
        out_specs=pl.BlockSpec((tm, tn), lambda j, i: (_serpentine(j, i, t // tm), j)),
        out_shape=jax.ShapeDtypeStruct((t, n), BF16),
        scratch_shapes=[pltpu.VMEM((d, tn), BF16), pltpu.VMEM((ple, tn), BF16)],
        compiler_params=_params(("parallel", "arbitrary"), 48),
        name="ple",
    )(h, p, w_gate, w_proj)


def _split_bf16x3(x):
    hi = x.astype(BF16)
    rest = x - hi.astype(F32)
    mid = rest.astype(BF16)
    lo = (rest - mid.astype(F32)).astype(BF16)
    return hi, mid, lo


def _forget_cumsum_kernel(h_ref, wf_ref, b_ref, spread_ref, ones_ref, c_ref, augk_ref, carry_ref,
                          wf_bf, *, tiles_per_seq):
    @pl.when(pl.program_id(0) == 0)
    def _():
        wf_bf[...] = wf_ref[...].astype(BF16)

    @pl.when(pl.program_id(0) % tiles_per_seq == 0)
    def _():
        carry_ref[...] = jnp.zeros_like(carry_ref)

    ts = h_ref.shape[0]
    nh = c_ref.shape[1]
    lane = lax.broadcasted_iota(jnp.int32, (ts, LANES), 1)
    log_f = jnp.where(lane < nh, jax.nn.log_sigmoid(_dot_nt(h_ref[...], wf_bf[...]) + b_ref[...]), 0.0)
    row = lax.broadcasted_iota(jnp.int32, (ts, ts), 0)
    col = lax.broadcasted_iota(jnp.int32, (ts, ts), 1)
    lower = (col <= row).astype(F32)
    csum = jnp.dot(lower, log_f, preferred_element_type=F32,
                   precision=lax.Precision.HIGHEST) + carry_ref[...]
    c_ref[...] = csum[:, 0:nh]
    carry_ref[...] = csum[ts - 1:ts, :]

    c3 = jnp.where(lane < nh, csum,
                   jnp.where(lane < 2 * nh, pltpu.roll(csum, nh, 1), pltpu.roll(csum, 2 * nh, 1)))
    hi, mid, lo = _split_bf16x3(c3 * LOG2_E)
    pieces = jnp.where(lane < nh, hi.astype(F32),
                       jnp.where(lane < 2 * nh, mid.astype(F32), lo.astype(F32))).astype(BF16)
    augk_ref[...] = (_dot(pieces, spread_ref[...]) + ones_ref[...]).astype(BF16)


def _forget_cumsum(h, w_in_t, row_start, bias, seq):
    t, d = h.shape
    nh = bias.shape[0]
    ts = CUMSUM_TILE
    assert row_start % LANES == 0 and 3 * nh <= LANES
    width = nh * HEAD_DIM
    lane = jnp.arange(width)[None, :]
    piece_row = jnp.arange(LANES)[:, None]
    spread = jnp.where((piece_row < 3 * nh)
                       & (lane == (piece_row % nh) * HEAD_DIM + piece_row // nh), -1.0, 0.0)
    ones = jnp.where((lane % HEAD_DIM >= 3) & (lane % HEAD_DIM < 6), 1.0, 0.0).astype(F32)
    bias_tile = jnp.zeros((1, LANES), F32).at[0, 0:nh].set(bias)
    return pl.pallas_call(
        functools.partial(_forget_cumsum_kernel, tiles_per_seq=seq // ts),
        grid=(t // ts,),
        in_specs=[pl.BlockSpec((ts, d), lambda i: (i, 0)),
                  pl.BlockSpec((LANES, d), lambda i: (row_start // LANES, 0)),
                  pl.BlockSpec((1, LANES), lambda i: (0, 0)),
                  pl.BlockSpec((LANES, width), lambda i: (0, 0)),
                  pl.BlockSpec((1, width), lambda i: (0, 0))],
        out_specs=[pl.BlockSpec((ts, nh), lambda i: (i, 0)),
                   pl.BlockSpec((ts, width), lambda i: (i, 0))],
        out_shape=[jax.ShapeDtypeStruct((t, nh), F32), jax.ShapeDtypeStruct((t, width), BF16)],
        scratch_shapes=[pltpu.VMEM((1, LANES), F32), pltpu.VMEM((LANES, d), BF16)],
        compiler_params=_params(("arbitrary",), 48),
        name="forget_cumsum",
    )(h, w_in_t, bias_tile, spread.astype(BF16), ones)


AUG_ROWS = 16
SUM_ROWS = 16
HEADS_PER_STEP = 2


def _attention_kernel(q_ref, k_ref, v_ref, augk_ref, c_ref, o_ref,
                      kx_ref, vt_ref, qxt_ref, m_ref, *buffers, tile):
    n = HEADS_PER_STEP
    buffers = [buf.at[:, 0:tile] for buf in buffers]
    s_ref = [buffers[2 * g:2 * g + 2] for g in range(n)]
    p_ref = [buffers[2 * n + 2 * g:2 * n + 2 * g + 2] for g in range(n)]
    acc_ref = buffers[4 * n:]
    kt = tile // 2
    qi = pl.program_id(2)
    n_key_tiles = kx_ref.shape[1]
    heads = range(HEADS_PER_STEP)

    def lanes(g):
        return slice(g * HEAD_DIM, (g + 1) * HEAD_DIM)

    @pl.when(qi == 0)
    def _():
        def stage(j, _):
            rows = pl.ds(pl.multiple_of(j * kt, kt), kt)
            for g in heads:
                kx_ref[g, j, :, 0:HEAD_DIM] = k_ref[rows, lanes(g)]
                kx_ref[g, j, :, HEAD_DIM:] = augk_ref[rows, lanes(g)]
                vt_ref[g, j, 0:HEAD_DIM, :] = v_ref[rows, lanes(g)].astype(F32).T.astype(BF16)
                vt_ref[g, j, HEAD_DIM:, :] = jnp.ones((SUM_ROWS, kt), BF16)
            return 0
        lax.fori_loop(0, n_key_tiles, stage, 0)
        for g in heads:
            qxt_ref[g, HEAD_DIM + AUG_ROWS:, :] = jnp.zeros((HEAD_DIM - AUG_ROWS, tile), BF16)

    r = lax.broadcasted_iota(jnp.int32, (AUG_ROWS, tile), 0)
    for g in heads:
        qxt_ref[g, 0:HEAD_DIM, :] = q_ref[:, lanes(g)].astype(F32).T.astype(BF16)
        hi, mid, lo = _split_bf16x3(c_ref[g, qi] * LOG2_E)
        aug = jnp.where(r < 3, 1.0,
                        jnp.where(r == 3, hi.astype(F32),
                                  jnp.where(r == 4, mid.astype(F32),
                                            jnp.where(r == 5, lo.astype(F32), 0.0))))
        qxt_ref[g, HEAD_DIM:HEAD_DIM + AUG_ROWS, :] = aug.astype(BF16)

    def scores_into(g, slot, ki):
        s = _dot(kx_ref[g, ki], qxt_ref[g])
        s_ref[g][slot][...] = s
        return jnp.max(s, axis=0, keepdims=True)

    def softmax_into(g, slot, m, tile_max):
        m_new = jnp.maximum(m, tile_max)
        p_ref[g][slot][...] = jnp.exp2(s_ref[g][slot][...] - m_new).astype(BF16)
        return m_new, jnp.exp2(m - m_new)

    def accumulate(g, slot, alpha, ki):
        acc_ref[g][...] = alpha * acc_ref[g][...] + _dot(vt_ref[g, ki], p_ref[g][slot][...])

    for g in heads:
        p_ref[g][1][...] = jnp.zeros((kt, tile), BF16)
        acc_ref[g][...] = jnp.zeros(acc_ref[g].shape, F32)
    max_0 = tuple(scores_into(g, 0, 0) for g in heads)

    def pair(jj, carry):
        m, alpha_1, max_0 = carry
        k0 = 2 * jj
        max_1 = [scores_into(g, 1, k0 + 1) for g in heads]
        m, alpha_0 = zip(*[softmax_into(g, 0, m[g], max_0[g]) for g in heads])
        for g in heads:
            accumulate(g, 1, alpha_1[g], jnp.maximum(k0 - 1, 0))
        max_0 = tuple(scores_into(g, 0, k0 + 2) for g in heads)
        m, alpha_1 = zip(*[softmax_into(g, 1, m[g], max_1[g]) for g in heads])
        for g in heads:
            accumulate(g, 0, alpha_0[g], k0)
        return m, alpha_1, max_0

    init = (tuple(jnp.full((1, tile), NEG_INF, F32) for _ in heads),
            tuple(jnp.ones((1, tile), F32) for _ in heads), max_0)
    m, alpha_1, _ = lax.fori_loop(0, qi, pair, init)

    k0 = 2 * qi
    key = lax.broadcasted_iota(jnp.int32, (kt, tile), 0)
    query = lax.broadcasted_iota(jnp.int32, (kt, tile), 1)
    upper = slice(kt, tile)

    def upper_scores_into(g):
        s = _dot(kx_ref[g, k0 + 1], qxt_ref[g, :, upper])
        s = jnp.where(lax.broadcasted_iota(jnp.int32, (kt, kt), 0)
                      <= lax.broadcasted_iota(jnp.int32, (kt, kt), 1), s, NEG_INF)
        s_ref[g][1][:, 0:kt] = s
        return jnp.max(s, axis=0, keepdims=True)

    max_1 = [upper_scores_into(g) for g in heads]
    for g in heads:
        s_ref[g][0][...] = jnp.where(key <= query, s_ref[g][0][...], NEG_INF)
    max_0 = [jnp.max(s_ref[g][0][...], axis=0, keepdims=True) for g in heads]
    m, alpha_0 = zip(*[softmax_into(g, 0, m[g], max_0[g]) for g in heads])
    for g in heads:
        accumulate(g, 1, alpha_1[g], jnp.maximum(k0 - 1, 0))
    alpha_upper = []
    for g in heads:
        m_ref[g, 0:1, :] = m[g]
        m_old = m_ref[g, 0:1, upper]
        m_upper = jnp.maximum(m_old, max_1[g])
        alpha_upper.append(jnp.exp2(m_old - m_upper))
        p_ref[g][1][:, 0:kt] = jnp.exp2(s_ref[g][1][:, 0:kt] - m_upper).astype(BF16)
    for g in heads:
        accumulate(g, 0, alpha_0[g], k0)
    for g in heads:
        acc_ref[g][:, upper] = (alpha_upper[g] * acc_ref[g][:, upper]
                                + _dot(vt_ref[g, k0 + 1], p_ref[g][1][:, 0:kt]))
    for g in heads:
        o_t = acc_ref[g][0:HEAD_DIM, :] / acc_ref[g][HEAD_DIM:HEAD_DIM + 1, :]
        o_ref[:, lanes(g)] = o_t.T.astype(BF16)


def _attention(qkv, augk, c, batch, seq):
    t = qkv.shape[0]
    tile = ATTN_TILE
    kt = tile // 2
    nq = seq // tile
    g = HEADS_PER_STEP
    width = g * HEAD_DIM
    groups = N_HEADS // g
    c_rows = jnp.transpose(c.reshape(batch, seq, N_HEADS), (0, 2, 1)).reshape(
        batch * N_HEADS, nq, 1, tile)
    return pl.pallas_call(
        functools.partial(_attention_kernel, tile=tile),
        grid=(batch, groups, nq),
        in_specs=[
            pl.BlockSpec((tile, width), lambda b, h, i: (b * nq + i, h)),
            pl.BlockSpec((seq, width), lambda b, h, i: (b, groups + h)),
            pl.BlockSpec((seq, width), lambda b, h, i: (b, 2 * groups + h)),
            pl.BlockSpec((seq, width), lambda b, h, i: (b, h)),
            pl.BlockSpec((g, nq, 1, tile), lambda b, h, i: (b * groups + h, 0, 0, 0)),
        ],
        out_specs=pl.BlockSpec((tile, width), lambda b, h, i: (b * nq + i, h)),
        out_shape=jax.ShapeDtypeStruct((t, N_HEADS * HEAD_DIM), BF16),
        scratch_shapes=[pltpu.VMEM((g, seq // kt, kt, 2 * HEAD_DIM), BF16),
                        pltpu.VMEM((g, seq // kt, HEAD_DIM + SUM_ROWS, kt), BF16),
                        pltpu.VMEM((g, 2 * HEAD_DIM, tile), BF16),
                        pltpu.VMEM((g, 8, tile), F32),
                        *[pltpu.VMEM((kt, tile + LANES), F32)] * (2 * g),
                        *[pltpu.VMEM((kt, tile + LANES), BF16)] * (2 * g),
                        *[pltpu.VMEM((HEAD_DIM + SUM_ROWS, tile + LANES), F32)] * g],
        compiler_params=_params(("parallel", "parallel", "arbitrary"), 60),
        name="fox_attention",
    )(qkv, qkv, qkv, augk, c_rows)


def kernel(x, p, norm_mix_pre, w_in, forget_bias, conv_mix_w, w_branch_conv, w_branch_attn, w_out,
           norm_mix_post, norm_ffn_pre, w_up, ffn_conv_w, ffn_conv_b, w_down, norm_ffn_post,
           w_ple_proj, norm_ple_gate, w_ple_gate, norm_ple_post):
    batch, seq, d = x.shape
    t = batch * seq
    depth = w_in.shape[0]
    conv_width = conv_mix_w.shape[-1]
    attn_width = N_HEADS * HEAD_DIM
    qkv_start = 3 * conv_width
    forget_start = qkv_start + 3 * attn_width
    gate_start = forget_start + N_HEADS

    xs = x.reshape(t, d)
    for i in range(depth):
        w_in_t = jnp.transpose(w_in[i])

        h1 = _norm(xs, norm_mix_pre[i])
        a = _conv_branch(h1, w_in_t, conv_mix_w[i], seq)
        qkv = _qkv(h1, w_in_t, qkv_start, 3 * attn_width)
        gates = _gates(h1, w_in_t, gate_start, 2 * d)
        c, augk = _forget_cumsum(h1, w_in_t, forget_start, forget_bias[i], seq)
        o = _attention(qkv, augk, c, batch, seq)
        merged = _merge(a, o, w_branch_conv[i].astype(BF16), w_branch_attn[i].astype(BF16), gates)
        y1 = _matmul(merged, w_out[i], ROW_TILE_MM, 512, "out_proj")
        x1, h2 = _residual_norm(xs, y1, norm_mix_post[i], norm_ffn_pre[i])

        ffn = _ffn_up(h2, w_up[i], ffn_conv_w[i], ffn_conv_b[i], seq)
        y2 = _matmul(ffn, w_down[i].astype(BF16), ROW_TILE_WIDE_K, 512, "ffn_down")
        x2, h3 = _residual_norm(x1, y2, norm_ffn_post[i], norm_ple_gate[i])

        ge = _ple(h3, p[i].reshape(t, -1), w_ple_gate[i], w_ple_proj[i])
        xs = _residual(x2, ge, norm_ple_post[i])
    return xs.reshape(batch, seq, d)
```

```python
import functools
import math

import jax
import jax.numpy as jnp
from jax import lax
from jax.experimental import pallas as pl
from jax.experimental.pallas import tpu as pltpu

BF16 = jnp.bfloat16
F32 = jnp.float32

LANES = 128
BF16_SUBLANES = 16
N_HEADS = 32
HEAD_DIM = 128
SHORT_K = 3
EPS = 1e-6
NEG_INF = -1e30
LOG2_E = math.log2(math.e)
GELU_C0 = math.sqrt(2.0 / math.pi)
GELU_C1 = GELU_C0 * 0.044715
MIB = 1024 * 1024

ROW_TILE_NORM = 256
ROW_TILE_MM = 1024
ROW_TILE_WIDE_K = 512
ATTN_TILE = 1024
CUMSUM_TILE = 512
CARRY_ROWS = 8


def _params(semantics, vmem_mib):
    return pltpu.CompilerParams(dimension_semantics=semantics, vmem_limit_bytes=vmem_mib * MIB)


def _dot(a, b):
    return jnp.dot(a, b, preferred_element_type=F32)


def _dot_nt(a, b):
    return lax.dot_general(a, b, (((1,), (1,)), ((), ())), preferred_element_type=F32)


def _sigmoid(x):
    return 0.5 * jnp.tanh(0.5 * x) + 0.5


def _rms_scale(x, g):
    inv = lax.rsqrt(jnp.mean(x * x, axis=-1, keepdims=True) + EPS)
    return x * inv * g


def _norm_kernel(x_ref, g_ref, h_ref):
    h_ref[...] = _rms_scale(x_ref[...], g_ref[...]).astype(BF16)


def _residual_norm_kernel(x_ref, y_ref, g_post_ref, g_pre_ref, x_out_ref, h_ref):
    x_new = x_ref[...] + _rms_scale(y_ref[...].astype(F32), g_post_ref[...])
    x_out_ref[...] = x_new
    h_ref[...] = _rms_scale(x_new, g_pre_ref[...]).astype(BF16)


def _residual_kernel(x_ref, y_ref, g_post_ref, x_out_ref):
    x_out_ref[...] = x_ref[...] + _rms_scale(y_ref[...].astype(F32), g_post_ref[...])


def _row_spec(tr, d):
    return pl.BlockSpec((tr, d), lambda i: (i, 0))


def _gain_spec(d):
    return pl.BlockSpec((1, d), lambda i: (0, 0))


def _norm(x, g):
    t, d = x.shape
    tr = ROW_TILE_NORM
    return pl.pallas_call(
        _norm_kernel,
        grid=(t // tr,),
        in_specs=[_row_spec(tr, d), _gain_spec(d)],
        out_specs=_row_spec(tr, d),
        out_shape=jax.ShapeDtypeStruct((t, d), BF16),
        compiler_params=_params(("parallel",), 40),
        name="norm",
    )(x, g.reshape(1, d))


def _residual_norm(x, y, g_post, g_pre):
    t, d = x.shape
    tr = ROW_TILE_NORM
    return pl.pallas_call(
        _residual_norm_kernel,
        grid=(t // tr,),
        in_specs=[_row_spec(tr, d), _row_spec(tr, d), _gain_spec(d), _gain_spec(d)],
        out_specs=[_row_spec(tr, d), _row_spec(tr, d)],
        out_shape=[jax.ShapeDtypeStruct((t, d), F32), jax.ShapeDtypeStruct((t, d), BF16)],
        compiler_params=_params(("parallel",), 40),
        name="residual_norm",
    )(x, y, g_post.reshape(1, d), g_pre.reshape(1, d))


def _residual(x, y, g_post):
    t, d = x.shape
    tr = ROW_TILE_NORM
    return pl.pallas_call(
        _residual_kernel,
        grid=(t // tr,),
        in_specs=[_row_spec(tr, d), _row_spec(tr, d), _gain_spec(d)],
        out_specs=_row_spec(tr, d),
        out_shape=jax.ShapeDtypeStruct((t, d), F32),
        compiler_params=_params(("parallel",), 40),
        name="residual",
    )(x, y, g_post.reshape(1, d))


def _causal_conv3(u, carry_ref, w_ref):
    tm = u.shape[0]
    w0, w1, w2 = w_ref[0:1, :], w_ref[1:2, :], w_ref[2:3, :]
    y = w0 * pltpu.roll(u, 2, 0) + w1 * pltpu.roll(u, 1, 0) + w2 * u
    head = u[0:CARRY_ROWS, :]
    prev = carry_ref[...]
    r = lax.broadcasted_iota(jnp.int32, head.shape, 0)
    back1 = jnp.where(r < 1, pltpu.roll(prev, 1, 0), pltpu.roll(head, 1, 0))
    back2 = jnp.where(r < 2, pltpu.roll(prev, 2, 0), pltpu.roll(head, 2, 0))
    y_head = w0 * back2 + w1 * back1 + w2 * head
    carry_ref[...] = u[tm - CARRY_ROWS:, :]
    return jnp.concatenate([y_head, y[CARRY_ROWS:, :]], axis=0)


def _reset_carry_at_sequence_start(carry_refs, tiles_per_seq):
    @pl.when(pl.program_id(1) % tiles_per_seq == 0)
    def _():
        for ref in carry_refs:
            ref[...] = jnp.zeros_like(ref)


def _serpentine(j, i, n_tiles):
    return jnp.where(j % 2 == 0, i, n_tiles - 1 - i)


def _cast_weights_at_sweep_start(pairs):
    @pl.when(pl.program_id(1) == 0)
    def _():
        for w_ref, w_bf16_ref in pairs:
            w_bf16_ref[...] = w_ref[...].astype(BF16)


FFN_UP_PARKED_PARTS, FFN_UP_PART_DEN = (12,), 16


def _write_identity_at_sweep_start(eye_ref):
    @pl.when(pl.program_id(1) == 0)
    def _():
        n = eye_ref.shape[0]
        eye_ref[...] = (lax.broadcasted_iota(jnp.int32, (n, n), 0)
                        == lax.broadcasted_iota(jnp.int32, (n, n), 1)).astype(BF16)


def _parked(ref):
    rows = pl.ds(pl.multiple_of(jnp.minimum(pl.program_id(1), 0), CARRY_ROWS), ref.shape[0])
    return ref[rows, :]


def _through_identity(x, eye_ref):
    return _dot(x, eye_ref[...]).astype(x.dtype)


def _conv_branch_kernel(h_ref, wb_ref, wc_ref, wv_ref, cw_ref, o_ref, carry_ref,
                        wb_bf, wc_bf, wv_bf, *, tiles_per_seq):
    _cast_weights_at_sweep_start([(wb_ref, wb_bf), (wc_ref, wc_bf), (wv_ref, wv_bf)])
    _reset_carry_at_sequence_start([carry_ref], tiles_per_seq)
    h = h_ref[...]
    u = _dot_nt(h, wc_bf[...]) * _dot_nt(h, wv_bf[...])
    y = _causal_conv3(u, carry_ref, cw_ref)
    o_ref[...] = (_dot_nt(h, wb_bf[...]) * y).astype(BF16)


def _conv_branch(h, w_in_t, conv_w, seq):
    t, d = h.shape
    width = conv_w.shape[1]
    tm, tn = ROW_TILE_MM, 256
    nb = width // tn
    return pl.pallas_call(
        functools.partial(_conv_branch_kernel, tiles_per_seq=seq // tm),
        grid=(nb, t // tm),
        in_specs=[
            pl.BlockSpec((tm, d), lambda j, i: (i, 0)),
            pl.BlockSpec((tn, d), lambda j, i: (j, 0)),
            pl.BlockSpec((tn, d), lambda j, i: (j + nb, 0)),
            pl.BlockSpec((tn, d), lambda j, i: (j + 2 * nb, 0)),
            pl.BlockSpec((SHORT_K, tn), lambda j, i: (0, j)),
        ],
        out_specs=pl.BlockSpec((tm, tn), lambda j, i: (i, j)),
        out_shape=jax.ShapeDtypeStruct((t, width), BF16),
        scratch_shapes=[pltpu.VMEM((CARRY_ROWS, tn), F32)] + [pltpu.VMEM((tn, d), BF16)] * 3,
        compiler_params=_params(("parallel", "arbitrary"), 60),
        name="conv_branch",
    )(h, w_in_t, w_in_t, w_in_t, conv_w)


def _qkv_kernel(h_ref, w_ref, o_ref, w_bf, *, q_tiles, scale):
    _cast_weights_at_sweep_start([(w_ref, w_bf)])
    acc = _dot_nt(h_ref[...], w_bf[...])
    factor = jnp.where(pl.program_id(0) < q_tiles, scale, 1.0).astype(F32)
    o_ref[...] = (acc * factor).astype(BF16)


def _qkv(h, w_in_t, row_start, n):
    t, d = h.shape
    tm, tn = ROW_TILE_MM, 512
    first = row_start // tn
    kern = functools.partial(_qkv_kernel, q_tiles=(n // 3) // tn, scale=LOG2_E / math.sqrt(HEAD_DIM))
    return pl.pallas_call(
        kern,
        grid=(n // tn, t // tm),
        in_specs=[pl.BlockSpec((tm, d), lambda j, i: (_serpentine(j, i, t // tm), 0)),
                  pl.BlockSpec((tn, d), lambda j, i: (first + j, 0))],
        out_specs=pl.BlockSpec((tm, tn), lambda j, i: (_serpentine(j, i, t // tm), j)),
        out_shape=jax.ShapeDtypeStruct((t, n), BF16),
        scratch_shapes=[pltpu.VMEM((tn, d), BF16)],
        compiler_params=_params(("parallel", "arbitrary"), 48),
        name="qkv",
    )(h, w_in_t)


def _gates_kernel(h_ref, w_ref, w_next_ref, wa_ref, wb_ref, o_ref, wa_bf_ref, wb_bf_ref, w_bf,
                  *, shift):
    wa_bf_ref[...] = wa_ref[...].astype(BF16)
    wb_bf_ref[...] = wb_ref[...].astype(BF16)

    @pl.when(pl.program_id(1) == 0)
    def _():
        tn = w_bf.shape[0]
        w_bf[0:tn - shift, :] = w_ref[shift:tn, :].astype(BF16)
        w_bf[tn - shift:tn, :] = w_next_ref[0:shift, :].astype(BF16)

    o_ref[...] = _sigmoid(_dot_nt(h_ref[...], w_bf[...])).astype(BF16)


def _gates(h, w_in_t, row_start, n, w_a, w_b):
    t, d = h.shape
    tm, tn = ROW_TILE_MM, 512
    shift = row_start % LANES
    aligned = row_start - shift
    assert 0 < shift and shift % BF16_SUBLANES == 0 and aligned % tn == 0
    mt = t // tm
    steps = (n // tn) * mt
    slab = w_a.shape[0] // steps
    assert w_a.shape == w_b.shape and slab * steps == w_a.shape[0] and slab % BF16_SUBLANES == 0
    slab_spec = pl.BlockSpec((slab, w_a.shape[1]), lambda j, i: (j * mt + i, 0))
    return pl.pallas_call(
        functools.partial(_gates_kernel, shift=shift),
        grid=(n // tn, mt),
        in_specs=[pl.BlockSpec((tm, d), lambda j, i: (_serpentine(j, i, mt), 0)),
                  pl.BlockSpec((tn, d), lambda j, i: (aligned // tn + j, 0)),
                  pl.BlockSpec((LANES, d), lambda j, i: ((aligned + (j + 1) * tn) // LANES, 0)),
                  slab_spec, slab_spec],
        out_specs=[pl.BlockSpec((tm, tn), lambda j, i: (_serpentine(j, i, mt), j)),
                   slab_spec, slab_spec],
        out_shape=[jax.ShapeDtypeStruct((t, n), BF16), jax.ShapeDtypeStruct(w_a.shape, BF16),
                   jax.ShapeDtypeStruct(w_b.shape, BF16)],
        scratch_shapes=[pltpu.VMEM((tn, d), BF16)],
        compiler_params=_params(("parallel", "arbitrary"), 48),
        name="gates",
    )(h, w_in_t, w_in_t, w_a, w_b)


def _matmul_kernel(a_ref, w_ref, o_ref):
    o_ref[...] = _dot(a_ref[...], w_ref[...]).astype(BF16)


def _matmul_f32_weights_kernel(a_ref, w_ref, o_ref, w_bf):
    _cast_weights_at_sweep_start([(w_ref, w_bf)])
    o_ref[...] = _dot(a_ref[...], w_bf[...]).astype(BF16)


def _matmul(a, w, tm, tn, name):
    t, k = a.shape
    n = w.shape[1]
    cast_in_kernel = w.dtype == F32
    return pl.pallas_call(
        _matmul_f32_weights_kernel if cast_in_kernel else _matmul_kernel,
        grid=(n // tn, t // tm),
        in_specs=[pl.BlockSpec((tm, k), lambda j, i: (_serpentine(j, i, t // tm), 0)),
                  pl.BlockSpec((k, tn), lambda j, i: (0, j))],
        out_specs=pl.BlockSpec((tm, tn), lambda j, i: (_serpentine(j, i, t // tm), j)),
        out_shape=jax.ShapeDtypeStruct((t, n), BF16),
        scratch_shapes=[pltpu.VMEM((k, tn), BF16)] if cast_in_kernel else [],
        compiler_params=_params(("parallel", "arbitrary"), 52),
        name=name,
    )(a, w)


def _merge_kernel(a_ref, o_ref, wa_ref, wb_ref, ga_ref, gb_ref, out_ref):
    y_a = _dot(a_ref[...], wa_ref[...])
    y_b = _dot(o_ref[...], wb_ref[...])
    out_ref[...] = (ga_ref[...] * y_a + gb_ref[...] * y_b).astype(BF16)


def _merge(a, o, w_a, w_b, gates):
    t, d = a.shape
    n = w_a.shape[1]
    tm, tn = ROW_TILE_MM, 512
    nb = n // tn
    return pl.pallas_call(
        _merge_kernel,
        grid=(nb, t // tm),
        in_specs=[
            pl.BlockSpec((tm, d), lambda j, i: (_serpentine(j, i, t // tm), 0)),
            pl.BlockSpec((tm, d), lambda j, i: (_serpentine(j, i, t // tm), 0)),
            pl.BlockSpec((d, tn), lambda j, i: (0, j)),
            pl.BlockSpec((d, tn), lambda j, i: (0, j)),
            pl.BlockSpec((tm, tn), lambda j, i: (_serpentine(j, i, t // tm), j)),
            pl.BlockSpec((tm, tn), lambda j, i: (_serpentine(j, i, t // tm), j + nb)),
        ],
        out_specs=pl.BlockSpec((tm, tn), lambda j, i: (_serpentine(j, i, t // tm), j)),
        out_shape=jax.ShapeDtypeStruct((t, n), BF16),
        compiler_params=_params(("parallel", "arbitrary"), 60),
        name="merge",
    )(a, o, w_a, w_b, gates, gates)


def _ffn_up_kernel(h_ref, wg_ref, wv_ref, cwg_ref, cwv_ref, bg_ref, bv_ref, wd_ref, o_ref, wd_bf_ref,
                   carry_g_ref, carry_v_ref, wg_bf, wv_bf, eye_ref, *parking, tiles_per_seq):
    _cast_weights_at_sweep_start([(wg_ref, wg_bf), (wv_ref, wv_bf)])
    _reset_carry_at_sequence_start([carry_g_ref, carry_v_ref], tiles_per_seq)
    _write_identity_at_sweep_start(eye_ref)
    wd_bf_ref[...] = wd_ref[...].astype(BF16)

    def activation(raw_gate, raw_val):
        u_gate = _causal_conv3(raw_gate, carry_g_ref, cwg_ref) + bg_ref[...]
        u_val = _causal_conv3(raw_val, carry_v_ref, cwv_ref) + bv_ref[...]
        half_gate = 0.5 * u_gate
        t = jnp.tanh(u_gate * (GELU_C0 + GELU_C1 * (u_gate * u_gate)))
        return ((half_gate + half_gate * t) * u_val).astype(BF16)

    start, waiting = 0, None
    for k in range(0, len(parking), 2):
        raw_g_ref, raw_v_ref = parking[k], parking[k + 1]
        rows = slice(start, start + raw_g_ref.shape[0])
        raw_g_ref[...] = _dot(h_ref[rows, :], wg_bf[...])
        raw_v_ref[...] = _dot(h_ref[rows, :], wv_bf[...])
        if waiting is not None:
            o_ref[waiting[0], :] = _through_identity(waiting[1], eye_ref)
        waiting = (rows, activation(_parked(raw_g_ref), _parked(raw_v_ref)))
        start = rows.stop
    h_last = h_ref[start:, :]
    raw_g_last = _dot(h_last, wg_bf[...])
    raw_v_last = _dot(h_last, wv_bf[...])
    o_ref[waiting[0], :] = _through_identity(waiting[1], eye_ref)
    o_ref[start:, :] = activation(raw_g_last, raw_v_last)


def _ffn_up(h, w_up, conv_w, conv_b, seq, w_down):
    t, d = h.shape
    d_ff = w_up.shape[1] // 2
    tm, tn = ROW_TILE_MM, 256
    parked_rows = [tm * num // FFN_UP_PART_DEN for num in FFN_UP_PARKED_PARTS]
    nb = d_ff // tn
    mt = t // tm
    slab = w_down.shape[0] // (nb * mt)
    assert slab * nb * mt == w_down.shape[0] and slab % BF16_SUBLANES == 0
    conv_b = conv_b.reshape(1, 2 * d_ff)
    return pl.pallas_call(
        functools.partial(_ffn_up_kernel, tiles_per_seq=seq // tm),
        grid=(nb, mt),
        in_specs=[
            pl.BlockSpec((tm, d), lambda j, i: (i, 0)),
            pl.BlockSpec((d, tn), lambda j, i: (0, j)),
            pl.BlockSpec((d, tn), lambda j, i: (0, j + nb)),
            pl.BlockSpec((SHORT_K, tn), lambda j, i: (0, j)),
            pl.BlockSpec((SHORT_K, tn), lambda j, i: (0, j + nb)),
            pl.BlockSpec((1, tn), lambda j, i: (0, j)),
            pl.BlockSpec((1, tn), lambda j, i: (0, j + nb)),
            pl.BlockSpec((slab, w_down.shape[1]), lambda j, i: (j * mt + i, 0)),
        ],
        out_specs=[pl.BlockSpec((tm, tn), lambda j, i: (i, j)),
                   pl.BlockSpec((slab, w_down.shape[1]), lambda j, i: (j * mt + i, 0))],
        out_shape=[jax.ShapeDtypeStruct((t, d_ff), BF16), jax.ShapeDtypeStruct(w_down.shape, BF16)],
        scratch_shapes=([pltpu.VMEM((CARRY_ROWS, tn), F32)] * 2 + [pltpu.VMEM((d, tn), BF16)] * 2
                        + [pltpu.VMEM((tn, tn), BF16)]
                        + [pltpu.VMEM((rows, tn), F32) for rows in parked_rows for _ in range(2)]),
        compiler_params=_params(("parallel", "arbitrary"), 48),
        name="ffn_up",
    )(h, w_up, w_up, conv_w, conv_w, conv_b, conv_b, w_down)


def _ple_kernel(h_ref, p_ref, wg_ref, wp_ref, o_ref, wg_bf, wp_bf):
    _cast_weights_at_sweep_start([(wg_ref, wg_bf), (wp_ref, wp_bf)])
    gate = _sigmoid(_dot(h_ref[...], wg_bf[...]))
    o_ref[...] = (gate * _dot(p_ref[...].astype(BF16), wp_bf[...])).astype(BF16)


def _ple(h, p, w_gate, w_proj):
    t, d = h.shape
    n = w_gate.shape[1]
    ple = p.shape[1]
    tm, tn = ROW_TILE_MM, 512
    return pl.pallas_call(
        _ple_kernel,
        grid=(n // tn, t // tm),
        in_specs=[
            pl.BlockSpec((tm, d), lambda j, i: (_serpentine(j, i, t // tm), 0)),
            pl.BlockSpec((tm, ple), lambda j, i: (_serpentine(j, i, t // tm), 0)),
            pl.BlockSpec((d, tn), lambda j, i: (0, j)),
            pl.BlockSpec((ple, tn), lambda j, i: (0, j)),
        ],
        out_specs=pl.BlockSpec((tm, tn), lambda j, i: (_serpentine(j, i, t // tm), j)),
        out_shape=jax.ShapeDtypeStruct((t, n), BF16),
        scratch_shapes=[pltpu.VMEM((d, tn), BF16), pltpu.VMEM((ple, tn), BF16)],
        compiler_params=_params(("parallel", "arbitrary"), 48),
        name="ple",
    )(h, p, w_gate, w_proj)


def _split_bf16x3(x):
    hi = x.astype(BF16)
    rest = x - hi.astype(F32)
    mid = rest.astype(BF16)
    lo = (rest - mid.astype(F32)).astype(BF16)
    return hi, mid, lo


def _forget_cumsum_kernel(h_ref, wf_ref, b_ref, spread_ref, ones_ref, c_ref, augk_ref, carry_ref,
                          wf_bf, *, tiles_per_seq):
    @pl.when(pl.program_id(0) == 0)
    def _():
        wf_bf[...] = wf_ref[...].astype(BF16)

    @pl.when(pl.program_id(0) % tiles_per_seq == 0)
    def _():
        carry_ref[...] = jnp.zeros_like(carry_ref)

    ts = h_ref.shape[0]
    nh = c_ref.shape[1]
    lane = lax.broadcasted_iota(jnp.int32, (ts, LANES), 1)
    log_f = jnp.where(lane < nh, jax.nn.log_sigmoid(_dot_nt(h_ref[...], wf_bf[...]) + b_ref[...]), 0.0)
    row = lax.broadcasted_iota(jnp.int32, (ts, ts), 0)
    col = lax.broadcasted_iota(jnp.int32, (ts, ts), 1)
    lower = (col <= row).astype(F32)
    csum = jnp.dot(lower, log_f, preferred_element_type=F32,
                   precision=lax.Precision.HIGHEST) + carry_ref[...]
    c_ref[...] = csum[:, 0:nh]
    carry_ref[...] = csum[ts - 1:ts, :]

    c3 = jnp.where(lane < nh, csum,
                   jnp.where(lane < 2 * nh, pltpu.roll(csum, nh, 1), pltpu.roll(csum, 2 * nh, 1)))
    hi, mid, lo = _split_bf16x3(c3 * LOG2_E)
    pieces = jnp.where(lane < nh, hi.astype(F32),
                       jnp.where(lane < 2 * nh, mid.astype(F32), lo.astype(F32))).astype(BF16)
    augk_ref[...] = (_dot(pieces, spread_ref[...]) + ones_ref[...]).astype(BF16)


def _forget_cumsum(h, w_in_t, row_start, bias, seq):
    t, d = h.shape
    nh = bias.shape[0]
    ts = CUMSUM_TILE
    assert row_start % LANES == 0 and 3 * nh <= LANES
    width = nh * HEAD_DIM
    lane = jnp.arange(width)[None, :]
    piece_row = jnp.arange(LANES)[:, None]
    spread = jnp.where((piece_row < 3 * nh)
                       & (lane == (piece_row % nh) * HEAD_DIM + piece_row // nh), -1.0, 0.0)
    ones = jnp.where((lane % HEAD_DIM >= 3) & (lane % HEAD_DIM < 6), 1.0, 0.0).astype(F32)
    bias_tile = jnp.zeros((1, LANES), F32).at[0, 0:nh].set(bias)
    return pl.pallas_call(
        functools.partial(_forget_cumsum_kernel, tiles_per_seq=seq // ts),
        grid=(t // ts,),
        in_specs=[pl.BlockSpec((ts, d), lambda i: (i, 0)),
                  pl.BlockSpec((LANES, d), lambda i: (row_start // LANES, 0)),
                  pl.BlockSpec((1, LANES), lambda i: (0, 0)),
                  pl.BlockSpec((LANES, width), lambda i: (0, 0)),
                  pl.BlockSpec((1, width), lambda i: (0, 0))],
        out_specs=[pl.BlockSpec((ts, nh), lambda i: (i, 0)),
                   pl.BlockSpec((ts, width), lambda i: (i, 0))],
        out_shape=[jax.ShapeDtypeStruct((t, nh), F32), jax.ShapeDtypeStruct((t, width), BF16)],
        scratch_shapes=[pltpu.VMEM((1, LANES), F32), pltpu.VMEM((LANES, d), BF16)],
        compiler_params=_params(("arbitrary",), 48),
        name="forget_cumsum",
    )(h, w_in_t, bias_tile, spread.astype(BF16), ones)


AUG_ROWS = 16
SUM_ROWS = 16
HEADS_PER_STEP = 2


def _attention_kernel(q_ref, k_ref, v_ref, augk_ref, c_ref, o_ref,
                      kx_ref, vt_ref, qxt_ref, m_ref, *buffers, tile):
    n = HEADS_PER_STEP
    buffers = [buf.at[:, 0:tile] for buf in buffers]
    s_ref = [buffers[2 * g:2 * g + 2] for g in range(n)]
    p_ref = [buffers[2 * n + 2 * g:2 * n + 2 * g + 2] for g in range(n)]
    acc_ref = buffers[4 * n:]
    kt = tile // 2
    qi = pl.program_id(2)
    n_key_tiles = kx_ref.shape[1]
    heads = range(HEADS_PER_STEP)

    def lanes(g):
        return slice(g * HEAD_DIM, (g + 1) * HEAD_DIM)

    @pl.when(qi == 0)
    def _():
        def stage(j, _):
            rows = pl.ds(pl.multiple_of(j * kt, kt), kt)
            for g in heads:
                kx_ref[g, j, :, 0:HEAD_DIM] = k_ref[rows, lanes(g)]
                kx_ref[g, j, :, HEAD_DIM:] = augk_ref[rows, lanes(g)]
                vt_ref[g, j, 0:HEAD_DIM, :] = v_ref[rows, lanes(g)].astype(F32).T.astype(BF16)
                vt_ref[g, j, HEAD_DIM:, :] = jnp.ones((SUM_ROWS, kt), BF16)
            return 0
        lax.fori_loop(0, n_key_tiles, stage, 0)
        for g in heads:
            qxt_ref[g, HEAD_DIM + AUG_ROWS:, :] = jnp.zeros((HEAD_DIM - AUG_ROWS, tile), BF16)

    r = lax.broadcasted_iota(jnp.int32, (AUG_ROWS, tile), 0)
    for g in heads:
        qxt_ref[g, 0:HEAD_DIM, :] = q_ref[:, lanes(g)].astype(F32).T.astype(BF16)
        hi, mid, lo = _split_bf16x3(c_ref[g, qi] * LOG2_E)
        aug = jnp.where(r < 3, 1.0,
                        jnp.where(r == 3, hi.astype(F32),
                                  jnp.where(r == 4, mid.astype(F32),
                                            jnp.where(r == 5, lo.astype(F32), 0.0))))
        qxt_ref[g, HEAD_DIM:HEAD_DIM + AUG_ROWS, :] = aug.astype(BF16)

    def scores_into(g, slot, ki):
        s = _dot(kx_ref[g, ki], qxt_ref[g])
        s_ref[g][slot][...] = s
        return jnp.max(s, axis=0, keepdims=True)

    def softmax_into(g, slot, m, tile_max):
        m_new = jnp.maximum(m, tile_max)
        p_ref[g][slot][...] = jnp.exp2(s_ref[g][slot][...] - m_new).astype(BF16)
        return m_new, jnp.exp2(m - m_new)

    def accumulate(g, slot, alpha, ki):
        acc_ref[g][...] = alpha * acc_ref[g][...] + _dot(vt_ref[g, ki], p_ref[g][slot][...])

    for g in heads:
        p_ref[g][1][...] = jnp.zeros((kt, tile), BF16)
        acc_ref[g][...] = jnp.zeros(acc_ref[g].shape, F32)
    max_0 = tuple(scores_into(g, 0, 0) for g in heads)

    def pair(jj, carry):
        m, alpha_1, max_0 = carry
        k0 = 2 * jj
        max_1 = [scores_into(g, 1, k0 + 1) for g in heads]
        m, alpha_0 = zip(*[softmax_into(g, 0, m[g], max_0[g]) for g in heads])
        for g in heads:
            accumulate(g, 1, alpha_1[g], jnp.maximum(k0 - 1, 0))
        max_0 = tuple(scores_into(g, 0, k0 + 2) for g in heads)
        m, alpha_1 = zip(*[softmax_into(g, 1, m[g], max_1[g]) for g in heads])
        for g in heads:
            accumulate(g, 0, alpha_0[g], k0)
        return m, alpha_1, max_0

    init = (tuple(jnp.full((1, tile), NEG_INF, F32) for _ in heads),
            tuple(jnp.ones((1, tile), F32) for _ in heads), max_0)
    m, alpha_1, _ = lax.fori_loop(0, qi, pair, init)

    k0 = 2 * qi
    key = lax.broadcasted_iota(jnp.int32, (kt, tile), 0)
    query = lax.broadcasted_iota(jnp.int32, (kt, tile), 1)
    upper = slice(kt, tile)

    def upper_scores_into(g):
        s = _dot(kx_ref[g, k0 + 1], qxt_ref[g, :, upper])
        s = jnp.where(lax.broadcasted_iota(jnp.int32, (kt, kt), 0)
                      <= lax.broadcasted_iota(jnp.int32, (kt, kt), 1), s, NEG_INF)
        s_ref[g][1][:, 0:kt] = s
        return jnp.max(s, axis=0, keepdims=True)

    max_1 = [upper_scores_into(g) for g in heads]
    for g in heads:
        s_ref[g][0][...] = jnp.where(key <= query, s_ref[g][0][...], NEG_INF)
    max_0 = [jnp.max(s_ref[g][0][...], axis=0, keepdims=True) for g in heads]
    m, alpha_0 = zip(*[softmax_into(g, 0, m[g], max_0[g]) for g in heads])
    for g in heads:
        accumulate(g, 1, alpha_1[g], jnp.maximum(k0 - 1, 0))
    alpha_upper = []
    for g in heads:
        m_ref[g, 0:1, :] = m[g]
        m_old = m_ref[g, 0:1, upper]
        m_upper = jnp.maximum(m_old, max_1[g])
        alpha_upper.append(jnp.exp2(m_old - m_upper))
        p_ref[g][1][:, 0:kt] = jnp.exp2(s_ref[g][1][:, 0:kt] - m_upper).astype(BF16)
    for g in heads:
        accumulate(g, 0, alpha_0[g], k0)
    for g in heads:
        acc_ref[g][:, upper] = (alpha_upper[g] * acc_ref[g][:, upper]
                                + _dot(vt_ref[g, k0 + 1], p_ref[g][1][:, 0:kt]))
    for g in heads:
        o_t = acc_ref[g][0:HEAD_DIM, :] / acc_ref[g][HEAD_DIM:HEAD_DIM + 1, :]
        o_ref[:, lanes(g)] = o_t.T.astype(BF16)


def _attention(qkv, augk, c, batch, seq):
    t = qkv.shape[0]
    tile = ATTN_TILE
    kt = tile // 2
    nq = seq // tile
    g = HEADS_PER_STEP
    width = g * HEAD_DIM
    groups = N_HEADS // g
    c_rows = jnp.transpose(c.reshape(batch, seq, N_HEADS), (0, 2, 1)).reshape(
        batch * N_HEADS, nq, 1, tile)
    return pl.pallas_call(
        functools.partial(_attention_kernel, tile=tile),
        grid=(batch, groups, nq),
        in_specs=[
            pl.BlockSpec((tile, width), lambda b, h, i: (b * nq + i, h)),
            pl.BlockSpec((seq, width), lambda b, h, i: (b, groups + h)),
            pl.BlockSpec((seq, width), lambda b, h, i: (b, 2 * groups + h)),
            pl.BlockSpec((seq, width), lambda b, h, i: (b, h)),
            pl.BlockSpec((g, nq, 1, tile), lambda b, h, i: (b * groups + h, 0, 0, 0)),
        ],
        out_specs=pl.BlockSpec((tile, width), lambda b, h, i: (b * nq + i, h)),
        out_shape=jax.ShapeDtypeStruct((t, N_HEADS * HEAD_DIM), BF16),
        scratch_shapes=[pltpu.VMEM((g, seq // kt, kt, 2 * HEAD_DIM), BF16),
                        pltpu.VMEM((g, seq // kt, HEAD_DIM + SUM_ROWS, kt), BF16),
                        pltpu.VMEM((g, 2 * HEAD_DIM, tile), BF16),
                        pltpu.VMEM((g, 8, tile), F32),
                        *[pltpu.VMEM((kt, tile + LANES), F32)] * (2 * g),
                        *[pltpu.VMEM((kt, tile + LANES), BF16)] * (2 * g),
                        *[pltpu.VMEM((HEAD_DIM + SUM_ROWS, tile + LANES), F32)] * g],
        compiler_params=_params(("parallel", "parallel", "arbitrary"), 60),
        name="fox_attention",
    )(qkv, qkv, qkv, augk, c_rows)


def kernel(x, p, norm_mix_pre, w_in, forget_bias, conv_mix_w, w_branch_conv, w_branch_attn, w_out,
           norm_mix_post, norm_ffn_pre, w_up, ffn_conv_w, ffn_conv_b, w_down, norm_ffn_post,
           w_ple_proj, norm_ple_gate, w_ple_gate, norm_ple_post):
    batch, seq, d = x.shape
    t = batch * seq
    depth = w_in.shape[0]
    conv_width = conv_mix_w.shape[-1]
    attn_width = N_HEADS * HEAD_DIM
    qkv_start = 3 * conv_width
    forget_start = qkv_start + 3 * attn_width
    gate_start = forget_start + N_HEADS

    xs = x.reshape(t, d)
    for i in range(depth):
        w_in_t = jnp.transpose(w_in[i])

        h1 = _norm(xs, norm_mix_pre[i])
        a = _conv_branch(h1, w_in_t, conv_mix_w[i], seq)
        qkv = _qkv(h1, w_in_t, qkv_start, 3 * attn_width)
        gates, w_a, w_b = _gates(h1, w_in_t, gate_start, 2 * d, w_branch_conv[i], w_branch_attn[i])
        c, augk = _forget_cumsum(h1, w_in_t, forget_start, forget_bias[i], seq)
        o = _attention(qkv, augk, c, batch, seq)
        merged = _merge(a, o, w_a, w_b, gates)
        y1 = _matmul(merged, w_out[i], ROW_TILE_MM, 512, "out_proj")
        x1, h2 = _residual_norm(xs, y1, norm_mix_post[i], norm_ffn_pre[i])

        ffn, w_down_bf16 = _ffn_up(h2, w_up[i], ffn_conv_w[i], ffn_conv_b[i], seq, w_down[i])
        y2 = _matmul(ffn, w_down_bf16, ROW_TILE_WIDE_K, 512, "ffn_down")
        x2, h3 = _residual_norm(x1, y2, norm_ffn_post[i], norm_ple_gate[i])

        ge = _ple(h3, p[i].reshape(t, -1), w_ple_gate[i], w_ple_proj[i])
        xs = _residual(x2, ge, norm_ple_post[i])
    return xs.reshape(batch, seq, d)
```

```python
import functools
import math

import jax
import jax.numpy as jnp
from jax import lax
from jax.experimental import pallas as pl
from jax.experimental.pallas import tpu as pltpu

BF16 = jnp.bfloat16
F32 = jnp.float32

LANES = 128
BF16_SUBLANES = 16
N_HEADS = 32
HEAD_DIM = 128
SHORT_K = 3
EPS = 1e-6
NEG_INF = -1e30
LOG2_E = math.log2(math.e)
GELU_C0 = math.sqrt(2.0 / math.pi)
GELU_C1 = GELU_C0 * 0.044715
MIB = 1024 * 1024

ROW_TILE_NORM = 256
ROW_TILE_MM = 1024
ROW_TILE_WIDE_K = 512
ATTN_TILE = 1024
CUMSUM_TILE = 512
CARRY_ROWS = 8


def _params(semantics, vmem_mib):
    return pltpu.CompilerParams(dimension_semantics=semantics, vmem_limit_bytes=vmem_mib * MIB)


def _dot(a, b):
    return jnp.dot(a, b, preferred_element_type=F32)


def _dot_nt(a, b):
    return lax.dot_general(a, b, (((1,), (1,)), ((), ())), preferred_element_type=F32)


def _sigmoid(x):
    return 0.5 * jnp.tanh(0.5 * x) + 0.5


def _rms_scale(x, g):
    inv = lax.rsqrt(jnp.mean(x * x, axis=-1, keepdims=True) + EPS)
    return x * inv * g


def _residual_norm_kernel(x_ref, y_ref, g_post_ref, g_pre_ref, x_out_ref, h_ref):
    x_new = x_ref[...] + _rms_scale(y_ref[...].astype(F32), g_post_ref[...])
    x_out_ref[...] = x_new
    h_ref[...] = _rms_scale(x_new, g_pre_ref[...]).astype(BF16)


def _residual_kernel(x_ref, y_ref, g_post_ref, x_out_ref):
    x_out_ref[...] = x_ref[...] + _rms_scale(y_ref[...].astype(F32), g_post_ref[...])


def _row_spec(tr, d):
    return pl.BlockSpec((tr, d), lambda i: (i, 0))


def _gain_spec(d):
    return pl.BlockSpec((1, d), lambda i: (0, 0))


def _residual_norm(x, y, g_post, g_pre):
    t, d = x.shape
    tr = ROW_TILE_NORM
    return pl.pallas_call(
        _residual_norm_kernel,
        grid=(t // tr,),
        in_specs=[_row_spec(tr, d), _row_spec(tr, d), _gain_spec(d), _gain_spec(d)],
        out_specs=[_row_spec(tr, d), _row_spec(tr, d)],
        out_shape=[jax.ShapeDtypeStruct((t, d), F32), jax.ShapeDtypeStruct((t, d), BF16)],
        compiler_params=_params(("parallel",), 40),
        name="residual_norm",
    )(x, y, g_post.reshape(1, d), g_pre.reshape(1, d))


def _residual(x, y, g_post):
    t, d = x.shape
    tr = ROW_TILE_NORM
    return pl.pallas_call(
        _residual_kernel,
        grid=(t // tr,),
        in_specs=[_row_spec(tr, d), _row_spec(tr, d), _gain_spec(d)],
        out_specs=_row_spec(tr, d),
        out_shape=jax.ShapeDtypeStruct((t, d), F32),
        compiler_params=_params(("parallel",), 40),
        name="residual",
    )(x, y, g_post.reshape(1, d))


def _causal_conv3(u, carry_ref, w_ref):
    tm = u.shape[0]
    w0, w1, w2 = w_ref[0:1, :], w_ref[1:2, :], w_ref[2:3, :]
    y = w0 * pltpu.roll(u, 2, 0) + w1 * pltpu.roll(u, 1, 0) + w2 * u
    head = u[0:CARRY_ROWS, :]
    prev = carry_ref[...]
    r = lax.broadcasted_iota(jnp.int32, head.shape, 0)
    back1 = jnp.where(r < 1, pltpu.roll(prev, 1, 0), pltpu.roll(head, 1, 0))
    back2 = jnp.where(r < 2, pltpu.roll(prev, 2, 0), pltpu.roll(head, 2, 0))
    y_head = w0 * back2 + w1 * back1 + w2 * head
    carry_ref[...] = u[tm - CARRY_ROWS:, :]
    return jnp.concatenate([y_head, y[CARRY_ROWS:, :]], axis=0)


def _reset_carry_at_sequence_start(carry_refs, tiles_per_seq):
    @pl.when(pl.program_id(1) % tiles_per_seq == 0)
    def _():
        for ref in carry_refs:
            ref[...] = jnp.zeros_like(ref)


def _serpentine(j, i, n_tiles):
    return jnp.where(j % 2 == 0, i, n_tiles - 1 - i)


def _cast_weights_at_sweep_start(pairs):
    @pl.when(pl.program_id(1) == 0)
    def _():
        for w_ref, w_bf16_ref in pairs:
            w_bf16_ref[...] = w_ref[...].astype(BF16)


FFN_UP_PARKED_PARTS, FFN_UP_PART_DEN = (12,), 16


def _write_identity_at_sweep_start(eye_ref):
    @pl.when(pl.program_id(1) == 0)
    def _():
        n = eye_ref.shape[0]
        eye_ref[...] = (lax.broadcasted_iota(jnp.int32, (n, n), 0)
                        == lax.broadcasted_iota(jnp.int32, (n, n), 1)).astype(BF16)


def _parked(ref):
    rows = pl.ds(pl.multiple_of(jnp.minimum(pl.program_id(1), 0), CARRY_ROWS), ref.shape[0])
    return ref[rows, :]


def _through_identity(x, eye_ref):
    return _dot(x, eye_ref[...]).astype(x.dtype)


def _conv_branch_kernel(h_ref, wb_ref, wc_ref, wv_ref, cw_ref, o_ref, carry_ref,
                        wb_bf, wc_bf, wv_bf, *, tiles_per_seq):
    _cast_weights_at_sweep_start([(wb_ref, wb_bf), (wc_ref, wc_bf), (wv_ref, wv_bf)])
    _reset_carry_at_sequence_start([carry_ref], tiles_per_seq)
    h = h_ref[...]
    u = _dot_nt(h, wc_bf[...]) * _dot_nt(h, wv_bf[...])
    y = _causal_conv3(u, carry_ref, cw_ref)
    o_ref[...] = (_dot_nt(h, wb_bf[...]) * y).astype(BF16)


def _conv_branch(h, w_in_t, conv_w, seq):
    t, d = h.shape
    width = conv_w.shape[1]
    tm, tn = ROW_TILE_MM, 256
    nb = width // tn
    return pl.pallas_call(
        functools.partial(_conv_branch_kernel, tiles_per_seq=seq // tm),
        grid=(nb, t // tm),
        in_specs=[
            pl.BlockSpec((tm, d), lambda j, i: (i, 0)),
            pl.BlockSpec((tn, d), lambda j, i: (j, 0)),
            pl.BlockSpec((tn, d), lambda j, i: (j + nb, 0)),
            pl.BlockSpec((tn, d), lambda j, i: (j + 2 * nb, 0)),
            pl.BlockSpec((SHORT_K, tn), lambda j, i: (0, j)),
        ],
        out_specs=pl.BlockSpec((tm, tn), lambda j, i: (i, j)),
        out_shape=jax.ShapeDtypeStruct((t, width), BF16),
        scratch_shapes=[pltpu.VMEM((CARRY_ROWS, tn), F32)] + [pltpu.VMEM((tn, d), BF16)] * 3,
        compiler_params=_params(("parallel", "arbitrary"), 60),
        name="conv_branch",
    )(h, w_in_t, w_in_t, w_in_t, conv_w)


def _qkv_kernel(h_ref, w_ref, o_ref, w_bf, *, q_tiles, scale):
    _cast_weights_at_sweep_start([(w_ref, w_bf)])
    acc = _dot_nt(h_ref[...], w_bf[...])
    factor = jnp.where(pl.program_id(0) < q_tiles, scale, 1.0).astype(F32)
    o_ref[...] = (acc * factor).astype(BF16)


def _qkv(h, w_in_t, row_start, n):
    t, d = h.shape
    tm, tn = ROW_TILE_MM, 512
    first = row_start // tn
    kern = functools.partial(_qkv_kernel, q_tiles=(n // 3) // tn, scale=LOG2_E / math.sqrt(HEAD_DIM))
    return pl.pallas_call(
        kern,
        grid=(n // tn, t // tm),
        in_specs=[pl.BlockSpec((tm, d), lambda j, i: (_serpentine(j, i, t // tm), 0)),
                  pl.BlockSpec((tn, d), lambda j, i: (first + j, 0))],
        out_specs=pl.BlockSpec((tm, tn), lambda j, i: (_serpentine(j, i, t // tm), j)),
        out_shape=jax.ShapeDtypeStruct((t, n), BF16),
        scratch_shapes=[pltpu.VMEM((tn, d), BF16)],
        compiler_params=_params(("parallel", "arbitrary"), 48),
        name="qkv",
    )(h, w_in_t)


def _gates_kernel(h_ref, w_ref, w_next_ref, wa_ref, wb_ref, o_ref, wa_bf_ref, wb_bf_ref, w_bf,
                  *, shift):
    wa_bf_ref[...] = wa_ref[...].astype(BF16)
    wb_bf_ref[...] = wb_ref[...].astype(BF16)

    @pl.when(pl.program_id(1) == 0)
    def _():
        tn = w_bf.shape[0]
        w_bf[0:tn - shift, :] = w_ref[shift:tn, :].astype(BF16)
        w_bf[tn - shift:tn, :] = w_next_ref[0:shift, :].astype(BF16)

    o_ref[...] = _sigmoid(_dot_nt(h_ref[...], w_bf[...])).astype(BF16)


def _gates(h, w_in_t, row_start, n, w_a, w_b):
    t, d = h.shape
    tm, tn = ROW_TILE_MM, 512
    shift = row_start % LANES
    aligned = row_start - shift
    assert 0 < shift and shift % BF16_SUBLANES == 0 and aligned % tn == 0
    mt = t // tm
    steps = (n // tn) * mt
    slab = w_a.shape[0] // steps
    assert w_a.shape == w_b.shape and slab * steps == w_a.shape[0] and slab % BF16_SUBLANES == 0
    slab_spec = pl.BlockSpec((slab, w_a.shape[1]), lambda j, i: (j * mt + i, 0))
    return pl.pallas_call(
        functools.partial(_gates_kernel, shift=shift),
        grid=(n // tn, mt),
        in_specs=[pl.BlockSpec((tm, d), lambda j, i: (_serpentine(j, i, mt), 0)),
                  pl.BlockSpec((tn, d), lambda j, i: (aligned // tn + j, 0)),
                  pl.BlockSpec((LANES, d), lambda j, i: ((aligned + (j + 1) * tn) // LANES, 0)),
                  slab_spec, slab_spec],
        out_specs=[pl.BlockSpec((tm, tn), lambda j, i: (_serpentine(j, i, mt), j)),
                   slab_spec, slab_spec],
        out_shape=[jax.ShapeDtypeStruct((t, n), BF16), jax.ShapeDtypeStruct(w_a.shape, BF16),
                   jax.ShapeDtypeStruct(w_b.shape, BF16)],
        scratch_shapes=[pltpu.VMEM((tn, d), BF16)],
        compiler_params=_params(("parallel", "arbitrary"), 48),
        name="gates",
    )(h, w_in_t, w_in_t, w_a, w_b)


def _matmul_kernel(a_ref, w_ref, o_ref):
    o_ref[...] = _dot(a_ref[...], w_ref[...]).astype(BF16)


def _matmul_f32_weights_kernel(a_ref, w_ref, o_ref, w_bf):
    _cast_weights_at_sweep_start([(w_ref, w_bf)])
    o_ref[...] = _dot(a_ref[...], w_bf[...]).astype(BF16)


def _matmul(a, w, tm, tn, name):
    t, k = a.shape
    n = w.shape[1]
    cast_in_kernel = w.dtype == F32
    return pl.pallas_call(
        _matmul_f32_weights_kernel if cast_in_kernel else _matmul_kernel,
        grid=(n // tn, t // tm),
        in_specs=[pl.BlockSpec((tm, k), lambda j, i: (_serpentine(j, i, t // tm), 0)),
                  pl.BlockSpec((k, tn), lambda j, i: (0, j))],
        out_specs=pl.BlockSpec((tm, tn), lambda j, i: (_serpentine(j, i, t // tm), j)),
        out_shape=jax.ShapeDtypeStruct((t, n), BF16),
        scratch_shapes=[pltpu.VMEM((k, tn), BF16)] if cast_in_kernel else [],
        compiler_params=_params(("parallel", "arbitrary"), 52),
        name=name,
    )(a, w)


def _merge_kernel(a_ref, o_ref, wa_ref, wb_ref, ga_ref, gb_ref, out_ref):
    y_a = _dot(a_ref[...], wa_ref[...])
    y_b = _dot(o_ref[...], wb_ref[...])
    out_ref[...] = (ga_ref[...] * y_a + gb_ref[...] * y_b).astype(BF16)


def _merge(a, o, w_a, w_b, gates):
    t, d = a.shape
    n = w_a.shape[1]
    tm, tn = ROW_TILE_MM, 512
    nb = n // tn
    return pl.pallas_call(
        _merge_kernel,
        grid=(nb, t // tm),
        in_specs=[
            pl.BlockSpec((tm, d), lambda j, i: (_serpentine(j, i, t // tm), 0)),
            pl.BlockSpec((tm, d), lambda j, i: (_serpentine(j, i, t // tm), 0)),
            pl.BlockSpec((d, tn), lambda j, i: (0, j)),
            pl.BlockSpec((d, tn), lambda j, i: (0, j)),
            pl.BlockSpec((tm, tn), lambda j, i: (_serpentine(j, i, t // tm), j)),
            pl.BlockSpec((tm, tn), lambda j, i: (_serpentine(j, i, t // tm), j + nb)),
        ],
        out_specs=pl.BlockSpec((tm, tn), lambda j, i: (_serpentine(j, i, t // tm), j)),
        out_shape=jax.ShapeDtypeStruct((t, n), BF16),
        compiler_params=_params(("parallel", "arbitrary"), 60),
        name="merge",
    )(a, o, w_a, w_b, gates, gates)


def _ffn_up_kernel(h_ref, wg_ref, wv_ref, cwg_ref, cwv_ref, bg_ref, bv_ref, wd_ref, o_ref, wd_bf_ref,
                   carry_g_ref, carry_v_ref, wg_bf, wv_bf, eye_ref, *parking, tiles_per_seq):
    _cast_weights_at_sweep_start([(wg_ref, wg_bf), (wv_ref, wv_bf)])
    _reset_carry_at_sequence_start([carry_g_ref, carry_v_ref], tiles_per_seq)
    _write_identity_at_sweep_start(eye_ref)
    wd_bf_ref[...] = wd_ref[...].astype(BF16)

    def activation(raw_gate, raw_val):
        u_gate = _causal_conv3(raw_gate, carry_g_ref, cwg_ref) + bg_ref[...]
        u_val = _causal_conv3(raw_val, carry_v_ref, cwv_ref) + bv_ref[...]
        half_gate = 0.5 * u_gate
        t = jnp.tanh(u_gate * (GELU_C0 + GELU_C1 * (u_gate * u_gate)))
        return ((half_gate + half_gate * t) * u_val).astype(BF16)

    start, waiting = 0, None
    for k in range(0, len(parking), 2):
        raw_g_ref, raw_v_ref = parking[k], parking[k + 1]
        rows = slice(start, start + raw_g_ref.shape[0])
        raw_g_ref[...] = _dot(h_ref[rows, :], wg_bf[...])
        raw_v_ref[...] = _dot(h_ref[rows, :], wv_bf[...])
        if waiting is not None:
            o_ref[waiting[0], :] = _through_identity(waiting[1], eye_ref)
        waiting = (rows, activation(_parked(raw_g_ref), _parked(raw_v_ref)))
        start = rows.stop
    h_last = h_ref[start:, :]
    raw_g_last = _dot(h_last, wg_bf[...])
    raw_v_last = _dot(h_last, wv_bf[...])
    o_ref[waiting[0], :] = _through_identity(waiting[1], eye_ref)
    o_ref[start:, :] = activation(raw_g_last, raw_v_last)


def _ffn_up(h, w_up, conv_w, conv_b, seq, w_down):
    t, d = h.shape
    d_ff = w_up.shape[1] // 2
    tm, tn = ROW_TILE_MM, 256
    parked_rows = [tm * num // FFN_UP_PART_DEN for num in FFN_UP_PARKED_PARTS]
    nb = d_ff // tn
    mt = t // tm
    slab = w_down.shape[0] // (nb * mt)
    assert slab * nb * mt == w_down.shape[0] and slab % BF16_SUBLANES == 0
    conv_b = conv_b.reshape(1, 2 * d_ff)
    return pl.pallas_call(
        functools.partial(_ffn_up_kernel, tiles_per_seq=seq // tm),
        grid=(nb, mt),
        in_specs=[
            pl.BlockSpec((tm, d), lambda j, i: (i, 0)),
            pl.BlockSpec((d, tn), lambda j, i: (0, j)),
            pl.BlockSpec((d, tn), lambda j, i: (0, j + nb)),
            pl.BlockSpec((SHORT_K, tn), lambda j, i: (0, j)),
            pl.BlockSpec((SHORT_K, tn), lambda j, i: (0, j + nb)),
            pl.BlockSpec((1, tn), lambda j, i: (0, j)),
            pl.BlockSpec((1, tn), lambda j, i: (0, j + nb)),
            pl.BlockSpec((slab, w_down.shape[1]), lambda j, i: (j * mt + i, 0)),
        ],
        out_specs=[pl.BlockSpec((tm, tn), lambda j, i: (i, j)),
                   pl.BlockSpec((slab, w_down.shape[1]), lambda j, i: (j * mt + i, 0))],
        out_shape=[jax.ShapeDtypeStruct((t, d_ff), BF16), jax.ShapeDtypeStruct(w_down.shape, BF16)],
        scratch_shapes=([pltpu.VMEM((CARRY_ROWS, tn), F32)] * 2 + [pltpu.VMEM((d, tn), BF16)] * 2
                        + [pltpu.VMEM((tn, tn), BF16)]
                        + [pltpu.VMEM((rows, tn), F32) for rows in parked_rows for _ in range(2)]),
        compiler_params=_params(("parallel", "arbitrary"), 48),
        name="ffn_up",
    )(h, w_up, w_up, conv_w, conv_w, conv_b, conv_b, w_down)


def _ple_kernel(h_ref, p_ref, wg_ref, wp_ref, o_ref, wg_bf, wp_bf):
    _cast_weights_at_sweep_start([(wg_ref, wg_bf), (wp_ref, wp_bf)])
    gate = _sigmoid(_dot(h_ref[...], wg_bf[...]))
    o_ref[...] = (gate * _dot(p_ref[...].astype(BF16), wp_bf[...])).astype(BF16)


def _ple(h, p, w_gate, w_proj):
    t, d = h.shape
    n = w_gate.shape[1]
    ple = p.shape[1]
    tm, tn = ROW_TILE_MM, 512
    return pl.pallas_call(
        _ple_kernel,
        grid=(n // tn, t // tm),
        in_specs=[
            pl.BlockSpec((tm, d), lambda j, i: (_serpentine(j, i, t // tm), 0)),
            pl.BlockSpec((tm, ple), lambda j, i: (_serpentine(j, i, t // tm), 0)),
            pl.BlockSpec((d, tn), lambda j, i: (0, j)),
            pl.BlockSpec((ple, tn), lambda j, i: (0, j)),
        ],
        out_specs=pl.BlockSpec((tm, tn), lambda j, i: (_serpentine(j, i, t // tm), j)),
        out_shape=jax.ShapeDtypeStruct((t, n), BF16),
        scratch_shapes=[pltpu.VMEM((d, tn), BF16), pltpu.VMEM((ple, tn), BF16)],
        compiler_params=_params(("parallel", "arbitrary"), 48),
        name="ple",
    )(h, p, w_gate, w_proj)


def _split_bf16x3(x):
    hi = x.astype(BF16)
    rest = x - hi.astype(F32)
    mid = rest.astype(BF16)
    lo = (rest - mid.astype(F32)).astype(BF16)
    return hi, mid, lo


def _forget_cumsum_kernel(x_ref, g_ref, wf_ref, b_ref, spread_ref, ones_ref,
                          h_ref, c_ref, augk_ref, carry_ref, wf_bf, *, tiles_per_seq):
    @pl.when(pl.program_id(0) == 0)
    def _():
        wf_bf[...] = wf_ref[...].astype(BF16)

    @pl.when(pl.program_id(0) % tiles_per_seq == 0)
    def _():
        carry_ref[...] = jnp.zeros_like(carry_ref)

    h = _rms_scale(x_ref[...], g_ref[...]).astype(BF16)
    h_ref[...] = h
    ts = h.shape[0]
    nh = c_ref.shape[1]
    lane = lax.broadcasted_iota(jnp.int32, (ts, LANES), 1)
    log_f = jnp.where(lane < nh, jax.nn.log_sigmoid(_dot_nt(h, wf_bf[...]) + b_ref[...]), 0.0)
    row = lax.broadcasted_iota(jnp.int32, (ts, ts), 0)
    col = lax.broadcasted_iota(jnp.int32, (ts, ts), 1)
    lower = (col <= row).astype(F32)
    csum = jnp.dot(lower, log_f, preferred_element_type=F32,
                   precision=lax.Precision.HIGHEST) + carry_ref[...]
    c_ref[...] = csum[:, 0:nh]
    carry_ref[...] = csum[ts - 1:ts, :]

    c3 = jnp.where(lane < nh, csum,
                   jnp.where(lane < 2 * nh, pltpu.roll(csum, nh, 1), pltpu.roll(csum, 2 * nh, 1)))
    hi, mid, lo = _split_bf16x3(c3 * LOG2_E)
    pieces = jnp.where(lane < nh, hi.astype(F32),
                       jnp.where(lane < 2 * nh, mid.astype(F32), lo.astype(F32))).astype(BF16)
    augk_ref[...] = (_dot(pieces, spread_ref[...]) + ones_ref[...]).astype(BF16)


def _norm_and_forget_cumsum(x, gain, w_in_t, row_start, bias, seq):
    t, d = x.shape
    nh = bias.shape[0]
    ts = CUMSUM_TILE
    assert row_start % LANES == 0 and 3 * nh <= LANES
    width = nh * HEAD_DIM
    lane = jnp.arange(width)[None, :]
    piece_row = jnp.arange(LANES)[:, None]
    spread = jnp.where((piece_row < 3 * nh)
                       & (lane == (piece_row % nh) * HEAD_DIM + piece_row // nh), -1.0, 0.0)
    ones = jnp.where((lane % HEAD_DIM >= 3) & (lane % HEAD_DIM < 6), 1.0, 0.0).astype(F32)
    bias_tile = jnp.zeros((1, LANES), F32).at[0, 0:nh].set(bias)
    return pl.pallas_call(
        functools.partial(_forget_cumsum_kernel, tiles_per_seq=seq // ts),
        grid=(t // ts,),
        in_specs=[pl.BlockSpec((ts, d), lambda i: (i, 0)),
                  _gain_spec(d),
                  pl.BlockSpec((LANES, d), lambda i: (row_start // LANES, 0)),
                  pl.BlockSpec((1, LANES), lambda i: (0, 0)),
                  pl.BlockSpec((LANES, width), lambda i: (0, 0)),
                  pl.BlockSpec((1, width), lambda i: (0, 0))],
        out_specs=[pl.BlockSpec((ts, d), lambda i: (i, 0)),
                   pl.BlockSpec((ts, nh), lambda i: (i, 0)),
                   pl.BlockSpec((ts, width), lambda i: (i, 0))],
        out_shape=[jax.ShapeDtypeStruct((t, d), BF16), jax.ShapeDtypeStruct((t, nh), F32),
                   jax.ShapeDtypeStruct((t, width), BF16)],
        scratch_shapes=[pltpu.VMEM((1, LANES), F32), pltpu.VMEM((LANES, d), BF16)],
        compiler_params=_params(("arbitrary",), 48),
        name="norm_forget_cumsum",
    )(x, gain.reshape(1, d), w_in_t, bias_tile, spread.astype(BF16), ones)


AUG_ROWS = 16
SUM_ROWS = 16
HEADS_PER_STEP = 2


def _attention_kernel(q_ref, k_ref, v_ref, augk_ref, c_ref, o_ref,
                      kx_ref, vt_ref, qxt_ref, m_ref, *buffers, tile):
    n = HEADS_PER_STEP
    buffers = [buf.at[:, 0:tile] for buf in buffers]
    s_ref = [buffers[2 * g:2 * g + 2] for g in range(n)]
    p_ref = [buffers[2 * n + 2 * g:2 * n + 2 * g + 2] for g in range(n)]
    acc_ref = buffers[4 * n:]
    kt = tile // 2
    qi = pl.program_id(2)
    n_key_tiles = kx_ref.shape[1]
    heads = range(HEADS_PER_STEP)

    def lanes(g):
        return slice(g * HEAD_DIM, (g + 1) * HEAD_DIM)

    @pl.when(qi == 0)
    def _():
        def stage(j, _):
            rows = pl.ds(pl.multiple_of(j * kt, kt), kt)
            for g in heads:
                kx_ref[g, j, :, 0:HEAD_DIM] = k_ref[rows, lanes(g)]
                kx_ref[g, j, :, HEAD_DIM:] = augk_ref[rows, lanes(g)]
                vt_ref[g, j, 0:HEAD_DIM, :] = v_ref[rows, lanes(g)].T
                vt_ref[g, j, HEAD_DIM:, :] = jnp.ones((SUM_ROWS, kt), BF16)
            return 0
        lax.fori_loop(0, n_key_tiles, stage, 0)
        for g in heads:
            qxt_ref[g, HEAD_DIM + AUG_ROWS:, :] = jnp.zeros((HEAD_DIM - AUG_ROWS, tile), BF16)

    r = lax.broadcasted_iota(jnp.int32, (AUG_ROWS, tile), 0)
    for g in heads:
        qxt_ref[g, 0:HEAD_DIM, :] = q_ref[:, lanes(g)].T
        hi, mid, lo = _split_bf16x3(c_ref[g, qi] * LOG2_E)
        aug = jnp.where(r < 3, 1.0,
                        jnp.where(r == 3, hi.astype(F32),
                                  jnp.where(r == 4, mid.astype(F32),
                                            jnp.where(r == 5, lo.astype(F32), 0.0))))
        qxt_ref[g, HEAD_DIM:HEAD_DIM + AUG_ROWS, :] = aug.astype(BF16)

    def scores_into(g, slot, ki):
        s = _dot(kx_ref[g, ki], qxt_ref[g])
        s_ref[g][slot][...] = s
        return jnp.max(s, axis=0, keepdims=True)

    def softmax_into(g, slot, m, tile_max):
        m_new = jnp.maximum(m, tile_max)
        p_ref[g][slot][...] = jnp.exp2(s_ref[g][slot][...] - m_new).astype(BF16)
        return m_new, jnp.exp2(m - m_new)

    def accumulate(g, slot, alpha, ki):
        acc_ref[g][...] = alpha * acc_ref[g][...] + _dot(vt_ref[g, ki], p_ref[g][slot][...])

    for g in heads:
        p_ref[g][1][...] = jnp.zeros((kt, tile), BF16)
        acc_ref[g][...] = jnp.zeros(acc_ref[g].shape, F32)
    max_0 = tuple(scores_into(g, 0, 0) for g in heads)

    def pair(jj, carry):
        m, alpha_1, max_0 = carry
        k0 = 2 * jj
        max_1 = [scores_into(g, 1, k0 + 1) for g in heads]
        m, alpha_0 = zip(*[softmax_into(g, 0, m[g], max_0[g]) for g in heads])
        for g in heads:
            accumulate(g, 1, alpha_1[g], jnp.maximum(k0 - 1, 0))
        max_0 = tuple(scores_into(g, 0, k0 + 2) for g in heads)
        m, alpha_1 = zip(*[softmax_into(g, 1, m[g], max_1[g]) for g in heads])
        for g in heads:
            accumulate(g, 0, alpha_0[g], k0)
        return m, alpha_1, max_0

    init = (tuple(jnp.full((1, tile), NEG_INF, F32) for _ in heads),
            tuple(jnp.ones((1, tile), F32) for _ in heads), max_0)
    m, alpha_1, _ = lax.fori_loop(0, qi, pair, init)

    k0 = 2 * qi
    key = lax.broadcasted_iota(jnp.int32, (kt, tile), 0)
    query = lax.broadcasted_iota(jnp.int32, (kt, tile), 1)
    upper = slice(kt, tile)

    def upper_scores_into(g):
        s = _dot(kx_ref[g, k0 + 1], qxt_ref[g, :, upper])
        s = jnp.where(lax.broadcasted_iota(jnp.int32, (kt, kt), 0)
                      <= lax.broadcasted_iota(jnp.int32, (kt, kt), 1), s, NEG_INF)
        s_ref[g][1][:, 0:kt] = s
        return jnp.max(s, axis=0, keepdims=True)

    max_1 = [upper_scores_into(g) for g in heads]
    for g in heads:
        s_ref[g][0][...] = jnp.where(key <= query, s_ref[g][0][...], NEG_INF)
    max_0 = [jnp.max(s_ref[g][0][...], axis=0, keepdims=True) for g in heads]
    m, alpha_0 = zip(*[softmax_into(g, 0, m[g], max_0[g]) for g in heads])
    for g in heads:
        accumulate(g, 1, alpha_1[g], jnp.maximum(k0 - 1, 0))
    alpha_upper = []
    for g in heads:
        m_ref[g, 0:1, :] = m[g]
        m_old = m_ref[g, 0:1, upper]
        m_upper = jnp.maximum(m_old, max_1[g])
        alpha_upper.append(jnp.exp2(m_old - m_upper))
        p_ref[g][1][:, 0:kt] = jnp.exp2(s_ref[g][1][:, 0:kt] - m_upper).astype(BF16)
    for g in heads:
        accumulate(g, 0, alpha_0[g], k0)
    for g in heads:
        acc_ref[g][:, upper] = (alpha_upper[g] * acc_ref[g][:, upper]
                                + _dot(vt_ref[g, k0 + 1], p_ref[g][1][:, 0:kt]))
    for g in heads:
        o_t = acc_ref[g][0:HEAD_DIM, :] / acc_ref[g][HEAD_DIM:HEAD_DIM + 1, :]
        o_ref[:, lanes(g)] = o_t.T.astype(BF16)


def _attention(qkv, augk, c, batch, seq):
    t = qkv.shape[0]
    tile = ATTN_TILE
    kt = tile // 2
    nq = seq // tile
    g = HEADS_PER_STEP
    width = g * HEAD_DIM
    groups = N_HEADS // g
    c_rows = jnp.transpose(c.reshape(batch, seq, N_HEADS), (0, 2, 1)).reshape(
        batch * N_HEADS, nq, 1, tile)
    return pl.pallas_call(
        functools.partial(_attention_kernel, tile=tile),
        grid=(batch, groups, nq),
        in_specs=[
            pl.BlockSpec((tile, width), lambda b, h, i: (b * nq + i, h)),
            pl.BlockSpec((seq, width), lambda b, h, i: (b, groups + h)),
            pl.BlockSpec((seq, width), lambda b, h, i: (b, 2 * groups + h)),
            pl.BlockSpec((seq, width), lambda b, h, i: (b, h)),
            pl.BlockSpec((g, nq, 1, tile), lambda b, h, i: (b * groups + h, 0, 0, 0)),
        ],
        out_specs=pl.BlockSpec((tile, width), lambda b, h, i: (b * nq + i, h)),
        out_shape=jax.ShapeDtypeStruct((t, N_HEADS * HEAD_DIM), BF16),
        scratch_shapes=[pltpu.VMEM((g, seq // kt, kt, 2 * HEAD_DIM), BF16),
                        pltpu.VMEM((g, seq // kt, HEAD_DIM + SUM_ROWS, kt), BF16),
                        pltpu.VMEM((g, 2 * HEAD_DIM, tile), BF16),
                        pltpu.VMEM((g, 8, tile), F32),
                        *[pltpu.VMEM((kt, tile + LANES), F32)] * (2 * g),
                        *[pltpu.VMEM((kt, tile + LANES), BF16)] * (2 * g),
                        *[pltpu.VMEM((HEAD_DIM + SUM_ROWS, tile + LANES), F32)] * g],
        compiler_params=_params(("parallel", "parallel", "arbitrary"), 60),
        name="fox_attention",
    )(qkv, qkv, qkv, augk, c_rows)


def kernel(x, p, norm_mix_pre, w_in, forget_bias, conv_mix_w, w_branch_conv, w_branch_attn, w_out,
           norm_mix_post, norm_ffn_pre, w_up, ffn_conv_w, ffn_conv_b, w_down, norm_ffn_post,
           w_ple_proj, norm_ple_gate, w_ple_gate, norm_ple_post):
    batch, seq, d = x.shape
    t = batch * seq
    depth = w_in.shape[0]
    conv_width = conv_mix_w.shape[-1]
    attn_width = N_HEADS * HEAD_DIM
    qkv_start = 3 * conv_width
    forget_start = qkv_start + 3 * attn_width
    gate_start = forget_start + N_HEADS

    xs = x.reshape(t, d)
    for i in range(depth):
        w_in_t = jnp.transpose(w_in[i])

        h1, c, augk = _norm_and_forget_cumsum(xs, norm_mix_pre[i], w_in_t, forget_start,
                                              forget_bias[i], seq)
        a = _conv_branch(h1, w_in_t, conv_mix_w[i], seq)
        qkv = _qkv(h1, w_in_t, qkv_start, 3 * attn_width)
        gates, w_a, w_b = _gates(h1, w_in_t, gate_start, 2 * d, w_branch_conv[i], w_branch_attn[i])
        o = _attention(qkv, augk, c, batch, seq)
        merged = _merge(a, o, w_a, w_b, gates)
        y1 = _matmul(merged, w_out[i], ROW_TILE_MM, 512, "out_proj")
        x1, h2 = _residual_norm(xs, y1, norm_mix_post[i], norm_ffn_pre[i])

        ffn, w_down_bf16 = _ffn_up(h2, w_up[i], ffn_conv_w[i], ffn_conv_b[i], seq, w_down[i])
        y2 = _matmul(ffn, w_down_bf16, ROW_TILE_WIDE_K, 512, "ffn_down")
        x2, h3 = _residual_norm(x1, y2, norm_ffn_post[i], norm_ple_gate[i])

        ge = _ple(h3, p[i].reshape(t, -1), w_ple_gate[i], w_ple_proj[i])
        xs = _residual(x2, ge, norm_ple_post[i])
    return xs.reshape(batch, seq, d)
```

```python
import functools
import math

import jax
import jax.numpy as jnp
from jax import lax
from jax.experimental import pallas as pl
from jax.experimental.pallas import tpu as pltpu

BF16 = jnp.bfloat16
F32 = jnp.float32

LANES = 128
BF16_SUBLANES = 16
N_HEADS = 32
HEAD_DIM = 128
SHORT_K = 3
EPS = 1e-6
NEG_INF = -1e30
LOG2_E = math.log2(math.e)
GELU_C0 = math.sqrt(2.0 / math.pi)
GELU_C1 = GELU_C0 * 0.044715
MIB = 1024 * 1024

ROW_TILE_NORM = 256
ROW_TILE_MM = 1024
ROW_TILE_WIDE_K = 512
ATTN_TILE = 1024
CUMSUM_TILE = 512
CARRY_ROWS = 8


def _params(semantics, vmem_mib):
    return pltpu.CompilerParams(dimension_semantics=semantics, vmem_limit_bytes=vmem_mib * MIB)


def _dot(a, b):
    return jnp.dot(a, b, preferred_element_type=F32)


def _dot_nt(a, b):
    return lax.dot_general(a, b, (((1,), (1,)), ((), ())), preferred_element_type=F32)


def _sigmoid(x):
    return 0.5 * jnp.tanh(0.5 * x) + 0.5


def _rms_scale(x, g):
    inv = lax.rsqrt(jnp.mean(x * x, axis=-1, keepdims=True) + EPS)
    return x * inv * g


def _residual_norm_kernel(x_ref, y_ref, g_post_ref, g_pre_ref, x_out_ref, h_ref):
    x_new = x_ref[...] + _rms_scale(y_ref[...].astype(F32), g_post_ref[...])
    x_out_ref[...] = x_new
    h_ref[...] = _rms_scale(x_new, g_pre_ref[...]).astype(BF16)


def _residual_kernel(x_ref, y_ref, g_post_ref, x_out_ref):
    x_out_ref[...] = x_ref[...] + _rms_scale(y_ref[...].astype(F32), g_post_ref[...])


def _row_spec(tr, d):
    return pl.BlockSpec((tr, d), lambda i: (i, 0))


def _gain_spec(d):
    return pl.BlockSpec((1, d), lambda i: (0, 0))


def _residual_norm(x, y, g_post, g_pre):
    t, d = x.shape
    tr = ROW_TILE_NORM
    return pl.pallas_call(
        _residual_norm_kernel,
        grid=(t // tr,),
        in_specs=[_row_spec(tr, d), _row_spec(tr, d), _gain_spec(d), _gain_spec(d)],
        out_specs=[_row_spec(tr, d), _row_spec(tr, d)],
        out_shape=[jax.ShapeDtypeStruct((t, d), F32), jax.ShapeDtypeStruct((t, d), BF16)],
        compiler_params=_params(("parallel",), 40),
        name="residual_norm",
    )(x, y, g_post.reshape(1, d), g_pre.reshape(1, d))


def _residual(x, y, g_post):
    t, d = x.shape
    tr = ROW_TILE_NORM
    return pl.pallas_call(
        _residual_kernel,
        grid=(t // tr,),
        in_specs=[_row_spec(tr, d), _row_spec(tr, d), _gain_spec(d)],
        out_specs=_row_spec(tr, d),
        out_shape=jax.ShapeDtypeStruct((t, d), F32),
        compiler_params=_params(("parallel",), 40),
        name="residual",
    )(x, y, g_post.reshape(1, d))


def _causal_conv3(u, carry_ref, w_ref):
    tm = u.shape[0]
    w0, w1, w2 = w_ref[0:1, :], w_ref[1:2, :], w_ref[2:3, :]
    y = w0 * pltpu.roll(u, 2, 0) + w1 * pltpu.roll(u, 1, 0) + w2 * u
    head = u[0:CARRY_ROWS, :]
    prev = carry_ref[...]
    r = lax.broadcasted_iota(jnp.int32, head.shape, 0)
    back1 = jnp.where(r < 1, pltpu.roll(prev, 1, 0), pltpu.roll(head, 1, 0))
    back2 = jnp.where(r < 2, pltpu.roll(prev, 2, 0), pltpu.roll(head, 2, 0))
    y_head = w0 * back2 + w1 * back1 + w2 * head
    carry_ref[...] = u[tm - CARRY_ROWS:, :]
    return jnp.concatenate([y_head, y[CARRY_ROWS:, :]], axis=0)


def _reset_carry_at_sequence_start(carry_refs, tiles_per_seq):
    @pl.when(pl.program_id(1) % tiles_per_seq == 0)
    def _():
        for ref in carry_refs:
            ref[...] = jnp.zeros_like(ref)


def _serpentine(j, i, n_tiles):
    return jnp.where(j % 2 == 0, i, n_tiles - 1 - i)


def _cast_weights_at_sweep_start(pairs):
    @pl.when(pl.program_id(1) == 0)
    def _():
        for w_ref, w_bf16_ref in pairs:
            w_bf16_ref[...] = w_ref[...].astype(BF16)


FFN_UP_PARKED_PARTS, FFN_UP_PART_DEN = (12,), 16


def _write_identity_at_sweep_start(eye_ref):
    @pl.when(pl.program_id(1) == 0)
    def _():
        n = eye_ref.shape[0]
        eye_ref[...] = (lax.broadcasted_iota(jnp.int32, (n, n), 0)
                        == lax.broadcasted_iota(jnp.int32, (n, n), 1)).astype(BF16)


def _parked(ref):
    rows = pl.ds(pl.multiple_of(jnp.minimum(pl.program_id(1), 0), CARRY_ROWS), ref.shape[0])
    return ref[rows, :]


def _through_identity(x, eye_ref):
    return _dot(x, eye_ref[...]).astype(x.dtype)


def _conv_branch_kernel(h_ref, wb_ref, wc_ref, wv_ref, cw_ref, o_ref, carry_ref,
                        wb_bf, wc_bf, wv_bf, *, tiles_per_seq):
    _cast_weights_at_sweep_start([(wb_ref, wb_bf), (wc_ref, wc_bf), (wv_ref, wv_bf)])
    _reset_carry_at_sequence_start([carry_ref], tiles_per_seq)
    h = h_ref[...]
    u = _dot_nt(h, wc_bf[...]) * _dot_nt(h, wv_bf[...])
    y = _causal_conv3(u, carry_ref, cw_ref)
    o_ref[...] = (_dot_nt(h, wb_bf[...]) * y).astype(BF16)


def _conv_branch(h, w_in_t, conv_w, seq):
    t, d = h.shape
    width = conv_w.shape[1]
    tm, tn = ROW_TILE_MM, 256
    nb = width // tn
    return pl.pallas_call(
        functools.partial(_conv_branch_kernel, tiles_per_seq=seq // tm),
        grid=(nb, t // tm),
        in_specs=[
            pl.BlockSpec((tm, d), lambda j, i: (i, 0)),
            pl.BlockSpec((tn, d), lambda j, i: (j, 0)),
            pl.BlockSpec((tn, d), lambda j, i: (j + nb, 0)),
            pl.BlockSpec((tn, d), lambda j, i: (j + 2 * nb, 0)),
            pl.BlockSpec((SHORT_K, tn), lambda j, i: (0, j)),
        ],
        out_specs=pl.BlockSpec((tm, tn), lambda j, i: (i, j)),
        out_shape=jax.ShapeDtypeStruct((t, width), BF16),
        scratch_shapes=[pltpu.VMEM((CARRY_ROWS, tn), F32)] + [pltpu.VMEM((tn, d), BF16)] * 3,
        compiler_params=_params(("parallel", "arbitrary"), 60),
        name="conv_branch",
    )(h, w_in_t, w_in_t, w_in_t, conv_w)


def _qkv_kernel(h_ref, w_ref, o_ref, w_bf, *, q_tiles, scale):
    _cast_weights_at_sweep_start([(w_ref, w_bf)])
    acc = _dot_nt(h_ref[...], w_bf[...])
    factor = jnp.where(pl.program_id(0) < q_tiles, scale, 1.0).astype(F32)
    o_ref[...] = (acc * factor).astype(BF16)


def _qkv(h, w_in_t, row_start, n):
    t, d = h.shape
    tm, tn = ROW_TILE_MM, 512
    first = row_start // tn
    kern = functools.partial(_qkv_kernel, q_tiles=(n // 3) // tn, scale=LOG2_E / math.sqrt(HEAD_DIM))
    return pl.pallas_call(
        kern,
        grid=(n // tn, t // tm),
        in_specs=[pl.BlockSpec((tm, d), lambda j, i: (_serpentine(j, i, t // tm), 0)),
                  pl.BlockSpec((tn, d), lambda j, i: (first + j, 0))],
        out_specs=pl.BlockSpec((tm, tn), lambda j, i: (_serpentine(j, i, t // tm), j)),
        out_shape=jax.ShapeDtypeStruct((t, n), BF16),
        scratch_shapes=[pltpu.VMEM((tn, d), BF16)],
        compiler_params=_params(("parallel", "arbitrary"), 48),
        name="qkv",
    )(h, w_in_t)


def _gates_kernel(h_ref, w_ref, w_next_ref, wa_ref, wb_ref, o_ref, wa_bf_ref, wb_bf_ref, w_bf,
                  *, shift):
    wa_bf_ref[...] = wa_ref[...].astype(BF16)
    wb_bf_ref[...] = wb_ref[...].astype(BF16)

    @pl.when(pl.program_id(1) == 0)
    def _():
        tn = w_bf.shape[0]
        w_bf[0:tn - shift, :] = w_ref[shift:tn, :].astype(BF16)
        w_bf[tn - shift:tn, :] = w_next_ref[0:shift, :].astype(BF16)

    o_ref[...] = _sigmoid(_dot_nt(h_ref[...], w_bf[...])).astype(BF16)


def _gates(h, w_in_t, row_start, n, w_a, w_b):
    t, d = h.shape
    tm, tn = ROW_TILE_MM, 512
    shift = row_start % LANES
    aligned = row_start - shift
    assert 0 < shift and shift % BF16_SUBLANES == 0 and aligned % tn == 0
    mt = t // tm
    steps = (n // tn) * mt
    slab = w_a.shape[0] // steps
    assert w_a.shape == w_b.shape and slab * steps == w_a.shape[0] and slab % BF16_SUBLANES == 0
    slab_spec = pl.BlockSpec((slab, w_a.shape[1]), lambda j, i: (j * mt + i, 0))
    return pl.pallas_call(
        functools.partial(_gates_kernel, shift=shift),
        grid=(n // tn, mt),
        in_specs=[pl.BlockSpec((tm, d), lambda j, i: (_serpentine(j, i, mt), 0)),
                  pl.BlockSpec((tn, d), lambda j, i: (aligned // tn + j, 0)),
                  pl.BlockSpec((LANES, d), lambda j, i: ((aligned + (j + 1) * tn) // LANES, 0)),
                  slab_spec, slab_spec],
        out_specs=[pl.BlockSpec((tm, tn), lambda j, i: (_serpentine(j, i, mt), j)),
                   slab_spec, slab_spec],
        out_shape=[jax.ShapeDtypeStruct((t, n), BF16), jax.ShapeDtypeStruct(w_a.shape, BF16),
                   jax.ShapeDtypeStruct(w_b.shape, BF16)],
        scratch_shapes=[pltpu.VMEM((tn, d), BF16)],
        compiler_params=_params(("parallel", "arbitrary"), 48),
        name="gates",
    )(h, w_in_t, w_in_t, w_a, w_b)


def _matmul_kernel(a_ref, w_ref, o_ref):
    o_ref[...] = _dot(a_ref[...], w_ref[...]).astype(BF16)


def _matmul_f32_weights_kernel(a_ref, w_ref, o_ref, w_bf):
    _cast_weights_at_sweep_start([(w_ref, w_bf)])
    o_ref[...] = _dot(a_ref[...], w_bf[...]).astype(BF16)


def _matmul(a, w, tm, tn, name):
    t, k = a.shape
    n = w.shape[1]
    cast_in_kernel = w.dtype == F32
    return pl.pallas_call(
        _matmul_f32_weights_kernel if cast_in_kernel else _matmul_kernel,
        grid=(n // tn, t // tm),
        in_specs=[pl.BlockSpec((tm, k), lambda j, i: (_serpentine(j, i, t // tm), 0)),
                  pl.BlockSpec((k, tn), lambda j, i: (0, j))],
        out_specs=pl.BlockSpec((tm, tn), lambda j, i: (_serpentine(j, i, t // tm), j)),
        out_shape=jax.ShapeDtypeStruct((t, n), BF16),
        scratch_shapes=[pltpu.VMEM((k, tn), BF16)] if cast_in_kernel else [],
        compiler_params=_params(("parallel", "arbitrary"), 52),
        name=name,
    )(a, w)


def _merge_kernel(a_ref, o_ref, wa_ref, wb_ref, ga_ref, gb_ref, out_ref):
    y_a = _dot(a_ref[...], wa_ref[...])
    y_b = _dot(o_ref[...], wb_ref[...])
    out_ref[...] = (ga_ref[...] * y_a + gb_ref[...] * y_b).astype(BF16)


def _merge(a, o, w_a, w_b, gates):
    t, d = a.shape
    n = w_a.shape[1]
    tm, tn = ROW_TILE_MM, 512
    nb = n // tn
    return pl.pallas_call(
        _merge_kernel,
        grid=(nb, t // tm),
        in_specs=[
            pl.BlockSpec((tm, d), lambda j, i: (_serpentine(j, i, t // tm), 0)),
            pl.BlockSpec((tm, d), lambda j, i: (_serpentine(j, i, t // tm), 0)),
            pl.BlockSpec((d, tn), lambda j, i: (0, j)),
            pl.BlockSpec((d, tn), lambda j, i: (0, j)),
            pl.BlockSpec((tm, tn), lambda j, i: (_serpentine(j, i, t // tm), j)),
            pl.BlockSpec((tm, tn), lambda j, i: (_serpentine(j, i, t // tm), j + nb)),
        ],
        out_specs=pl.BlockSpec((tm, tn), lambda j, i: (_serpentine(j, i, t // tm), j)),
        out_shape=jax.ShapeDtypeStruct((t, n), BF16),
        compiler_params=_params(("parallel", "arbitrary"), 60),
        name="merge",
    )(a, o, w_a, w_b, gates, gates)


def _ffn_up_kernel(h_ref, wg_ref, wv_ref, cwg_ref, cwv_ref, bg_ref, bv_ref, wd_ref, o_ref, wd_bf_ref,
                   carry_g_ref, carry_v_ref, w_bf, eye_ref, *parking, tiles_per_seq):
    tn = o_ref.shape[1]
    _cast_weights_at_sweep_start([(wg_ref, w_bf.at[:, 0:tn]), (wv_ref, w_bf.at[:, tn:])])
    _reset_carry_at_sequence_start([carry_g_ref, carry_v_ref], tiles_per_seq)
    _write_identity_at_sweep_start(eye_ref)
    wd_bf_ref[...] = wd_ref[...].astype(BF16)

    def activation(raw_gate, raw_val):
        u_gate = _causal_conv3(raw_gate, carry_g_ref, cwg_ref) + bg_ref[...]
        u_val = _causal_conv3(raw_val, carry_v_ref, cwv_ref) + bv_ref[...]
        half_gate = 0.5 * u_gate
        t = jnp.tanh(u_gate * (GELU_C0 + GELU_C1 * (u_gate * u_gate)))
        return ((half_gate + half_gate * t) * u_val).astype(BF16)

    start, waiting = 0, None
    for raw_ref in parking:
        rows = slice(start, start + raw_ref.shape[0])
        raw_ref[...] = _dot(h_ref[rows, :], w_bf[...])
        if waiting is not None:
            o_ref[waiting[0], :] = _through_identity(waiting[1], eye_ref)
        raw = _parked(raw_ref)
        waiting = (rows, activation(raw[:, 0:tn], raw[:, tn:]))
        start = rows.stop
    raw_last = _dot(h_ref[start:, :], w_bf[...])
    o_ref[waiting[0], :] = _through_identity(waiting[1], eye_ref)
    o_ref[start:, :] = activation(raw_last[:, 0:tn], raw_last[:, tn:])


def _ffn_up(h, w_up, conv_w, conv_b, seq, w_down):
    t, d = h.shape
    d_ff = w_up.shape[1] // 2
    tm, tn = ROW_TILE_MM, 256
    parked_rows = [tm * num // FFN_UP_PART_DEN for num in FFN_UP_PARKED_PARTS]
    nb = d_ff // tn
    mt = t // tm
    slab = w_down.shape[0] // (nb * mt)
    assert slab * nb * mt == w_down.shape[0] and slab % BF16_SUBLANES == 0
    conv_b = conv_b.reshape(1, 2 * d_ff)
    return pl.pallas_call(
        functools.partial(_ffn_up_kernel, tiles_per_seq=seq // tm),
        grid=(nb, mt),
        in_specs=[
            pl.BlockSpec((tm, d), lambda j, i: (i, 0)),
            pl.BlockSpec((d, tn), lambda j, i: (0, j)),
            pl.BlockSpec((d, tn), lambda j, i: (0, j + nb)),
            pl.BlockSpec((SHORT_K, tn), lambda j, i: (0, j)),
            pl.BlockSpec((SHORT_K, tn), lambda j, i: (0, j + nb)),
            pl.BlockSpec((1, tn), lambda j, i: (0, j)),
            pl.BlockSpec((1, tn), lambda j, i: (0, j + nb)),
            pl.BlockSpec((slab, w_down.shape[1]), lambda j, i: (j * mt + i, 0)),
        ],
        out_specs=[pl.BlockSpec((tm, tn), lambda j, i: (i, j)),
                   pl.BlockSpec((slab, w_down.shape[1]), lambda j, i: (j * mt + i, 0))],
        out_shape=[jax.ShapeDtypeStruct((t, d_ff), BF16), jax.ShapeDtypeStruct(w_down.shape, BF16)],
        scratch_shapes=([pltpu.VMEM((CARRY_ROWS, tn), F32)] * 2 + [pltpu.VMEM((d, 2 * tn), BF16)]
                        + [pltpu.VMEM((tn, tn), BF16)]
                        + [pltpu.VMEM((rows, 2 * tn), F32) for rows in parked_rows]),
        compiler_params=_params(("parallel", "arbitrary"), 48),
        name="ffn_up",
    )(h, w_up, w_up, conv_w, conv_w, conv_b, conv_b, w_down)


def _ple_kernel(h_ref, p_ref, wg_ref, wp_ref, o_ref, wg_bf, wp_bf):
    _cast_weights_at_sweep_start([(wg_ref, wg_bf), (wp_ref, wp_bf)])
    gate = _sigmoid(_dot(h_ref[...], wg_bf[...]))
    o_ref[...] = (gate * _dot(p_ref[...].astype(BF16), wp_bf[...])).astype(BF16)


def _ple(h, p, w_gate, w_proj):
    t, d = h.shape
    n = w_gate.shape[1]
    ple = p.shape[1]
    tm, tn = ROW_TILE_MM, 512
    return pl.pallas_call(
        _ple_kernel,
        grid=(n // tn, t // tm),
        in_specs=[
            pl.BlockSpec((tm, d), lambda j, i: (_serpentine(j, i, t // tm), 0)),
            pl.BlockSpec((tm, ple), lambda j, i: (_serpentine(j, i, t // tm), 0)),
            pl.BlockSpec((d, tn), lambda j, i: (0, j)),
            pl.BlockSpec((ple, tn), lambda j, i: (0, j)),
        ],
        out_specs=pl.BlockSpec((tm, tn), lambda j, i: (_serpentine(j, i, t // tm), j)),
        out_shape=jax.ShapeDtypeStruct((t, n), BF16),
        scratch_shapes=[pltpu.VMEM((d, tn), BF16), pltpu.VMEM((ple, tn), BF16)],
        compiler_params=_params(("parallel", "arbitrary"), 48),
        name="ple",
    )(h, p, w_gate, w_proj)


def _split_bf16x3(x):
    hi = x.astype(BF16)
    rest = x - hi.astype(F32)
    mid = rest.astype(BF16)
    lo = (rest - mid.astype(F32)).astype(BF16)
    return hi, mid, lo


def _forget_cumsum_kernel(x_ref, g_ref, wf_ref, b_ref, spread_ref, ones_ref,
                          h_ref, c_ref, augk_ref, carry_ref, wf_bf, *, tiles_per_seq):
    @pl.when(pl.program_id(0) == 0)
    def _():
        wf_bf[...] = wf_ref[...].astype(BF16)

    @pl.when(pl.program_id(0) % tiles_per_seq == 0)
    def _():
        carry_ref[...] = jnp.zeros_like(carry_ref)

    h = _rms_scale(x_ref[...], g_ref[...]).astype(BF16)
    h_ref[...] = h
    ts = h.shape[0]
    nh = c_ref.shape[1]
    lane = lax.broadcasted_iota(jnp.int32, (ts, LANES), 1)
    log_f = jnp.where(lane < nh, jax.nn.log_sigmoid(_dot_nt(h, wf_bf[...]) + b_ref[...]), 0.0)
    row = lax.broadcasted_iota(jnp.int32, (ts, ts), 0)
    col = lax.broadcasted_iota(jnp.int32, (ts, ts), 1)
    lower = (col <= row).astype(F32)
    csum = jnp.dot(lower, log_f, preferred_element_type=F32,
                   precision=lax.Precision.HIGHEST) + carry_ref[...]
    c_ref[...] = csum[:, 0:nh]
    carry_ref[...] = csum[ts - 1:ts, :]

    c3 = jnp.where(lane < nh, csum,
                   jnp.where(lane < 2 * nh, pltpu.roll(csum, nh, 1), pltpu.roll(csum, 2 * nh, 1)))
    hi, mid, lo = _split_bf16x3(c3 * LOG2_E)
    pieces = jnp.where(lane < nh, hi.astype(F32),
                       jnp.where(lane < 2 * nh, mid.astype(F32), lo.astype(F32))).astype(BF16)
    augk_ref[...] = (_dot(pieces, spread_ref[...]) + ones_ref[...]).astype(BF16)


def _norm_and_forget_cumsum(x, gain, w_in_t, row_start, bias, seq):
    t, d = x.shape
    nh = bias.shape[0]
    ts = CUMSUM_TILE
    assert row_start % LANES == 0 and 3 * nh <= LANES
    width = nh * HEAD_DIM
    lane = jnp.arange(width)[None, :]
    piece_row = jnp.arange(LANES)[:, None]
    spread = jnp.where((piece_row < 3 * nh)
                       & (lane == (piece_row % nh) * HEAD_DIM + piece_row // nh), -1.0, 0.0)
    ones = jnp.where((lane % HEAD_DIM >= 3) & (lane % HEAD_DIM < 6), 1.0, 0.0).astype(F32)
    bias_tile = jnp.zeros((1, LANES), F32).at[0, 0:nh].set(bias)
    return pl.pallas_call(
        functools.partial(_forget_cumsum_kernel, tiles_per_seq=seq // ts),
        grid=(t // ts,),
        in_specs=[pl.BlockSpec((ts, d), lambda i: (i, 0)),
                  _gain_spec(d),
                  pl.BlockSpec((LANES, d), lambda i: (row_start // LANES, 0)),
                  pl.BlockSpec((1, LANES), lambda i: (0, 0)),
                  pl.BlockSpec((LANES, width), lambda i: (0, 0)),
                  pl.BlockSpec((1, width), lambda i: (0, 0))],
        out_specs=[pl.BlockSpec((ts, d), lambda i: (i, 0)),
                   pl.BlockSpec((ts, nh), lambda i: (i, 0)),
                   pl.BlockSpec((ts, width), lambda i: (i, 0))],
        out_shape=[jax.ShapeDtypeStruct((t, d), BF16), jax.ShapeDtypeStruct((t, nh), F32),
                   jax.ShapeDtypeStruct((t, width), BF16)],
        scratch_shapes=[pltpu.VMEM((1, LANES), F32), pltpu.VMEM((LANES, d), BF16)],
        compiler_params=_params(("arbitrary",), 48),
        name="norm_forget_cumsum",
    )(x, gain.reshape(1, d), w_in_t, bias_tile, spread.astype(BF16), ones)


AUG_ROWS = 16
SUM_ROWS = 16
HEADS_PER_STEP = 2


def _attention_kernel(q_ref, k_ref, v_ref, augk_ref, c_ref, o_ref,
                      kx_ref, vt_ref, qxt_ref, m_ref, *buffers, tile):
    n = HEADS_PER_STEP
    buffers = [buf.at[:, 0:tile] for buf in buffers]
    s_ref = [buffers[2 * g:2 * g + 2] for g in range(n)]
    p_ref = [buffers[2 * n + 2 * g:2 * n + 2 * g + 2] for g in range(n)]
    acc_ref = buffers[4 * n:]
    kt = tile // 2
    qi = pl.program_id(2)
    n_key_tiles = kx_ref.shape[1]
    heads = range(HEADS_PER_STEP)

    def lanes(g):
        return slice(g * HEAD_DIM, (g + 1) * HEAD_DIM)

    @pl.when(qi == 0)
    def _():
        def stage(j, _):
            rows = pl.ds(pl.multiple_of(j * kt, kt), kt)
            for g in heads:
                kx_ref[g, j, :, 0:HEAD_DIM] = k_ref[rows, lanes(g)]
                kx_ref[g, j, :, HEAD_DIM:] = augk_ref[rows, lanes(g)]
                vt_ref[g, j, 0:HEAD_DIM, :] = v_ref[rows, lanes(g)].T
                vt_ref[g, j, HEAD_DIM:, :] = jnp.ones((SUM_ROWS, kt), BF16)
            return 0
        lax.fori_loop(0, n_key_tiles, stage, 0)
        for g in heads:
            qxt_ref[g, HEAD_DIM + AUG_ROWS:, :] = jnp.zeros((HEAD_DIM - AUG_ROWS, tile), BF16)

    r = lax.broadcasted_iota(jnp.int32, (AUG_ROWS, tile), 0)
    for g in heads:
        qxt_ref[g, 0:HEAD_DIM, :] = q_ref[:, lanes(g)].T
        hi, mid, lo = _split_bf16x3(c_ref[g, qi] * LOG2_E)
        aug = jnp.where(r < 3, 1.0,
                        jnp.where(r == 3, hi.astype(F32),
                                  jnp.where(r == 4, mid.astype(F32),
                                            jnp.where(r == 5, lo.astype(F32), 0.0))))
        qxt_ref[g, HEAD_DIM:HEAD_DIM + AUG_ROWS, :] = aug.astype(BF16)

    def scores_into(g, slot, ki):
        s = _dot(kx_ref[g, ki], qxt_ref[g])
        s_ref[g][slot][...] = s
        return jnp.max(s, axis=0, keepdims=True)

    def softmax_into(g, slot, m, tile_max):
        m_new = jnp.maximum(m, tile_max)
        p_ref[g][slot][...] = jnp.exp2(s_ref[g][slot][...] - m_new).astype(BF16)
        return m_new, jnp.exp2(m - m_new)

    def accumulate(g, slot, alpha, ki):
        acc_ref[g][...] = alpha * acc_ref[g][...] + _dot(vt_ref[g, ki], p_ref[g][slot][...])

    for g in heads:
        p_ref[g][1][...] = jnp.zeros((kt, tile), BF16)
        acc_ref[g][...] = jnp.zeros(acc_ref[g].shape, F32)
    max_0 = tuple(scores_into(g, 0, 0) for g in heads)

    def pair(jj, carry):
        m, alpha_1, max_0 = carry
        k0 = 2 * jj
        max_1 = [scores_into(g, 1, k0 + 1) for g in heads]
        m, alpha_0 = zip(*[softmax_into(g, 0, m[g], max_0[g]) for g in heads])
        for g in heads:
            accumulate(g, 1, alpha_1[g], jnp.maximum(k0 - 1, 0))
        max_0 = tuple(scores_into(g, 0, k0 + 2) for g in heads)
        m, alpha_1 = zip(*[softmax_into(g, 1, m[g], max_1[g]) for g in heads])
        for g in heads:
            accumulate(g, 0, alpha_0[g], k0)
        return m, alpha_1, max_0

    init = (tuple(jnp.full((1, tile), NEG_INF, F32) for _ in heads),
            tuple(jnp.ones((1, tile), F32) for _ in heads), max_0)
    m, alpha_1, _ = lax.fori_loop(0, qi, pair, init)

    k0 = 2 * qi
    key = lax.broadcasted_iota(jnp.int32, (kt, tile), 0)
    query = lax.broadcasted_iota(jnp.int32, (kt, tile), 1)
    upper = slice(kt, tile)

    def upper_scores_into(g):
        s = _dot(kx_ref[g, k0 + 1], qxt_ref[g, :, upper])
        s = jnp.where(lax.broadcasted_iota(jnp.int32, (kt, kt), 0)
                      <= lax.broadcasted_iota(jnp.int32, (kt, kt), 1), s, NEG_INF)
        s_ref[g][1][:, 0:kt] = s
        return jnp.max(s, axis=0, keepdims=True)

    max_1 = [upper_scores_into(g) for g in heads]
    for g in heads:
        s_ref[g][0][...] = jnp.where(key <= query, s_ref[g][0][...], NEG_INF)
    max_0 = [jnp.max(s_ref[g][0][...], axis=0, keepdims=True) for g in heads]
    m, alpha_0 = zip(*[softmax_into(g, 0, m[g], max_0[g]) for g in heads])
    for g in heads:
        accumulate(g, 1, alpha_1[g], jnp.maximum(k0 - 1, 0))
    alpha_upper = []
    for g in heads:
        m_ref[g, 0:1, :] = m[g]
        m_old = m_ref[g, 0:1, upper]
        m_upper = jnp.maximum(m_old, max_1[g])
        alpha_upper.append(jnp.exp2(m_old - m_upper))
        p_ref[g][1][:, 0:kt] = jnp.exp2(s_ref[g][1][:, 0:kt] - m_upper).astype(BF16)
    for g in heads:
        accumulate(g, 0, alpha_0[g], k0)
    for g in heads:
        acc_ref[g][:, upper] = (alpha_upper[g] * acc_ref[g][:, upper]
                                + _dot(vt_ref[g, k0 + 1], p_ref[g][1][:, 0:kt]))
    for g in heads:
        o_t = acc_ref[g][0:HEAD_DIM, :] / acc_ref[g][HEAD_DIM:HEAD_DIM + 1, :]
        o_ref[:, lanes(g)] = o_t.T.astype(BF16)


def _attention(qkv, augk, c, batch, seq):
    t = qkv.shape[0]
    tile = ATTN_TILE
    kt = tile // 2
    nq = seq // tile
    g = HEADS_PER_STEP
    width = g * HEAD_DIM
    groups = N_HEADS // g
    c_rows = jnp.transpose(c.reshape(batch, seq, N_HEADS), (0, 2, 1)).reshape(
        batch * N_HEADS, nq, 1, tile)
    return pl.pallas_call(
        functools.partial(_attention_kernel, tile=tile),
        grid=(batch, groups, nq),
        in_specs=[
            pl.BlockSpec((tile, width), lambda b, h, i: (b * nq + i, h)),
            pl.BlockSpec((seq, width), lambda b, h, i: (b, groups + h)),
            pl.BlockSpec((seq, width), lambda b, h, i: (b, 2 * groups + h)),
            pl.BlockSpec((seq, width), lambda b, h, i: (b, h)),
            pl.BlockSpec((g, nq, 1, tile), lambda b, h, i: (b * groups + h, 0, 0, 0)),
        ],
        out_specs=pl.BlockSpec((tile, width), lambda b, h, i: (b * nq + i, h)),
        out_shape=jax.ShapeDtypeStruct((t, N_HEADS * HEAD_DIM), BF16),
        scratch_shapes=[pltpu.VMEM((g, seq // kt, kt, 2 * HEAD_DIM), BF16),
                        pltpu.VMEM((g, seq // kt, HEAD_DIM + SUM_ROWS, kt), BF16),
                        pltpu.VMEM((g, 2 * HEAD_DIM, tile), BF16),
                        pltpu.VMEM((g, 8, tile), F32),
                        *[pltpu.VMEM((kt, tile + LANES), F32)] * (2 * g),
                        *[pltpu.VMEM((kt, tile + LANES), BF16)] * (2 * g),
                        *[pltpu.VMEM((HEAD_DIM + SUM_ROWS, tile + LANES), F32)] * g],
        compiler_params=_params(("parallel", "parallel", "arbitrary"), 60),
        name="fox_attention",
    )(qkv, qkv, qkv, augk, c_rows)


def kernel(x, p, norm_mix_pre, w_in, forget_bias, conv_mix_w, w_branch_conv, w_branch_attn, w_out,
           norm_mix_post, norm_ffn_pre, w_up, ffn_conv_w, ffn_conv_b, w_down, norm_ffn_post,
           w_ple_proj, norm_ple_gate, w_ple_gate, norm_ple_post):
    batch, seq, d = x.shape
    t = batch * seq
    depth = w_in.shape[0]
    conv_width = conv_mix_w.shape[-1]
    attn_width = N_HEADS * HEAD_DIM
    qkv_start = 3 * conv_width
    forget_start = qkv_start + 3 * attn_width
    gate_start = forget_start + N_HEADS

    xs = x.reshape(t, d)
    for i in range(depth):
        w_in_t = jnp.transpose(w_in[i])

        h1, c, augk = _norm_and_forget_cumsum(xs, norm_mix_pre[i], w_in_t, forget_start,
                                              forget_bias[i], seq)
        a = _conv_branch(h1, w_in_t, conv_mix_w[i], seq)
        qkv = _qkv(h1, w_in_t, qkv_start, 3 * attn_width)
        gates, w_a, w_b = _gates(h1, w_in_t, gate_start, 2 * d, w_branch_conv[i], w_branch_attn[i])
        o = _attention(qkv, augk, c, batch, seq)
        merged = _merge(a, o, w_a, w_b, gates)
        y1 = _matmul(merged, w_out[i], ROW_TILE_MM, 512, "out_proj")
        x1, h2 = _residual_norm(xs, y1, norm_mix_post[i], norm_ffn_pre[i])

        ffn, w_down_bf16 = _ffn_up(h2, w_up[i], ffn_conv_w[i], ffn_conv_b[i], seq, w_down[i])
        y2 = _matmul(ffn, w_down_bf16, ROW_TILE_WIDE_K, 512, "ffn_down")
        x2, h3 = _residual_norm(x1, y2, norm_ffn_post[i], norm_ple_gate[i])

        ge = _ple(h3, p[i].reshape(t, -1), w_ple_gate[i], w_ple_proj[i])
        xs = _residual(x2, ge, norm_ple_post[i])
    return xs.reshape(batch, seq, d)
```
